```python
import jax, jax.numpy as jnp
from jax import lax
import numpy as np

D_MODEL = 2048
BATCH = 4
SEQ = 2048
DEPTH = 2

MEM_LEN = 256
GRID_W = 64
ROPE_THETA = 10000.0
EPS = 1e-6

N_BRANCH = 4
BRANCH_W = D_MODEL // 4

NA_HEADS = 4
NA_HD = BRANCH_W // NA_HEADS
NA_WIN_R = 8
NA_WIN_C = 16

MLA_HEADS = 4
MLA_NOPE = 128
MLA_ROPE = 64
MLA_V = BRANCH_W // MLA_HEADS
MLA_Q_RANK = (3 * D_MODEL) // 8
MLA_KV_RANK = D_MODEL // 8

SWA_HEADS = 8
SWA_KV_HEADS = 2
SWA_HD = BRANCH_W // SWA_HEADS
SWA_WINDOW = 128
SWA_BLOCK = 128

MEM_HEADS = 4
MEM_HD = BRANCH_W // MEM_HEADS

Q_BLOCK = 128

D_FF = 5632
N_EXPERTS = 8
TOP_K = 2
D_FF_EXPERT = 7168
N_DENSE = (DEPTH + 1) // 2
N_MOE = DEPTH // 2

SPLIT_SIZES = [NA_HEADS * NA_HD, NA_HEADS * NA_HD, NA_HEADS * NA_HD,
               MLA_Q_RANK, MLA_KV_RANK, MLA_ROPE,
               SWA_HEADS * SWA_HD, SWA_KV_HEADS * SWA_HD, SWA_KV_HEADS * SWA_HD,
               MEM_HEADS * MEM_HD]
IN_COLS = int(sum(SPLIT_SIZES))
SPLIT_POINTS = [int(c) for c in np.cumsum(SPLIT_SIZES)[:-1]]

kernel_name = "hybrid_gated_parallel_mixer_encoder"


def rmsnorm(x, g):
    xf = x.astype(jnp.float32)
    y = xf * lax.rsqrt(jnp.mean(xf * xf, axis=-1, keepdims=True) + EPS)
    return (y * g.astype(jnp.float32)).astype(x.dtype)


def rope(x, pos):
    d = x.shape[-1]
    half = d // 2
    freqs = ROPE_THETA ** (-2.0 * jnp.arange(half, dtype=jnp.float32) / d)
    ang = pos.astype(jnp.float32)[:, None] * freqs[None, :]
    cos = jnp.cos(ang)[:, None, :]
    sin = jnp.sin(ang)[:, None, :]
    xf = x.astype(jnp.float32)
    x1, x2 = xf[..., :half], xf[..., half:]
    return jnp.concatenate([x1 * cos - x2 * sin, x2 * cos + x1 * sin], axis=-1).astype(x.dtype)


def neighbourhood_attention(q, k, v, rpb):
    B, S, H, D = q.shape
    rows = S // GRID_W
    wr = min(NA_WIN_R, rows)
    qg = q.reshape(B, rows, GRID_W, H, D)
    kg = k.reshape(B, rows, GRID_W, H, D)
    vg = v.reshape(B, rows, GRID_W, H, D)
    col = jnp.arange(GRID_W)
    col_start = jnp.clip(col - NA_WIN_C // 2, 0, GRID_W - NA_WIN_C)
    col_idx = col_start[:, None] + jnp.arange(NA_WIN_C)[None, :]
    dc = col_idx - col[:, None] + (NA_WIN_C - 1)
    scale = D ** -0.5

    def row_block(args):
        r, q_r = args
        rs = jnp.clip(r - wr // 2, 0, rows - wr)
        k_rows = lax.dynamic_slice_in_dim(kg, rs, wr, axis=1)
        v_rows = lax.dynamic_slice_in_dim(vg, rs, wr, axis=1)
        k_nb = k_rows[:, :, col_idx]
        v_nb = v_rows[:, :, col_idx]
        s = jnp.einsum('bqhd,biqjhd->bhqij', q_r, k_nb).astype(jnp.float32) * scale
        dr = rs + jnp.arange(wr) - r + (NA_WIN_R - 1)
        bias = rpb[:, dr][:, :, dc].astype(jnp.float32)
        s = s + jnp.transpose(bias, (0, 2, 1, 3))[None]
        p = jax.nn.softmax(s.reshape(B, H, GRID_W, wr * NA_WIN_C), axis=-1)
        p = p.reshape(B, H, GRID_W, wr, NA_WIN_C).astype(v.dtype)
        return jnp.einsum('bhqij,biqjhd->bqhd', p, v_nb)

    o = lax.map(row_block, (jnp.arange(rows), jnp.transpose(qg, (1, 0, 2, 3, 4))))
    return jnp.transpose(o, (1, 0, 2, 3, 4)).reshape(B, S, H * D)


def dense_block_attention(q, k, v, scale):
    B, S, H, Dq = q.shape
    Dv = v.shape[-1]
    nb = S // Q_BLOCK
    qb = jnp.transpose(q.reshape(B, nb, Q_BLOCK, H, Dq), (1, 0, 2, 3, 4))

    def one(q_blk):
        s = jnp.einsum('bqhd,bkhd->bhqk', q_blk, k).astype(jnp.float32) * scale
        p = jax.nn.softmax(s, axis=-1).astype(v.dtype)
        return jnp.einsum('bhqk,bkhd->bqhd', p, v)

    o = lax.map(one, qb)
    return jnp.transpose(o, (1, 0, 2, 3, 4)).reshape(B, S, H * Dv)


def mla_branch(c_q, c_kv, k_pe, pos, g_cq, w_uq, g_ckv, w_ukv, g_qn, g_kn):
    B, S, _ = c_q.shape
    q = (rmsnorm(c_q, g_cq) @ w_uq).reshape(B, S, MLA_HEADS, MLA_NOPE + MLA_ROPE)
    kv = (rmsnorm(c_kv, g_ckv) @ w_ukv).reshape(B, S, MLA_HEADS, MLA_NOPE + MLA_V)
    k_nope, v = kv[..., :MLA_NOPE], kv[..., MLA_NOPE:]
    k_rope = jnp.broadcast_to(k_pe[:, :, None, :], (B, S, MLA_HEADS, MLA_ROPE))
    k = jnp.concatenate([k_nope, k_rope], axis=-1)
    q = rmsnorm(q, g_qn)
    k = rmsnorm(k, g_kn)
    q = jnp.concatenate([q[..., :MLA_NOPE], rope(q[..., MLA_NOPE:], pos)], axis=-1)
    k = jnp.concatenate([k[..., :MLA_NOPE], rope(k[..., MLA_NOPE:], pos)], axis=-1)
    return dense_block_attention(q, k, v, (MLA_NOPE + MLA_ROPE) ** -0.5)


def sliding_window_gqa(q, k, v, sink):
    B, S, Hq, D = q.shape
    Hkv = k.shape[2]
    G = Hq // Hkv
    nb = S // SWA_BLOCK
    padw = ((0, 0), (SWA_BLOCK, SWA_BLOCK), (0, 0), (0, 0))
    kb = jnp.pad(k, padw).reshape(B, nb + 2, SWA_BLOCK, Hkv, D)
    vb = jnp.pad(v, padw).reshape(B, nb + 2, SWA_BLOCK, Hkv, D)
    k_band = jnp.concatenate([kb[:, :-2], kb[:, 1:-1], kb[:, 2:]], axis=2)
    v_band = jnp.concatenate([vb[:, :-2], vb[:, 1:-1], vb[:, 2:]], axis=2)
    qb = q.reshape(B, nb, SWA_BLOCK, Hkv, G, D)
    s = jnp.einsum('bnqhgd,bnkhd->bnhgqk', qb, k_band).astype(jnp.float32) * (D ** -0.5)
    qpos = jnp.arange(nb)[:, None] * SWA_BLOCK + jnp.arange(SWA_BLOCK)[None, :]
    kpos = jnp.arange(nb)[:, None] * SWA_BLOCK - SWA_BLOCK + jnp.arange(3 * SWA_BLOCK)[None, :]
    valid = ((jnp.abs(qpos[:, :, None] - kpos[:, None, :]) <= SWA_WINDOW)
             & (kpos >= 0)[:, None, :] & (kpos < S)[:, None, :])
    s = jnp.where(valid[None, :, None, None], s, -1e30)
    sink_b = jnp.broadcast_to(sink.astype(jnp.float32).reshape(1, 1, Hkv, G, 1, 1),
                              s.shape[:-1] + (1,))
    p = jax.nn.softmax(jnp.concatenate([s, sink_b], axis=-1), axis=-1)[..., :-1]
    o = jnp.einsum('bnhgqk,bnkhd->bnqhgd', p.astype(v.dtype), v_band)
    return o.reshape(B, S, Hq * D)


def memory_attention(q, k, v):
    B, S, H, D = q.shape
    s = jnp.einsum('bshd,bmhd->bhsm', q, k).astype(jnp.float32) * (D ** -0.5)
    p = jax.nn.softmax(s, axis=-1).astype(v.dtype)
    return jnp.einsum('bhsm,bmhd->bshd', p, v).reshape(B, S, H * D)


def swiglu(x, w_up, w_down):
    g, u = jnp.split(x @ w_up, 2, axis=-1)
    return (jax.nn.silu(g) * u) @ w_down


def moe_swiglu(x, w_router, w_up, w_down):
    logits = (x @ w_router).astype(jnp.float32)
    top_v, top_i = lax.top_k(logits, TOP_K)
    top_w = jax.nn.softmax(top_v, axis=-1)
    combine = jnp.sum(jax.nn.one_hot(top_i, N_EXPERTS, dtype=jnp.float32) * top_w[..., None], axis=-2)
    combine = combine.astype(x.dtype)
    out = jnp.zeros_like(x)
    for e in range(N_EXPERTS):
        out = out + combine[..., e:e + 1] * swiglu(x, w_up[e], w_down[e])
    return out


def setup_inputs(seed: int = 0) -> dict:
    key = jax.random.key(seed)
    ks = iter(jax.random.split(key, 40))
    f32 = jnp.float32

    def normal(shape, scale):
        return jax.random.normal(next(ks), shape, f32) * scale

    def gain(shape):
        return 1.0 + normal(shape, 0.02)

    D = D_MODEL
    return {
        "x": normal((BATCH, SEQ, D), 1.0),
        "mem": normal((BATCH, MEM_LEN, D), 1.0),
        "norm_mix": gain((DEPTH, D)),
        "w_in": normal((DEPTH, D, IN_COLS), D ** -0.5),
        "na_q_norm": gain((DEPTH, NA_HD)),
        "na_k_norm": gain((DEPTH, NA_HD)),
        "na_rpb": normal((DEPTH, NA_HEADS, 2 * NA_WIN_R - 1, 2 * NA_WIN_C - 1), 0.1),
        "mla_cq_norm": gain((DEPTH, MLA_Q_RANK)),
        "mla_w_uq": normal((DEPTH, MLA_Q_RANK, MLA_HEADS * (MLA_NOPE + MLA_ROPE)), MLA_Q_RANK ** -0.5),
        "mla_ckv_norm": gain((DEPTH, MLA_KV_RANK)),
        "mla_w_ukv": normal((DEPTH, MLA_KV_RANK, MLA_HEADS * (MLA_NOPE + MLA_V)), MLA_KV_RANK ** -0.5),
        "mla_q_norm": gain((DEPTH, MLA_NOPE + MLA_ROPE)),
        "mla_k_norm": gain((DEPTH, MLA_NOPE + MLA_ROPE)),
        "swa_q_norm": gain((DEPTH, SWA_HD)),
        "swa_k_norm": gain((DEPTH, SWA_HD)),
        "swa_sink": normal((DEPTH, SWA_HEADS), 0.5),
        "mem_norm": gain((DEPTH, D)),
        "mem_w_kv": normal((DEPTH, D, 2 * MEM_HEADS * MEM_HD), D ** -0.5),
        "mem_q_norm": gain((DEPTH, MEM_HD)),
        "mem_k_norm": gain((DEPTH, MEM_HD)),
        "w_branch": normal((DEPTH, N_BRANCH, BRANCH_W, D), BRANCH_W ** -0.5),
        "w_gate": normal((DEPTH, D, N_BRANCH, D), D ** -0.5),
        "b_gate": normal((DEPTH, N_BRANCH, D), 0.1),
        "w_o": normal((DEPTH, D, D), (N_BRANCH * D) ** -0.5),
        "norm_ffn": gain((DEPTH, D)),
        "ffn_w_up": normal((N_DENSE, D, 2 * D_FF), D ** -0.5),
        "ffn_w_down": normal((N_DENSE, D_FF, D), D_FF ** -0.5),
        "moe_router": normal((N_MOE, D, N_EXPERTS), D ** -0.5),
        "moe_w_up": normal((N_MOE, N_EXPERTS, D, 2 * D_FF_EXPERT), D ** -0.5),
        "moe_w_down": normal((N_MOE, N_EXPERTS, D_FF_EXPERT, D), D_FF_EXPERT ** -0.5),
    }


def reference(x, mem, norm_mix, w_in, na_q_norm, na_k_norm, na_rpb,
              mla_cq_norm, mla_w_uq, mla_ckv_norm, mla_w_ukv, mla_q_norm, mla_k_norm,
              swa_q_norm, swa_k_norm, swa_sink,
              mem_norm, mem_w_kv, mem_q_norm, mem_k_norm,
              w_branch, w_gate, b_gate, w_o, norm_ffn,
              ffn_w_up, ffn_w_down, moe_router, moe_w_up, moe_w_down):
    B, S, _ = x.shape
    M = mem.shape[1]
    pos = jnp.arange(S)
    h = x
    for l in range(DEPTH):
        u = rmsnorm(h, norm_mix[l])
        z = u @ w_in[l]
        (na_q, na_k, na_v, c_q, c_kv, k_pe,
         sw_q, sw_k, sw_v, me_q) = jnp.split(z, SPLIT_POINTS, axis=-1)

        qa = rmsnorm(na_q.reshape(B, S, NA_HEADS, NA_HD), na_q_norm[l])
        ka = rmsnorm(na_k.reshape(B, S, NA_HEADS, NA_HD), na_k_norm[l])
        va = na_v.reshape(B, S, NA_HEADS, NA_HD)
        o_na = neighbourhood_attention(qa, ka, va, na_rpb[l])

        o_mla = mla_branch(c_q, c_kv, k_pe, pos, mla_cq_norm[l], mla_w_uq[l],
                           mla_ckv_norm[l], mla_w_ukv[l], mla_q_norm[l], mla_k_norm[l])

        qc = rope(rmsnorm(sw_q.reshape(B, S, SWA_HEADS, SWA_HD), swa_q_norm[l]), pos)
        kc = rope(rmsnorm(sw_k.reshape(B, S, SWA_KV_HEADS, SWA_HD), swa_k_norm[l]), pos)
        vc = sw_v.reshape(B, S, SWA_KV_HEADS, SWA_HD)
        o_swa = sliding_window_gqa(qc, kc, vc, swa_sink[l])

        mkv = (rmsnorm(mem, mem_norm[l]) @ mem_w_kv[l]).reshape(B, M, 2, MEM_HEADS, MEM_HD)
        mk = rmsnorm(mkv[:, :, 0], mem_k_norm[l])
        mv = mkv[:, :, 1]
        mq = rmsnorm(me_q.reshape(B, S, MEM_HEADS, MEM_HD), mem_q_norm[l])
        o_mem = memory_attention(mq, mk, mv)

        br = jnp.stack([o_na, o_mla, o_swa, o_mem], axis=2)
        proj = jnp.einsum('bsnc,ncd->bsnd', br, w_branch[l])
        gate_logits = jnp.einsum('bsd,dne->bsne', u, w_gate[l]) + b_gate[l]
        gates = jax.nn.sigmoid(gate_logits.astype(jnp.float32)).astype(proj.dtype)
        merged = jnp.einsum('bsnd,bsnd->bsd', gates, proj)
        h = h + merged @ w_o[l]

        hn = rmsnorm(h, norm_ffn[l])
        if l % 2 == 0:
            h = h + swiglu(hn, ffn_w_up[l // 2], ffn_w_down[l // 2])
        else:
            h = h + moe_swiglu(hn, moe_router[l // 2], moe_w_up[l // 2], moe_w_down[l // 2])
    return h
```

```python
import functools

import jax
import jax.numpy as jnp
import numpy as np
from jax import lax
from jax.experimental import pallas as pl
from jax.experimental.pallas import tpu as pltpu

F32 = jnp.float32
BF16 = jnp.bfloat16

D_MODEL = 2048
SEQ = 2048
MEM_LEN = 256
GRID_W = 64
ROPE_THETA = 10000.0
EPS = 1e-6
NEG = -1e30

NA_HEADS = 4
NA_HD = 128
NA_WIN_R = 8
NA_WIN_C = 16
NA_QBLK = 128
NA_KWIN = 640

MLA_HEADS = 4
MLA_NOPE = 128
MLA_ROPE = 64
MLA_QK = MLA_NOPE + MLA_ROPE
MLA_PAD = 256
MLA_Q_RANK = 768
MLA_KV_RANK = 256

SWA_HEADS = 8
SWA_KV_HEADS = 2
SWA_HD = 64
SWA_BLOCK = 128

MEM_HEADS = 4
MEM_HD = 128

D_FF = 5632
N_EXPERTS = 8
D_FF_EXPERT = 7168
MOE_TM = 512

LANES = 128
VMEM_CAP = 60000 * 1024

COL_NA_Q, COL_NA_K, COL_NA_V = 0, 512, 1024
COL_CQ, COL_CKV = 1536, 2304
COL_SWQ, COL_MEQ, COL_SWK, COL_SWV, COL_KPE = 2560, 3072, 3584, 3712, 3840
IN_COLS_PAD = 4096


def _params(n_axes, vmem_bytes):
    return pltpu.CompilerParams(
        dimension_semantics=("arbitrary",) * n_axes,
        vmem_limit_bytes=int(min(VMEM_CAP, vmem_bytes)))


def _rms(x, g):
    ms = jnp.mean(x * x, axis=-1, keepdims=True)
    return x * lax.rsqrt(ms + EPS) * g


def _sigmoid(x):
    return 1.0 / (1.0 + jnp.exp(-x))


def _cast_rows(src_ref, dst_ref, rows, chunk=256):
    def body(c, carry):
        r = pl.multiple_of(c * chunk, chunk)
        dst_ref[pl.ds(r, chunk), :] = src_ref[pl.ds(r, chunk), :].astype(BF16)
        return carry
    lax.fori_loop(0, rows // chunk, body, 0)


def _dot(a, b):
    return jnp.dot(a, b, preferred_element_type=F32)


def _dot_nt(a, b):
    return lax.dot_general(a, b, (((1,), (1,)), ((), ())), preferred_element_type=F32)


def _rmsnorm_kernel(x_ref, g_ref, o_ref):
    o_ref[...] = _rms(x_ref[...], g_ref[...]).astype(o_ref.dtype)


def rmsnorm_bf16(x, g, tm=512):
    m, d = x.shape
    return pl.pallas_call(
        _rmsnorm_kernel,
        grid=(m // tm,),
        in_specs=[pl.BlockSpec((tm, d), lambda i: (i, 0)),
                  pl.BlockSpec((1, d), lambda i: (0, 0))],
        out_specs=pl.BlockSpec((tm, d), lambda i: (i, 0)),
        out_shape=jax.ShapeDtypeStruct((m, d), BF16),
        compiler_params=_params(1, 4 * tm * d * 6 + (8 << 20)),
        name="rmsnorm_bf16",
    )(x, g.reshape(1, d))


def _ws_plain_kernel(x_ref, w_ref, o_ref, wb_ref):
    @pl.when(pl.program_id(1) == 0)
    def _():
        _cast_rows(w_ref, wb_ref, w_ref.shape[0])
    o_ref[...] = _dot(x_ref[...], wb_ref[...]).astype(o_ref.dtype)


def _ws_residual_kernel(x_ref, w_ref, r_ref, o_ref, wb_ref):
    @pl.when(pl.program_id(1) == 0)
    def _():
        _cast_rows(w_ref, wb_ref, w_ref.shape[0])
    o_ref[...] = r_ref[...] + _dot(x_ref[...], wb_ref[...])


def _ws_swiglu_kernel(x_ref, wg_ref, wu_ref, o_ref, wgb_ref, wub_ref):
    @pl.when(pl.program_id(1) == 0)
    def _():
        _cast_rows(wg_ref, wgb_ref, wg_ref.shape[0])
        _cast_rows(wu_ref, wub_ref, wu_ref.shape[0])
    x = x_ref[...]
    g = _dot(x, wgb_ref[...])
    u = _dot(x, wub_ref[...])
    o_ref[...] = (g * _sigmoid(g) * u).astype(o_ref.dtype)


def ws_matmul(x, w, *, tm, tn, out_dtype, residual=None, name):
    m, k = x.shape
    n = w.shape[1]
    in_specs = [pl.BlockSpec((tm, k), lambda j, i: (i, 0)),
                pl.BlockSpec((k, tn), lambda j, i: (0, j))]
    args = [x, w]
    kern = _ws_plain_kernel
    if residual is not None:
        in_specs.append(pl.BlockSpec((tm, tn), lambda j, i: (i, j)))
        args.append(residual)
        kern = _ws_residual_kernel
    vmem = 2 * (tm * k * 2 + k * tn * 4 + 2 * tm * tn * 4) + k * tn * 2 + tm * tn * 4 + (6 << 20)
    return pl.pallas_call(
        kern,
        grid=(n // tn, m // tm),
        in_specs=in_specs,
        out_specs=pl.BlockSpec((tm, tn), lambda j, i: (i, j)),
        out_shape=jax.ShapeDtypeStruct((m, n), out_dtype),
        scratch_shapes=[pltpu.VMEM((k, tn), BF16)],
        compiler_params=_params(2, vmem),
        name=name,
    )(*args)


def ws_swiglu(x, w_up, d_ff, *, tm, tn, name):
    m, k = x.shape
    nb = d_ff // tn
    vmem = 2 * (tm * k * 2 + 2 * k * tn * 4 + tm * tn * 2) + 2 * k * tn * 2 + 3 * tm * tn * 4 + (6 << 20)
    return pl.pallas_call(
        _ws_swiglu_kernel,
        grid=(nb, m // tm),
        in_specs=[pl.BlockSpec((tm, k), lambda j, i: (i, 0)),
                  pl.BlockSpec((k, tn), lambda j, i: (0, j)),
                  pl.BlockSpec((k, tn), lambda j, i: (0, j + nb))],
        out_specs=pl.BlockSpec((tm, tn), lambda j, i: (i, j)),
        out_shape=jax.ShapeDtypeStruct((m, d_ff), BF16),
        scratch_shapes=[pltpu.VMEM((k, tn), BF16), pltpu.VMEM((k, tn), BF16)],
        compiler_params=_params(2, vmem),
        name=name,
    )(x, w_up, w_up)


def _na_bias_table(rpb):
    rows = SEQ // GRID_W
    nblk = SEQ // NA_QBLK
    pairs = NA_KWIN // NA_QBLK
    blk = np.arange(nblk)
    kstart = np.clip(blk - 2, 0, nblk - pairs)
    ql = np.arange(NA_QBLK)
    qr = 2 * blk[:, None] + ql[None, :] // GRID_W
    qc = ql % GRID_W
    kl = np.arange(NA_KWIN)
    kr = 2 * kstart[:, None] + kl[None, :] // GRID_W
    kc = kl % GRID_W
    rs = np.clip(qr - NA_WIN_R // 2, 0, rows - NA_WIN_R)
    cs = np.clip(qc - NA_WIN_C // 2, 0, GRID_W - NA_WIN_C)
    krb = kr[:, None, :]
    kcb = kc[None, None, :]
    valid = ((krb >= rs[:, :, None]) & (krb < rs[:, :, None] + NA_WIN_R)
             & (kcb >= cs[None, :, None]) & (kcb < cs[None, :, None] + NA_WIN_C))
    dr = np.clip(krb - qr[:, :, None] + (NA_WIN_R - 1), 0, 2 * NA_WIN_R - 2)
    dc = np.clip(kcb - qc[None, :, None] + (NA_WIN_C - 1), 0, 2 * NA_WIN_C - 2)
    dr, dc = np.broadcast_arrays(dr, dc)
    bias = rpb[:, dr, dc]
    bias = jnp.where(valid[None], bias, NEG)
    return jnp.transpose(bias, (1, 0, 2, 3))


def _na_kernel(q_ref, k_ref, v_ref, bias_ref, gq_ref, gk_ref, o_ref, kn_ref, vb_ref):
    i = pl.program_id(1)
    nblk = SEQ // NA_QBLK

    @pl.when(i == 0)
    def _():
        def body(c, carry):
            r = pl.multiple_of(c * 256, 256)
            for h in range(NA_HEADS):
                hs = slice(h * NA_HD, (h + 1) * NA_HD)
                kn_ref[pl.ds(r, 256), hs] = _rms(k_ref[pl.ds(r, 256), hs], gk_ref[...]).astype(BF16)
            vb_ref[pl.ds(r, 256), :] = v_ref[pl.ds(r, 256), :].astype(BF16)
            return carry
        lax.fori_loop(0, SEQ // 256, body, 0)

    start = pl.multiple_of(jnp.clip(i - 2, 0, nblk - NA_KWIN // NA_QBLK) * NA_QBLK, NA_QBLK)
    scale = NA_HD ** -0.5
    for h in range(NA_HEADS):
        hs = slice(h * NA_HD, (h + 1) * NA_HD)
        q = (_rms(q_ref[:, hs], gq_ref[...]) * scale).astype(BF16)
        s = _dot_nt(q, kn_ref[pl.ds(start, NA_KWIN), hs]) + bias_ref[0, h]
        m = jnp.max(s, axis=-1, keepdims=True)
        p = jnp.exp(s - m)
        l = jnp.sum(p, axis=-1, keepdims=True)
        o = _dot(p.astype(BF16), vb_ref[pl.ds(start, NA_KWIN), hs]) / l
        o_ref[:, hs] = o.astype(o_ref.dtype)


def na_attention(z, bias, gq, gk, batch):
    nblk = SEQ // NA_QBLK
    w = NA_HEADS * NA_HD
    vmem = (2 * (NA_QBLK * w * 4 + 2 * SEQ * w * 4 + NA_HEADS * NA_QBLK * NA_KWIN * 4 + NA_QBLK * w * 2)
            + 2 * SEQ * w * 2 + (8 << 20))
    return pl.pallas_call(
        _na_kernel,
        grid=(batch, nblk),
        in_specs=[pl.BlockSpec((NA_QBLK, w), lambda b, i: (b * nblk + i, COL_NA_Q // w)),
                  pl.BlockSpec((SEQ, w), lambda b, i: (b, COL_NA_K // w)),
                  pl.BlockSpec((SEQ, w), lambda b, i: (b, COL_NA_V // w)),
                  pl.BlockSpec((1, NA_HEADS, NA_QBLK, NA_KWIN), lambda b, i: (i, 0, 0, 0)),
                  pl.BlockSpec((1, NA_HD), lambda b, i: (0, 0)),
                  pl.BlockSpec((1, NA_HD), lambda b, i: (0, 0))],
        out_specs=pl.BlockSpec((NA_QBLK, w), lambda b, i: (b * nblk + i, 0)),
        out_shape=jax.ShapeDtypeStruct((batch * SEQ, w), BF16),
        scratch_shapes=[pltpu.VMEM((SEQ, w), BF16), pltpu.VMEM((SEQ, w), BF16)],
        compiler_params=_params(2, vmem),
        name="na_attention",
    )(z, z, z, bias, gq.reshape(1, NA_HD), gk.reshape(1, NA_HD))


def _rope_tables(width, head_dim, first_lane):
    half = head_dim // 2
    freqs = ROPE_THETA ** (-2.0 * np.arange(half, dtype=np.float32) / head_dim)
    ang = jnp.arange(SEQ, dtype=F32)[:, None] * jnp.asarray(freqs, F32)[None, :]
    cos, sin = jnp.cos(ang), jnp.sin(ang)
    lane = np.arange(width)
    rel = lane - first_lane
    in_rope = (rel >= 0) & (rel < (width - first_lane if first_lane == 0 else head_dim))
    p = np.where(in_rope, rel % head_dim, 0)
    j = p % half
    first_half = in_rope & (p < half)
    second_half = in_rope & (p >= half)
    cos_t = jnp.where(in_rope[None, :], cos[:, j], 1.0)
    sin_a = jnp.where(first_half[None, :], -sin[:, j], 0.0)
    sin_b = jnp.where(second_half[None, :], sin[:, j], 0.0)
    return cos_t, sin_a, sin_b


def _apply_rope(x, cos_t, sin_a, sin_b, half):
    n = x.shape[-1]
    return (x * cos_t + pltpu.roll(x, n - half, axis=1) * sin_a
            + pltpu.roll(x, half, axis=1) * sin_b)


def _mla_prep_kernel(cq_ref, ckv_ref, kpe_ref, wq_ref, wkv_ref, gcq_ref, gckv_ref,
                     gqn_ref, gkn_ref, cos_ref, sa_ref, sb_ref, q_ref, k_ref, v_ref):
    half = MLA_ROPE // 2
    cos_t, sin_a, sin_b = cos_ref[...], sa_ref[...], sb_ref[...]
    scale = MLA_QK ** -0.5
    cq = _rms(cq_ref[...], gcq_ref[...]).astype(BF16)
    q_raw = _dot(cq, wq_ref[...])
    ckv = _rms(ckv_ref[...], gckv_ref[...]).astype(BF16)
    kv_raw = _dot(ckv, wkv_ref[...])
    kpe = kpe_ref[...]
    kpe_ss = jnp.sum(kpe * kpe, axis=-1, keepdims=True)
    for h in range(MLA_HEADS):
        cs = slice(h * MLA_PAD, (h + 1) * MLA_PAD)
        qc = q_raw[:, cs]
        ms = jnp.sum(qc * qc, axis=-1, keepdims=True) * (1.0 / MLA_QK)
        qn = qc * lax.rsqrt(ms + EPS) * gqn_ref[...]
        q_ref[:, cs] = (_apply_rope(qn, cos_t, sin_a, sin_b, half) * scale).astype(BF16)
        kn = kv_raw[:, h * MLA_PAD:h * MLA_PAD + MLA_NOPE]
        ms = (jnp.sum(kn * kn, axis=-1, keepdims=True) + kpe_ss) * (1.0 / MLA_QK)
        kc = jnp.concatenate([kn, kpe], axis=-1) * lax.rsqrt(ms + EPS) * gkn_ref[...]
        k_ref[:, cs] = _apply_rope(kc, cos_t, sin_a, sin_b, half).astype(BF16)
        v_ref[:, h * MLA_NOPE:(h + 1) * MLA_NOPE] = kv_raw[:, h * MLA_PAD + MLA_NOPE:(h + 1) * MLA_PAD].astype(BF16)


def mla_prep(z, wq_pad, wkv, gcq, gckv, gqn_pad, gkn_pad, tabs, tm=512):
    t = z.shape[0]
    sb = SEQ // tm
    qw = MLA_HEADS * MLA_PAD
    row = lambda i: (i, 0)
    const = lambda i: (0, 0)
    pos = lambda i: (i % sb, 0)
    return pl.pallas_call(
        _mla_prep_kernel,
        grid=(t // tm,),
        in_specs=[pl.BlockSpec((tm, MLA_Q_RANK), lambda i: (i, COL_CQ // MLA_Q_RANK)),
                  pl.BlockSpec((tm, MLA_KV_RANK), lambda i: (i, COL_CKV // MLA_KV_RANK)),
                  pl.BlockSpec((tm, LANES), lambda i: (i, COL_KPE // LANES)),
                  pl.BlockSpec((MLA_Q_RANK, qw), const),
                  pl.BlockSpec((MLA_KV_RANK, qw), const),
                  pl.BlockSpec((1, MLA_Q_RANK), const),
                  pl.BlockSpec((1, MLA_KV_RANK), const),
                  pl.BlockSpec((1, MLA_PAD), const),
                  pl.BlockSpec((1, MLA_PAD), const),
                  pl.BlockSpec((tm, MLA_PAD), pos),
                  pl.BlockSpec((tm, MLA_PAD), pos),
                  pl.BlockSpec((tm, MLA_PAD), pos)],
        out_specs=[pl.BlockSpec((tm, qw), row), pl.BlockSpec((tm, qw), row),
                   pl.BlockSpec((tm, MLA_HEADS * MLA_NOPE), row)],
        out_shape=[jax.ShapeDtypeStruct((t, qw), BF16), jax.ShapeDtypeStruct((t, qw), BF16),
                   jax.ShapeDtypeStruct((t, MLA_HEADS * MLA_NOPE), BF16)],
        compiler_params=_params(1, 40 << 20),
        name="mla_prep",
    )(z, z, z, wq_pad, wkv, gcq.reshape(1, -1), gckv.reshape(1, -1), gqn_pad, gkn_pad, *tabs)


def _mla_attn_kernel(q_ref, k_ref, v_ref, o_ref):
    for h in range(MLA_HEADS):
        cs = slice(h * MLA_PAD, (h + 1) * MLA_PAD)
        vs = slice(h * MLA_NOPE, (h + 1) * MLA_NOPE)
        s = _dot_nt(q_ref[:, cs], k_ref[:, cs])
        m = jnp.max(s, axis=-1, keepdims=True)
        p = jnp.exp(s - m)
        l = jnp.sum(p, axis=-1, keepdims=True)
        o = _dot(p.astype(BF16), v_ref[:, vs]) / l
        o_ref[:, vs] = o.astype(o_ref.dtype)


def mla_attention(q, k, v, batch, tq=256):
    nq = SEQ // tq
    qw = MLA_HEADS * MLA_PAD
    vw = MLA_HEADS * MLA_NOPE
    vmem = 2 * (tq * qw * 2 + SEQ * qw * 2 + SEQ * vw * 2 + tq * vw * 2) + 4 * tq * SEQ * 4 + (8 << 20)
    return pl.pallas_call(
        _mla_attn_kernel,
        grid=(batch, nq),
        in_specs=[pl.BlockSpec((tq, qw), lambda b, i: (b * nq + i, 0)),
                  pl.BlockSpec((SEQ, qw), lambda b, i: (b, 0)),
                  pl.BlockSpec((SEQ, vw), lambda b, i: (b, 0))],
        out_specs=pl.BlockSpec((tq, vw), lambda b, i: (b * nq + i, 0)),
        out_shape=jax.ShapeDtypeStruct((batch * SEQ, vw), BF16),
        compiler_params=_params(2, vmem),
        name="mla_attention",
    )(q, k, v)


def _rms_halves(x, g):
    lo = lax.broadcasted_iota(jnp.int32, x.shape, 1) < SWA_HD
    x2 = x * x
    s_lo = jnp.sum(jnp.where(lo, x2, 0.0), axis=-1, keepdims=True)
    s_hi = jnp.sum(jnp.where(lo, 0.0, x2), axis=-1, keepdims=True)
    ms = jnp.where(lo, s_lo, s_hi) * (1.0 / SWA_HD)
    return x * lax.rsqrt(ms + EPS) * g


def _swa_kernel(sink_ref, q_ref, k_ref, v_ref, gq_ref, gk_ref, cosq_ref, saq_ref, sbq_ref,
                cosk_ref, sak_ref, sbk_ref, o_ref, kk_ref, vv_ref):
    n = pl.program_id(1)
    nblk = SEQ // SWA_BLOCK
    half = SWA_HD // 2

    @pl.when(n == 0)
    def _():
        zeros = jnp.zeros((SWA_BLOCK, LANES), BF16)
        for c in range(4):
            kk_ref[c, pl.ds(0, SWA_BLOCK), :] = zeros
            kk_ref[c, pl.ds(SEQ + SWA_BLOCK, SWA_BLOCK), :] = zeros
            vv_ref[c, pl.ds(0, SWA_BLOCK), :] = zeros
            vv_ref[c, pl.ds(SEQ + SWA_BLOCK, SWA_BLOCK), :] = zeros

        def body(c, carry):
            r = pl.multiple_of(c * 256, 256)
            dst = pl.ds(r + SWA_BLOCK, 256)
            lo = lax.broadcasted_iota(jnp.int32, (256, LANES), 1) < SWA_HD
            kr = _apply_rope(_rms_halves(k_ref[pl.ds(r, 256), :], gk_ref[...]),
                             cosk_ref[pl.ds(r, 256), :], sak_ref[pl.ds(r, 256), :],
                             sbk_ref[pl.ds(r, 256), :], half)
            ks = pltpu.roll(kr, SWA_HD, axis=1)
            kk_ref[0, dst, :] = jnp.where(lo, kr, 0.0).astype(BF16)
            kk_ref[1, dst, :] = jnp.where(lo, 0.0, ks).astype(BF16)
            kk_ref[2, dst, :] = jnp.where(lo, ks, 0.0).astype(BF16)
            kk_ref[3, dst, :] = jnp.where(lo, 0.0, kr).astype(BF16)
            vr = v_ref[pl.ds(r, 256), :]
            vs = pltpu.roll(vr, SWA_HD, axis=1)
            vv_ref[0, dst, :] = jnp.where(lo, vr, 0.0).astype(BF16)
            vv_ref[1, dst, :] = jnp.where(lo, 0.0, vs).astype(BF16)
            vv_ref[2, dst, :] = jnp.where(lo, vs, 0.0).astype(BF16)
            vv_ref[3, dst, :] = jnp.where(lo, 0.0, vr).astype(BF16)
            return carry
        lax.fori_loop(0, SEQ // 256, body, 0)

    band = pl.ds(pl.multiple_of(n * SWA_BLOCK, SWA_BLOCK), 3 * SWA_BLOCK)
    a = lax.broadcasted_iota(jnp.int32, (SWA_BLOCK, 3 * SWA_BLOCK), 0)
    c = lax.broadcasted_iota(jnp.int32, (SWA_BLOCK, 3 * SWA_BLOCK), 1)
    c_min = jnp.where(n == 0, SWA_BLOCK, 0)
    c_max = jnp.where(n == nblk - 1, 2 * SWA_BLOCK, 3 * SWA_BLOCK)
    valid = (c >= a) & (c <= a + 2 * SWA_BLOCK) & (c >= c_min) & (c < c_max)
    scale = SWA_HD ** -0.5
    for pair in range(SWA_HEADS // 2):
        ps = slice(pair * LANES, (pair + 1) * LANES)
        grp = pair // 2
        q = _apply_rope(_rms_halves(q_ref[:, ps], gq_ref[...]),
                        cosq_ref[...], saq_ref[...], sbq_ref[...], half)
        q = (q * scale).astype(BF16)
        acc = None
        for hf in range(2):
            sink = sink_ref[2 * pair + hf]
            s = _dot_nt(q, kk_ref[2 * grp + hf, band, :])
            s = jnp.where(valid, s, NEG)
            m = jnp.maximum(jnp.max(s, axis=-1, keepdims=True), sink)
            p = jnp.exp(s - m)
            den = jnp.sum(p, axis=-1, keepdims=True) + jnp.exp(sink - m)
            o = _dot(p.astype(BF16), vv_ref[2 * grp + hf, band, :]) / den
            acc = o if acc is None else acc + o
        o_ref[:, ps] = acc.astype(o_ref.dtype)


def swa_attention(z, sink, gq, gk, tabs, batch):
    nblk = SEQ // SWA_BLOCK
    qw = SWA_HEADS * SWA_HD
    cos_t, sin_a, sin_b = tabs
    g2 = lambda g: jnp.concatenate([g, g]).reshape(1, LANES)
    blk = lambda b, n: (n, 0)
    full = lambda b, n: (0, 0)
    tab_blk = pl.BlockSpec((SWA_BLOCK, LANES), blk)
    tab_full = pl.BlockSpec((SEQ, LANES), full)
    vmem = (2 * (SWA_BLOCK * qw * 6 + 2 * SEQ * LANES * 4 + 3 * SEQ * LANES * 4 + 3 * SWA_BLOCK * LANES * 4)
            + 8 * (SEQ + 2 * SWA_BLOCK) * LANES * 2 + (8 << 20))
    return pl.pallas_call(
        _swa_kernel,
        grid=(batch, nblk),
        in_specs=[pl.BlockSpec(memory_space=pltpu.SMEM),
                  pl.BlockSpec((SWA_BLOCK, qw), lambda b, n: (b * nblk + n, COL_SWQ // qw)),
                  pl.BlockSpec((SEQ, LANES), lambda b, n: (b, COL_SWK // LANES)),
                  pl.BlockSpec((SEQ, LANES), lambda b, n: (b, COL_SWV // LANES)),
                  pl.BlockSpec((1, LANES), full), pl.BlockSpec((1, LANES), full),
                  tab_blk, tab_blk, tab_blk, tab_full, tab_full, tab_full],
        out_specs=pl.BlockSpec((SWA_BLOCK, qw), lambda b, n: (b * nblk + n, 0)),
        out_shape=jax.ShapeDtypeStruct((batch * SEQ, qw), BF16),
        scratch_shapes=[pltpu.VMEM((4, SEQ + 2 * SWA_BLOCK, LANES), BF16),
                        pltpu.VMEM((4, SEQ + 2 * SWA_BLOCK, LANES), BF16)],
        compiler_params=_params(2, vmem),
        name="swa_attention",
    )(sink, z, z, z, g2(gq), g2(gk), cos_t, sin_a, sin_b, cos_t, sin_a, sin_b)


def _mem_attn_kernel(q_ref, k_ref, v_ref, gq_ref, gk_ref, o_ref):
    scale = MEM_HD ** -0.5
    for h in range(MEM_HEADS):
        hs = slice(h * MEM_HD, (h + 1) * MEM_HD)
        q = (_rms(q_ref[:, hs], gq_ref[...]) * scale).astype(BF16)
        k = _rms(k_ref[:, hs], gk_ref[...]).astype(BF16)
        s = _dot_nt(q, k)
        m = jnp.max(s, axis=-1, keepdims=True)
        p = jnp.exp(s - m)
        l = jnp.sum(p, axis=-1, keepdims=True)
        o = _dot(p.astype(BF16), v_ref[:, hs].astype(BF16)) / l
        o_ref[:, hs] = o.astype(o_ref.dtype)


def mem_attention(z, memkv, gq, gk, batch, tq=512):
    nq = SEQ // tq
    w = MEM_HEADS * MEM_HD
    return pl.pallas_call(
        _mem_attn_kernel,
        grid=(batch, nq),
        in_specs=[pl.BlockSpec((tq, w), lambda b, i: (b * nq + i, COL_MEQ // w)),
                  pl.BlockSpec((MEM_LEN, w), lambda b, i: (b, 0)),
                  pl.BlockSpec((MEM_LEN, w), lambda b, i: (b, 1)),
                  pl.BlockSpec((1, MEM_HD), lambda b, i: (0, 0)),
                  pl.BlockSpec((1, MEM_HD), lambda b, i: (0, 0))],
        out_specs=pl.BlockSpec((tq, w), lambda b, i: (b * nq + i, 0)),
        out_shape=jax.ShapeDtypeStruct((batch * SEQ, w), BF16),
        compiler_params=_params(2, 24 << 20),
        name="mem_attention",
    )(z, memkv, memkv, gq.reshape(1, MEM_HD), gk.reshape(1, MEM_HD))


def _merge_kernel(u_ref, b0_ref, b1_ref, b2_ref, b3_ref, wg0_ref, wg1_ref, wg2_ref, wg3_ref,
                  wb_ref, bg_ref, o_ref, wgs_ref, wbs_ref):
    branches = (b0_ref, b1_ref, b2_ref, b3_ref)
    gate_w = (wg0_ref, wg1_ref, wg2_ref, wg3_ref)

    @pl.when(pl.program_id(1) == 0)
    def _():
        for n in range(4):
            _cast_rows(gate_w[n], wgs_ref.at[n], D_MODEL)
            wbs_ref[n] = wb_ref[n].astype(BF16)

    u = u_ref[...]
    acc = None
    for n in range(4):
        gate = _sigmoid(_dot(u, wgs_ref[n]) + bg_ref[n:n + 1, :])
        term = gate * _dot(branches[n][...], wbs_ref[n])
        acc = term if acc is None else acc + term
    o_ref[...] = acc.astype(o_ref.dtype)


def gated_merge(u, branches, w_gate, w_branch, b_gate, tm=1024, tn=256):
    t = u.shape[0]
    bw = branches[0].shape[1]
    nb = D_MODEL // tn
    wg = w_gate.reshape(D_MODEL, 4 * D_MODEL)
    gate_specs = [pl.BlockSpec((D_MODEL, tn), functools.partial(lambda j, i, n: (0, n * nb + j), n=n))
                  for n in range(4)]
    vmem = (2 * (tm * D_MODEL * 2 + 4 * tm * bw * 2 + 4 * D_MODEL * tn * 4 + 4 * bw * tn * 4 + tm * tn * 2)
            + 4 * D_MODEL * tn * 2 + 4 * bw * tn * 2 + 4 * tm * tn * 4 + (6 << 20))
    return pl.pallas_call(
        _merge_kernel,
        grid=(nb, t // tm),
        in_specs=[pl.BlockSpec((tm, D_MODEL), lambda j, i: (i, 0))]
        + [pl.BlockSpec((tm, bw), lambda j, i: (i, 0)) for _ in range(4)]
        + gate_specs
        + [pl.BlockSpec((4, bw, tn), lambda j, i: (0, 0, j)),
           pl.BlockSpec((4, tn), lambda j, i: (0, j))],
        out_specs=pl.BlockSpec((tm, tn), lambda j, i: (i, j)),
        out_shape=jax.ShapeDtypeStruct((t, D_MODEL), BF16),
        scratch_shapes=[pltpu.VMEM((4, D_MODEL, tn), BF16), pltpu.VMEM((4, bw, tn), BF16)],
        compiler_params=_params(2, vmem),
        name="gated_merge",
    )(u, *branches, wg, wg, wg, wg, w_branch, b_gate)


def _router_kernel(h_ref, g_ref, wr_ref, ri_ref, rw_ref, cnt_ref, carry_ref):
    @pl.when(pl.program_id(0) == 0)
    def _():
        carry_ref[...] = jnp.zeros_like(carry_ref)

    tm = h_ref.shape[0]
    hn = _rms(h_ref[...], g_ref[...])
    logits = jnp.dot(hn, wr_ref[...], precision=lax.Precision.HIGHEST, preferred_element_type=F32)
    lane = lax.broadcasted_iota(jnp.int32, (tm, LANES), 1)
    lane_f = lane.astype(F32)
    logits = jnp.where(lane < N_EXPERTS, logits, -jnp.inf)
    m1 = jnp.max(logits, axis=-1, keepdims=True)
    i1 = jnp.min(jnp.where(logits == m1, lane_f, float(LANES)), axis=-1, keepdims=True)
    oh1 = lane_f == i1
    rest = jnp.where(oh1, -jnp.inf, logits)
    m2 = jnp.max(rest, axis=-1, keepdims=True)
    i2 = jnp.min(jnp.where(rest == m2, lane_f, float(LANES)), axis=-1, keepdims=True)
    oh2 = lane_f == i2
    e2 = jnp.exp(m2 - m1)
    w1 = 1.0 / (1.0 + e2)
    w2 = e2 / (1.0 + e2)
    chosen = jnp.where(oh1 | oh2, 1.0, 0.0)
    before = (lax.broadcasted_iota(jnp.int32, (tm, tm), 1)
              < lax.broadcasted_iota(jnp.int32, (tm, tm), 0))
    prefix = _dot(jnp.where(before, 1.0, 0.0).astype(BF16), chosen.astype(BF16)) + carry_ref[...]
    r1 = jnp.sum(jnp.where(oh1, prefix, 0.0), axis=-1, keepdims=True).astype(jnp.int32)
    r2 = jnp.sum(jnp.where(oh2, prefix, 0.0), axis=-1, keepdims=True).astype(jnp.int32)
    carry_ref[...] += jnp.sum(chosen, axis=0, keepdims=True)
    ri_ref[...] = jnp.where(lane == 0, i1.astype(jnp.int32),
                            jnp.where(lane == 1, i2.astype(jnp.int32),
                                      jnp.where(lane == 2, r1, jnp.where(lane == 3, r2, 0))))
    rw_ref[...] = jnp.where(lane == 0, w1, jnp.where(lane == 1, w2, 0.0))
    cnt_ref[...] = carry_ref[...]


def moe_router(h, g, w_router, tm=512):
    t, d = h.shape
    wr = jnp.pad(w_router, ((0, 0), (0, LANES - N_EXPERTS)))
    return pl.pallas_call(
        _router_kernel,
        grid=(t // tm,),
        in_specs=[pl.BlockSpec((tm, d), lambda i: (i, 0)),
                  pl.BlockSpec((1, d), lambda i: (0, 0)),
                  pl.BlockSpec((d, LANES), lambda i: (0, 0))],
        out_specs=[pl.BlockSpec((tm, LANES), lambda i: (i, 0)),
                   pl.BlockSpec((tm, LANES), lambda i: (i, 0)),
                   pl.BlockSpec((1, LANES), lambda i: (0, 0))],
        out_shape=[jax.ShapeDtypeStruct((t, LANES), jnp.int32),
                   jax.ShapeDtypeStruct((t, LANES), F32),
                   jax.ShapeDtypeStruct((1, LANES), F32)],
        scratch_shapes=[pltpu.VMEM((1, LANES), F32)],
        compiler_params=_params(1, 32 << 20),
        name="moe_router",
    )(h, g.reshape(1, d), wr)


def _row_copy(src_hbm, row, dst_vmem, r, sem):
    return pltpu.make_async_copy(src_hbm.at[pl.ds(row, 1), :], dst_vmem.at[pl.ds(r, 1), :], sem)


def _moe_gather_kernel(src_ref, h_hbm, g_ref, o_ref, buf_ref, sem):
    tg = buf_ref.shape[0]
    base = pl.program_id(0) * tg

    def issue(r, carry):
        _row_copy(h_hbm, src_ref[base + r], buf_ref, r, sem).start()
        return carry
    lax.fori_loop(0, tg, issue, 0, unroll=8)

    def drain(r, carry):
        _row_copy(h_hbm, 0, buf_ref, r, sem).wait()
        return carry
    lax.fori_loop(0, tg, drain, 0, unroll=8)
    o_ref[...] = _rms(buf_ref[...], g_ref[...]).astype(o_ref.dtype)


def moe_gather(src_tok, h, g, rows, tg=256):
    d = h.shape[1]
    return pl.pallas_call(
        _moe_gather_kernel,
        grid_spec=pltpu.PrefetchScalarGridSpec(
            num_scalar_prefetch=1,
            grid=(rows // tg,),
            in_specs=[pl.BlockSpec(memory_space=pl.ANY),
                      pl.BlockSpec((1, d), lambda i, s: (0, 0))],
            out_specs=pl.BlockSpec((tg, d), lambda i, s: (i, 0)),
            scratch_shapes=[pltpu.VMEM((tg, d), F32), pltpu.SemaphoreType.DMA(())]),
        out_shape=jax.ShapeDtypeStruct((rows, d), BF16),
        compiler_params=_params(1, 24 << 20),
        name="moe_gather",
    )(src_tok, h, g.reshape(1, d))


def _new_expert(te_ref, i):
    return jnp.logical_or(i == 0, te_ref[i] != te_ref[jnp.maximum(i - 1, 0)])


def _moe_up_kernel(te_ref, nt_ref, x_ref, wg_ref, wu_ref, o_ref, wgb_ref, wub_ref):
    i = pl.program_id(1)
    active = i < nt_ref[0]

    @pl.when(jnp.logical_and(active, _new_expert(te_ref, i)))
    def _():
        _cast_rows(wg_ref.at[0], wgb_ref, D_MODEL)
        _cast_rows(wu_ref.at[0], wub_ref, D_MODEL)

    @pl.when(active)
    def _():
        x = x_ref[...]
        g = _dot(x, wgb_ref[...])
        u = _dot(x, wub_ref[...])
        o_ref[...] = (g * _sigmoid(g) * u).astype(o_ref.dtype)

    @pl.when(jnp.logical_not(active))
    def _():
        o_ref[...] = jnp.zeros_like(o_ref)


def moe_up(xs, w_up, tile_e, n_tiles, tf=512):
    rows, d = xs.shape
    nf = D_FF_EXPERT // tf
    row = lambda j, i, te, nt: (jnp.minimum(i, nt[0] - 1), 0)
    vmem = 2 * (MOE_TM * d * 2 + 2 * d * tf * 4 + MOE_TM * tf * 2) + 2 * d * tf * 2 + 3 * MOE_TM * tf * 4 + (6 << 20)
    return pl.pallas_call(
        _moe_up_kernel,
        grid_spec=pltpu.PrefetchScalarGridSpec(
            num_scalar_prefetch=2,
            grid=(nf, rows // MOE_TM),
            in_specs=[pl.BlockSpec((MOE_TM, d), row),
                      pl.BlockSpec((1, d, tf), lambda j, i, te, nt: (te[i], 0, j)),
                      pl.BlockSpec((1, d, tf), lambda j, i, te, nt: (te[i], 0, j + nf))],
            out_specs=pl.BlockSpec((MOE_TM, tf), lambda j, i, te, nt: (i, j)),
            scratch_shapes=[pltpu.VMEM((d, tf), BF16), pltpu.VMEM((d, tf), BF16)]),
        out_shape=jax.ShapeDtypeStruct((rows, D_FF_EXPERT), BF16),
        compiler_params=_params(2, vmem),
        name="moe_up",
    )(tile_e, n_tiles, xs, w_up, w_up)


def _moe_down_kernel(te_ref, nt_ref, a_ref, w_ref, o_ref, wb_ref):
    i = pl.program_id(1)
    active = i < nt_ref[0]

    @pl.when(jnp.logical_and(active, _new_expert(te_ref, i)))
    def _():
        _cast_rows(w_ref.at[0], wb_ref, D_FF_EXPERT)

    @pl.when(active)
    def _():
        o_ref[...] = _dot(a_ref[...], wb_ref[...])

    @pl.when(jnp.logical_not(active))
    def _():
        o_ref[...] = jnp.zeros_like(o_ref)


def moe_down(act, w_down, tile_e, n_tiles, tn=256):
    rows, f = act.shape
    vmem = 2 * (MOE_TM * f * 2 + f * tn * 4 + MOE_TM * tn * 4) + f * tn * 2 + MOE_TM * tn * 4 + (6 << 20)
    return pl.pallas_call(
        _moe_down_kernel,
        grid_spec=pltpu.PrefetchScalarGridSpec(
            num_scalar_prefetch=2,
            grid=(D_MODEL // tn, rows // MOE_TM),
            in_specs=[pl.BlockSpec((MOE_TM, f), lambda j, i, te, nt: (jnp.minimum(i, nt[0] - 1), 0)),
                      pl.BlockSpec((1, f, tn), lambda j, i, te, nt: (te[i], 0, j))],
            out_specs=pl.BlockSpec((MOE_TM, tn), lambda j, i, te, nt: (i, j)),
            scratch_shapes=[pltpu.VMEM((f, tn), BF16)]),
        out_shape=jax.ShapeDtypeStruct((rows, D_MODEL), F32),
        compiler_params=_params(2, vmem),
        name="moe_down",
    )(tile_e, n_tiles, act, w_down)


def _moe_combine_kernel(pos_ref, h_ref, rw_ref, y_hbm, o_ref, b1_ref, b2_ref, sem):
    tc = b1_ref.shape[0]
    base = pl.program_id(0) * tc

    def issue(r, carry):
        t = base + r
        _row_copy(y_hbm, pos_ref[2 * t], b1_ref, r, sem).start()
        _row_copy(y_hbm, pos_ref[2 * t + 1], b2_ref, r, sem).start()
        return carry
    lax.fori_loop(0, tc, issue, 0, unroll=8)

    def drain(r, carry):
        _row_copy(y_hbm, 0, b1_ref, r, sem).wait()
        _row_copy(y_hbm, 0, b2_ref, r, sem).wait()
        return carry
    lax.fori_loop(0, tc, drain, 0, unroll=8)
    o_ref[...] = h_ref[...] + rw_ref[:, 0:1] * b1_ref[...] + rw_ref[:, 1:2] * b2_ref[...]


def moe_combine(pos_flat, h, rw, y, tc=256):
    t, d = h.shape
    return pl.pallas_call(
        _moe_combine_kernel,
        grid_spec=pltpu.PrefetchScalarGridSpec(
            num_scalar_prefetch=1,
            grid=(t // tc,),
            in_specs=[pl.BlockSpec((tc, d), lambda i, p: (i, 0)),
                      pl.BlockSpec((tc, LANES), lambda i, p: (i, 0)),
                      pl.BlockSpec(memory_space=pl.ANY)],
            out_specs=pl.BlockSpec((tc, d), lambda i, p: (i, 0)),
            scratch_shapes=[pltpu.VMEM((tc, d), F32), pltpu.VMEM((tc, d), F32),
                            pltpu.SemaphoreType.DMA(())]),
        out_shape=jax.ShapeDtypeStruct((t, d), F32),
        compiler_params=_params(1, 32 << 20),
        name="moe_combine",
    )(pos_flat, h, rw, y)


def moe_block(h, g, w_router, w_up, w_down):
    t = h.shape[0]
    rows = 2 * t + N_EXPERTS * MOE_TM
    n_row_tiles = rows // MOE_TM
    ri, rw, cnt = moe_router(h, g, w_router)
    counts = cnt[0, :N_EXPERTS].astype(jnp.int32)
    padded = ((counts + MOE_TM - 1) // MOE_TM) * MOE_TM
    ends = jnp.cumsum(padded)
    starts = ends - padded
    pos = starts[ri[:, 0:2]] + ri[:, 2:4]
    tok = jnp.broadcast_to(jnp.arange(t, dtype=jnp.int32)[:, None], (t, 2))
    src_tok = jnp.zeros((rows,), jnp.int32).at[pos.reshape(-1)].set(tok.reshape(-1))
    n_tiles = (ends[-1] // MOE_TM).astype(jnp.int32)
    tile_first_row = jnp.arange(n_row_tiles, dtype=jnp.int32) * MOE_TM
    tile_e = jnp.minimum(jnp.searchsorted(ends, tile_first_row, side="right"), N_EXPERTS - 1).astype(jnp.int32)
    tile_e = jnp.where(jnp.arange(n_row_tiles) < n_tiles, tile_e, tile_e[n_tiles - 1])
    nt = n_tiles.reshape(1)

    xs = moe_gather(src_tok, h, g, rows)
    act = moe_up(xs, w_up, tile_e, nt)
    y = moe_down(act, w_down, tile_e, nt)
    return moe_combine(pos.reshape(-1).astype(jnp.int32), h, rw, y)


def _pad_in_proj(w_in):
    segs = np.cumsum([0, 512, 512, 512, 768, 256, 64, 512, 128, 128, 512])
    na_q, na_k, na_v, c_q, c_kv, k_pe, sw_q, sw_k, sw_v, me_q = [
        w_in[:, segs[n]:segs[n + 1]] for n in range(10)]
    d = w_in.shape[0]
    return jnp.concatenate(
        [na_q, na_k, na_v, c_q, c_kv, sw_q, me_q, sw_k, sw_v, k_pe,
         jnp.zeros((d, IN_COLS_PAD - COL_KPE - MLA_ROPE), w_in.dtype)], axis=1)


def _pad_heads(x, n_heads, width):
    lead = x.shape[:-1]
    x = x.reshape(lead + (n_heads, width))
    x = jnp.pad(x, [(0, 0)] * len(lead) + [(0, 0), (0, MLA_PAD - width)])
    return x.reshape(lead + (n_heads * MLA_PAD,))


def kernel(x, mem, norm_mix, w_in, na_q_norm, na_k_norm, na_rpb, mla_cq_norm, mla_w_uq, mla_ckv_norm, mla_w_ukv, mla_q_norm, mla_k_norm, swa_q_norm, swa_k_norm, swa_sink, mem_norm, mem_w_kv, mem_q_norm, mem_k_norm, w_branch, w_gate, b_gate, w_o, norm_ffn, ffn_w_up, ffn_w_down, moe_router, moe_w_up, moe_w_down):
    batch, seq, d = x.shape
    assert (seq, d) == (SEQ, D_MODEL) and mem.shape[1] == MEM_LEN
    t = batch * seq
    depth = w_in.shape[0]
    h = x.reshape(t, d)
    mem2 = mem.reshape(batch * MEM_LEN, d)
    mla_tabs = _rope_tables(MLA_PAD, MLA_ROPE, MLA_NOPE)
    swa_tabs = _rope_tables(LANES, SWA_HD, 0)

    for l in range(depth):
        u = rmsnorm_bf16(h, norm_mix[l])
        z = ws_matmul(u, _pad_in_proj(w_in[l]), tm=1024, tn=512, out_dtype=F32, name="in_proj")

        o_na = na_attention(z, _na_bias_table(na_rpb[l]), na_q_norm[l], na_k_norm[l], batch)

        q_mla, k_mla, v_mla = mla_prep(
            z, _pad_heads(mla_w_uq[l], MLA_HEADS, MLA_QK).astype(BF16), mla_w_ukv[l].astype(BF16),
            mla_cq_norm[l], mla_ckv_norm[l],
            jnp.pad(mla_q_norm[l], (0, MLA_PAD - MLA_QK)).reshape(1, MLA_PAD),
            jnp.pad(mla_k_norm[l], (0, MLA_PAD - MLA_QK)).reshape(1, MLA_PAD), mla_tabs)
        o_mla = mla_attention(q_mla, k_mla, v_mla, batch)

        o_swa = swa_attention(z, swa_sink[l], swa_q_norm[l], swa_k_norm[l], swa_tabs, batch)

        memn = rmsnorm_bf16(mem2, mem_norm[l])
        memkv = ws_matmul(memn, mem_w_kv[l], tm=1024, tn=512, out_dtype=F32, name="mem_kv")
        o_mem = mem_attention(z, memkv, mem_q_norm[l], mem_k_norm[l], batch)

        merged = gated_merge(u, (o_na, o_mla, o_swa, o_mem), w_gate[l], w_branch[l], b_gate[l])
        h = ws_matmul(merged, w_o[l], tm=1024, tn=512, out_dtype=F32, residual=h, name="out_proj")

        if l % 2 == 0:
            hn = rmsnorm_bf16(h, norm_ffn[l])
            act = ws_swiglu(hn, ffn_w_up[l // 2], D_FF, tm=1024, tn=512, name="ffn_up")
            h = ws_matmul(act, ffn_w_down[l // 2], tm=512, tn=512, out_dtype=F32, residual=h, name="ffn_down")
        else:
            h = moe_block(h, norm_ffn[l], moe_router[l // 2], moe_w_up[l // 2], moe_w_down[l // 2])
    return h.reshape(batch, seq, d)
```

```python
import functools

import jax
import jax.numpy as jnp
import numpy as np
from jax import lax
from jax.experimental import pallas as pl
from jax.experimental.pallas import tpu as pltpu

F32 = jnp.float32
BF16 = jnp.bfloat16

D_MODEL = 2048
SEQ = 2048
MEM_LEN = 256
GRID_W = 64
ROPE_THETA = 10000.0
EPS = 1e-6
NEG = -1e30

NA_HEADS = 4
NA_HD = 128
NA_WIN_R = 8
NA_WIN_C = 16
NA_QBLK = 128
NA_KWIN = 640

MLA_HEADS = 4
MLA_NOPE = 128
MLA_ROPE = 64
MLA_QK = MLA_NOPE + MLA_ROPE
MLA_PAD = 256
MLA_Q_RANK = 768
MLA_KV_RANK = 256

SWA_HEADS = 8
SWA_KV_HEADS = 2
SWA_HD = 64
SWA_BLOCK = 128

MEM_HEADS = 4
MEM_HD = 128

D_FF = 5632
N_EXPERTS = 8
D_FF_EXPERT = 7168
MOE_TM = 512

LANES = 128
VMEM_CAP = 60000 * 1024

COL_NA_Q, COL_NA_K, COL_NA_V = 0, 512, 1024
COL_CQ, COL_CKV = 1536, 2304
COL_SWQ, COL_MEQ, COL_SWK, COL_SWV, COL_KPE = 2560, 3072, 3584, 3712, 3840
IN_COLS_PAD = 4096


def _params(n_axes, vmem_bytes):
    return pltpu.CompilerParams(
        dimension_semantics=("arbitrary",) * n_axes,
        vmem_limit_bytes=int(min(VMEM_CAP, vmem_bytes)))


def _rms(x, g):
    ms = jnp.mean(x * x, axis=-1, keepdims=True)
    return x * lax.rsqrt(ms + EPS) * g


def _sigmoid(x):
    return 1.0 / (1.0 + jnp.exp(-x))


def _cast_rows(src_ref, dst_ref, rows, chunk=256):
    def body(c, carry):
        r = pl.multiple_of(c * chunk, chunk)
        dst_ref[pl.ds(r, chunk), :] = src_ref[pl.ds(r, chunk), :].astype(BF16)
        return carry
    lax.fori_loop(0, rows // chunk, body, 0)


def _dot(a, b):
    return jnp.dot(a, b, preferred_element_type=F32)


def _dot_nt(a, b):
    return lax.dot_general(a, b, (((1,), (1,)), ((), ())), preferred_element_type=F32)


def _rmsnorm_kernel(x_ref, g_ref, o_ref):
    o_ref[...] = _rms(x_ref[...], g_ref[...]).astype(o_ref.dtype)


def rmsnorm_bf16(x, g, tm=512):
    m, d = x.shape
    return pl.pallas_call(
        _rmsnorm_kernel,
        grid=(m // tm,),
        in_specs=[pl.BlockSpec((tm, d), lambda i: (i, 0)),
                  pl.BlockSpec((1, d), lambda i: (0, 0))],
        out_specs=pl.BlockSpec((tm, d), lambda i: (i, 0)),
        out_shape=jax.ShapeDtypeStruct((m, d), BF16),
        compiler_params=_params(1, 4 * tm * d * 6 + (8 << 20)),
        name="rmsnorm_bf16",
    )(x, g.reshape(1, d))


def _ws_plain_kernel(x_ref, w_ref, o_ref, wb_ref):
    @pl.when(pl.program_id(1) == 0)
    def _():
        _cast_rows(w_ref, wb_ref, w_ref.shape[0])
    o_ref[...] = _dot(x_ref[...], wb_ref[...]).astype(o_ref.dtype)


def _ws_residual_kernel(x_ref, w_ref, r_ref, o_ref, wb_ref):
    @pl.when(pl.program_id(1) == 0)
    def _():
        _cast_rows(w_ref, wb_ref, w_ref.shape[0])
    o_ref[...] = r_ref[...] + _dot(x_ref[...], wb_ref[...])


def _ws_swiglu_kernel(x_ref, wg_ref, wu_ref, o_ref, wgb_ref, wub_ref):
    @pl.when(pl.program_id(1) == 0)
    def _():
        _cast_rows(wg_ref, wgb_ref, wg_ref.shape[0])
        _cast_rows(wu_ref, wub_ref, wu_ref.shape[0])
    x = x_ref[...]
    g = _dot(x, wgb_ref[...])
    u = _dot(x, wub_ref[...])
    o_ref[...] = (g * _sigmoid(g) * u).astype(o_ref.dtype)


def ws_matmul(x, w, *, tm, tn, out_dtype, residual=None, name):
    m, k = x.shape
    n = w.shape[1]
    in_specs = [pl.BlockSpec((tm, k), lambda j, i: (i, 0)),
                pl.BlockSpec((k, tn), lambda j, i: (0, j))]
    args = [x, w]
    kern = _ws_plain_kernel
    if residual is not None:
        in_specs.append(pl.BlockSpec((tm, tn), lambda j, i: (i, j)))
        args.append(residual)
        kern = _ws_residual_kernel
    vmem = 2 * (tm * k * 2 + k * tn * 4 + 2 * tm * tn * 4) + k * tn * 2 + tm * tn * 4 + (6 << 20)
    return pl.pallas_call(
        kern,
        grid=(n // tn, m // tm),
        in_specs=in_specs,
        out_specs=pl.BlockSpec((tm, tn), lambda j, i: (i, j)),
        out_shape=jax.ShapeDtypeStruct((m, n), out_dtype),
        scratch_shapes=[pltpu.VMEM((k, tn), BF16)],
        compiler_params=_params(2, vmem),
        name=name,
    )(*args)


def ws_swiglu(x, w_up, d_ff, *, tm, tn, name):
    m, k = x.shape
    nb = d_ff // tn
    vmem = 2 * (tm * k * 2 + 2 * k * tn * 4 + tm * tn * 2) + 2 * k * tn * 2 + 3 * tm * tn * 4 + (6 << 20)
    return pl.pallas_call(
        _ws_swiglu_kernel,
        grid=(nb, m // tm),
        in_specs=[pl.BlockSpec((tm, k), lambda j, i: (i, 0)),
                  pl.BlockSpec((k, tn), lambda j, i: (0, j)),
                  pl.BlockSpec((k, tn), lambda j, i: (0, j + nb))],
        out_specs=pl.BlockSpec((tm, tn), lambda j, i: (i, j)),
        out_shape=jax.ShapeDtypeStruct((m, d_ff), BF16),
        scratch_shapes=[pltpu.VMEM((k, tn), BF16), pltpu.VMEM((k, tn), BF16)],
        compiler_params=_params(2, vmem),
        name=name,
    )(x, w_up, w_up)


NA_ROWS = SEQ // GRID_W
NA_NBLK = SEQ // NA_QBLK
NA_PAIRS = NA_KWIN // NA_QBLK
NA_PATTERN_BLOCKS = (0, 1, 2, NA_NBLK - 2, NA_NBLK - 1)


def _na_key_start(blk):
    return min(max(blk - 2, 0), NA_NBLK - NA_PAIRS)


def _na_pattern(i):
    return jnp.where(i < 2, i, jnp.where(i <= NA_NBLK - 3, 2, i - (NA_NBLK - 5)))


def _na_bias_table(rpb):
    hh = rpb.shape[0]
    span = GRID_W - 1
    left = span - (NA_WIN_C - 1)
    right = 2 * span + 1 - left - (2 * NA_WIN_C - 1)
    ext = jnp.pad(rpb, ((0, 0), (0, 0), (left, right)), constant_values=NEG)
    toep = jnp.stack([ext[:, :, span - qc:span - qc + GRID_W] for qc in range(GRID_W)], axis=2)
    qc = np.arange(GRID_W)
    cs = np.clip(qc - NA_WIN_C // 2, 0, GRID_W - NA_WIN_C)
    kc = np.arange(GRID_W)
    col_ok = (kc[None, :] >= cs[:, None]) & (kc[None, :] < cs[:, None] + NA_WIN_C)
    toep = jnp.where(col_ok[None, None], toep, NEG)
    neg = jnp.full((hh, GRID_W, GRID_W), NEG, rpb.dtype)
    patterns = []
    for blk in NA_PATTERN_BLOCKS:
        q_rows = []
        for a in range(NA_QBLK // GRID_W):
            qr = 2 * blk + a
            rs = min(max(qr - NA_WIN_R // 2, 0), NA_ROWS - NA_WIN_R)
            k_blocks = []
            for c in range(NA_KWIN // GRID_W):
                kr = 2 * _na_key_start(blk) + c
                ok = rs <= kr < rs + NA_WIN_R
                k_blocks.append(toep[:, kr - qr + NA_WIN_R - 1] if ok else neg)
            q_rows.append(jnp.concatenate(k_blocks, axis=-1))
        patterns.append(jnp.concatenate(q_rows, axis=-2))
    return jnp.stack(patterns, axis=0)


def _na_kernel(q_ref, k_ref, v_ref, bias_ref, gq_ref, gk_ref, o_ref, kn_ref, vb_ref):
    i = pl.program_id(1)

    @pl.when(i == 0)
    def _():
        def body(c, carry):
            r = pl.multiple_of(c * 256, 256)
            for h in range(NA_HEADS):
                hs = slice(h * NA_HD, (h + 1) * NA_HD)
                kn_ref[pl.ds(r, 256), hs] = _rms(k_ref[pl.ds(r, 256), hs], gk_ref[...]).astype(BF16)
            vb_ref[pl.ds(r, 256), :] = v_ref[pl.ds(r, 256), :].astype(BF16)
            return carry
        lax.fori_loop(0, SEQ // 256, body, 0)

    start = pl.multiple_of(jnp.clip(i - 2, 0, NA_NBLK - NA_PAIRS) * NA_QBLK, NA_QBLK)
    scale = NA_HD ** -0.5
    for h in range(NA_HEADS):
        hs = slice(h * NA_HD, (h + 1) * NA_HD)
        q = (_rms(q_ref[:, hs], gq_ref[...]) * scale).astype(BF16)
        s = _dot_nt(q, kn_ref[pl.ds(start, NA_KWIN), hs]) + bias_ref[0, h]
        m = jnp.max(s, axis=-1, keepdims=True)
        p = jnp.exp(s - m)
        l = jnp.sum(p, axis=-1, keepdims=True)
        o = _dot(p.astype(BF16), vb_ref[pl.ds(start, NA_KWIN), hs]) / l
        o_ref[:, hs] = o.astype(o_ref.dtype)


def na_attention(z, bias, gq, gk, batch):
    nblk = SEQ // NA_QBLK
    w = NA_HEADS * NA_HD
    vmem = (2 * (NA_QBLK * w * 4 + 2 * SEQ * w * 4 + NA_HEADS * NA_QBLK * NA_KWIN * 4 + NA_QBLK * w * 2)
            + 2 * SEQ * w * 2 + (8 << 20))
    return pl.pallas_call(
        _na_kernel,
        grid=(batch, nblk),
        in_specs=[pl.BlockSpec((NA_QBLK, w), lambda b, i: (b * nblk + i, COL_NA_Q // w)),
                  pl.BlockSpec((SEQ, w), lambda b, i: (b, COL_NA_K // w)),
                  pl.BlockSpec((SEQ, w), lambda b, i: (b, COL_NA_V // w)),
                  pl.BlockSpec((1, NA_HEADS, NA_QBLK, NA_KWIN), lambda b, i: (_na_pattern(i), 0, 0, 0)),
                  pl.BlockSpec((1, NA_HD), lambda b, i: (0, 0)),
                  pl.BlockSpec((1, NA_HD), lambda b, i: (0, 0))],
        out_specs=pl.BlockSpec((NA_QBLK, w), lambda b, i: (b * nblk + i, 0)),
        out_shape=jax.ShapeDtypeStruct((batch * SEQ, w), BF16),
        scratch_shapes=[pltpu.VMEM((SEQ, w), BF16), pltpu.VMEM((SEQ, w), BF16)],
        compiler_params=_params(2, vmem),
        name="na_attention",
    )(z, z, z, bias, gq.reshape(1, NA_HD), gk.reshape(1, NA_HD))


def _rope_tables(width, head_dim, first_lane):
    half = head_dim // 2
    freqs = ROPE_THETA ** (-2.0 * np.arange(half, dtype=np.float32) / head_dim)
    ang = jnp.arange(SEQ, dtype=F32)[:, None] * jnp.asarray(freqs, F32)[None, :]
    cos, sin = jnp.cos(ang), jnp.sin(ang)
    lane = np.arange(width)
    rel = lane - first_lane
    in_rope = (rel >= 0) & (rel < (width - first_lane if first_lane == 0 else head_dim))
    p = np.where(in_rope, rel % head_dim, 0)
    j = p % half
    first_half = in_rope & (p < half)
    second_half = in_rope & (p >= half)
    cos_t = jnp.where(in_rope[None, :], cos[:, j], 1.0)
    sin_a = jnp.where(first_half[None, :], -sin[:, j], 0.0)
    sin_b = jnp.where(second_half[None, :], sin[:, j], 0.0)
    return cos_t, sin_a, sin_b


def _apply_rope(x, cos_t, sin_a, sin_b, half):
    n = x.shape[-1]
    return (x * cos_t + pltpu.roll(x, n - half, axis=1) * sin_a
            + pltpu.roll(x, half, axis=1) * sin_b)


def _mla_prep_kernel(cq_ref, ckv_ref, kpe_ref, wq_ref, wkv_ref, gcq_ref, gckv_ref,
                     gqn_ref, gkn_ref, cos_ref, sa_ref, sb_ref, q_ref, k_ref, v_ref):
    half = MLA_ROPE // 2
    cos_t, sin_a, sin_b = cos_ref[...], sa_ref[...], sb_ref[...]
    scale = MLA_QK ** -0.5
    cq = _rms(cq_ref[...], gcq_ref[...]).astype(BF16)
    q_raw = _dot(cq, wq_ref[...])
    ckv = _rms(ckv_ref[...], gckv_ref[...]).astype(BF16)
    kv_raw = _dot(ckv, wkv_ref[...])
    kpe = kpe_ref[...]
    kpe_ss = jnp.sum(kpe * kpe, axis=-1, keepdims=True)
    for h in range(MLA_HEADS):
        cs = slice(h * MLA_PAD, (h + 1) * MLA_PAD)
        qc = q_raw[:, cs]
        ms = jnp.sum(qc * qc, axis=-1, keepdims=True) * (1.0 / MLA_QK)
        qn = qc * lax.rsqrt(ms + EPS) * gqn_ref[...]
        q_ref[:, cs] = (_apply_rope(qn, cos_t, sin_a, sin_b, half) * scale).astype(BF16)
        kn = kv_raw[:, h * MLA_PAD:h * MLA_PAD + MLA_NOPE]
        ms = (jnp.sum(kn * kn, axis=-1, keepdims=True) + kpe_ss) * (1.0 / MLA_QK)
        kc = jnp.concatenate([kn, kpe], axis=-1) * lax.rsqrt(ms + EPS) * gkn_ref[...]
        k_ref[:, cs] = _apply_rope(kc, cos_t, sin_a, sin_b, half).astype(BF16)
        v_ref[:, h * MLA_NOPE:(h + 1) * MLA_NOPE] = kv_raw[:, h * MLA_PAD + MLA_NOPE:(h + 1) * MLA_PAD].astype(BF16)


def mla_prep(z, wq_pad, wkv, gcq, gckv, gqn_pad, gkn_pad, tabs, tm=512):
    t = z.shape[0]
    sb = SEQ // tm
    qw = MLA_HEADS * MLA_PAD
    row = lambda i: (i, 0)
    const = lambda i: (0, 0)
    pos = lambda i: (i % sb, 0)
    return pl.pallas_call(
        _mla_prep_kernel,
        grid=(t // tm,),
        in_specs=[pl.BlockSpec((tm, MLA_Q_RANK), lambda i: (i, COL_CQ // MLA_Q_RANK)),
                  pl.BlockSpec((tm, MLA_KV_RANK), lambda i: (i, COL_CKV // MLA_KV_RANK)),
                  pl.BlockSpec((tm, LANES), lambda i: (i, COL_KPE // LANES)),
                  pl.BlockSpec((MLA_Q_RANK, qw), const),
                  pl.BlockSpec((MLA_KV_RANK, qw), const),
                  pl.BlockSpec((1, MLA_Q_RANK), const),
                  pl.BlockSpec((1, MLA_KV_RANK), const),
                  pl.BlockSpec((1, MLA_PAD), const),
                  pl.BlockSpec((1, MLA_PAD), const),
                  pl.BlockSpec((tm, MLA_PAD), pos),
                  pl.BlockSpec((tm, MLA_PAD), pos),
                  pl.BlockSpec((tm, MLA_PAD), pos)],
        out_specs=[pl.BlockSpec((tm, qw), row), pl.BlockSpec((tm, qw), row),
                   pl.BlockSpec((tm, MLA_HEADS * MLA_NOPE), row)],
        out_shape=[jax.ShapeDtypeStruct((t, qw), BF16), jax.ShapeDtypeStruct((t, qw), BF16),
                   jax.ShapeDtypeStruct((t, MLA_HEADS * MLA_NOPE), BF16)],
        compiler_params=_params(1, 40 << 20),
        name="mla_prep",
    )(z, z, z, wq_pad, wkv, gcq.reshape(1, -1), gckv.reshape(1, -1), gqn_pad, gkn_pad, *tabs)


def _mla_attn_kernel(q_ref, k_ref, v_ref, o_ref):
    for h in range(MLA_HEADS):
        cs = slice(h * MLA_PAD, (h + 1) * MLA_PAD)
        vs = slice(h * MLA_NOPE, (h + 1) * MLA_NOPE)
        s = _dot_nt(q_ref[:, cs], k_ref[:, cs])
        m = jnp.max(s, axis=-1, keepdims=True)
        p = jnp.exp(s - m)
        l = jnp.sum(p, axis=-1, keepdims=True)
        o = _dot(p.astype(BF16), v_ref[:, vs]) / l
        o_ref[:, vs] = o.astype(o_ref.dtype)


def mla_attention(q, k, v, batch, tq=256):
    nq = SEQ // tq
    qw = MLA_HEADS * MLA_PAD
    vw = MLA_HEADS * MLA_NOPE
    vmem = 2 * (tq * qw * 2 + SEQ * qw * 2 + SEQ * vw * 2 + tq * vw * 2) + 4 * tq * SEQ * 4 + (8 << 20)
    return pl.pallas_call(
        _mla_attn_kernel,
        grid=(batch, nq),
        in_specs=[pl.BlockSpec((tq, qw), lambda b, i: (b * nq + i, 0)),
                  pl.BlockSpec((SEQ, qw), lambda b, i: (b, 0)),
                  pl.BlockSpec((SEQ, vw), lambda b, i: (b, 0))],
        out_specs=pl.BlockSpec((tq, vw), lambda b, i: (b * nq + i, 0)),
        out_shape=jax.ShapeDtypeStruct((batch * SEQ, vw), BF16),
        compiler_params=_params(2, vmem),
        name="mla_attention",
    )(q, k, v)


def _rms_halves(x, g):
    lo = lax.broadcasted_iota(jnp.int32, x.shape, 1) < SWA_HD
    x2 = x * x
    s_lo = jnp.sum(jnp.where(lo, x2, 0.0), axis=-1, keepdims=True)
    s_hi = jnp.sum(jnp.where(lo, 0.0, x2), axis=-1, keepdims=True)
    ms = jnp.where(lo, s_lo, s_hi) * (1.0 / SWA_HD)
    return x * lax.rsqrt(ms + EPS) * g


def _swa_kernel(sink_ref, q_ref, k_ref, v_ref, gq_ref, gk_ref, cosq_ref, saq_ref, sbq_ref,
                cosk_ref, sak_ref, sbk_ref, o_ref, kk_ref, vv_ref):
    n = pl.program_id(1)
    nblk = SEQ // SWA_BLOCK
    half = SWA_HD // 2

    @pl.when(n == 0)
    def _():
        zeros = jnp.zeros((SWA_BLOCK, LANES), BF16)
        for c in range(4):
            kk_ref[c, pl.ds(0, SWA_BLOCK), :] = zeros
            kk_ref[c, pl.ds(SEQ + SWA_BLOCK, SWA_BLOCK), :] = zeros
            vv_ref[c, pl.ds(0, SWA_BLOCK), :] = zeros
            vv_ref[c, pl.ds(SEQ + SWA_BLOCK, SWA_BLOCK), :] = zeros

        def body(c, carry):
            r = pl.multiple_of(c * 256, 256)
            dst = pl.ds(r + SWA_BLOCK, 256)
            lo = lax.broadcasted_iota(jnp.int32, (256, LANES), 1) < SWA_HD
            kr = _apply_rope(_rms_halves(k_ref[pl.ds(r, 256), :], gk_ref[...]),
                             cosk_ref[pl.ds(r, 256), :], sak_ref[pl.ds(r, 256), :],
                             sbk_ref[pl.ds(r, 256), :], half)
            ks = pltpu.roll(kr, SWA_HD, axis=1)
            kk_ref[0, dst, :] = jnp.where(lo, kr, 0.0).astype(BF16)
            kk_ref[1, dst, :] = jnp.where(lo, 0.0, ks).astype(BF16)
            kk_ref[2, dst, :] = jnp.where(lo, ks, 0.0).astype(BF16)
            kk_ref[3, dst, :] = jnp.where(lo, 0.0, kr).astype(BF16)
            vr = v_ref[pl.ds(r, 256), :]
            vs = pltpu.roll(vr, SWA_HD, axis=1)
            vv_ref[0, dst, :] = jnp.where(lo, vr, 0.0).astype(BF16)
            vv_ref[1, dst, :] = jnp.where(lo, 0.0, vs).astype(BF16)
            vv_ref[2, dst, :] = jnp.where(lo, vs, 0.0).astype(BF16)
            vv_ref[3, dst, :] = jnp.where(lo, 0.0, vr).astype(BF16)
            return carry
        lax.fori_loop(0, SEQ // 256, body, 0)

    band = pl.ds(pl.multiple_of(n * SWA_BLOCK, SWA_BLOCK), 3 * SWA_BLOCK)
    a = lax.broadcasted_iota(jnp.int32, (SWA_BLOCK, 3 * SWA_BLOCK), 0)
    c = lax.broadcasted_iota(jnp.int32, (SWA_BLOCK, 3 * SWA_BLOCK), 1)
    c_min = jnp.where(n == 0, SWA_BLOCK, 0)
    c_max = jnp.where(n == nblk - 1, 2 * SWA_BLOCK, 3 * SWA_BLOCK)
    valid = (c >= a) & (c <= a + 2 * SWA_BLOCK) & (c >= c_min) & (c < c_max)
    scale = SWA_HD ** -0.5
    for pair in range(SWA_HEADS // 2):
        ps = slice(pair * LANES, (pair + 1) * LANES)
        grp = pair // 2
        q = _apply_rope(_rms_halves(q_ref[:, ps], gq_ref[...]),
                        cosq_ref[...], saq_ref[...], sbq_ref[...], half)
        q = (q * scale).astype(BF16)
        acc = None
        for hf in range(2):
            sink = sink_ref[2 * pair + hf]
            s = _dot_nt(q, kk_ref[2 * grp + hf, band, :])
            s = jnp.where(valid, s, NEG)
            m = jnp.maximum(jnp.max(s, axis=-1, keepdims=True), sink)
            p = jnp.exp(s - m)
            den = jnp.sum(p, axis=-1, keepdims=True) + jnp.exp(sink - m)
            o = _dot(p.astype(BF16), vv_ref[2 * grp + hf, band, :]) / den
            acc = o if acc is None else acc + o
        o_ref[:, ps] = acc.astype(o_ref.dtype)


def swa_attention(z, sink, gq, gk, tabs, batch):
    nblk = SEQ // SWA_BLOCK
    qw = SWA_HEADS * SWA_HD
    cos_t, sin_a, sin_b = tabs
    g2 = lambda g: jnp.concatenate([g, g]).reshape(1, LANES)
    blk = lambda b, n: (n, 0)
    full = lambda b, n: (0, 0)
    tab_blk = pl.BlockSpec((SWA_BLOCK, LANES), blk)
    tab_full = pl.BlockSpec((SEQ, LANES), full)
    vmem = (2 * (SWA_BLOCK * qw * 6 + 2 * SEQ * LANES * 4 + 3 * SEQ * LANES * 4 + 3 * SWA_BLOCK * LANES * 4)
            + 8 * (SEQ + 2 * SWA_BLOCK) * LANES * 2 + (8 << 20))
    return pl.pallas_call(
        _swa_kernel,
        grid=(batch, nblk),
        in_specs=[pl.BlockSpec(memory_space=pltpu.SMEM),
                  pl.BlockSpec((SWA_BLOCK, qw), lambda b, n: (b * nblk + n, COL_SWQ // qw)),
                  pl.BlockSpec((SEQ, LANES), lambda b, n: (b, COL_SWK // LANES)),
                  pl.BlockSpec((SEQ, LANES), lambda b, n: (b, COL_SWV // LANES)),
                  pl.BlockSpec((1, LANES), full), pl.BlockSpec((1, LANES), full),
                  tab_blk, tab_blk, tab_blk, tab_full, tab_full, tab_full],
        out_specs=pl.BlockSpec((SWA_BLOCK, qw), lambda b, n: (b * nblk + n, 0)),
        out_shape=jax.ShapeDtypeStruct((batch * SEQ, qw), BF16),
        scratch_shapes=[pltpu.VMEM((4, SEQ + 2 * SWA_BLOCK, LANES), BF16),
                        pltpu.VMEM((4, SEQ + 2 * SWA_BLOCK, LANES), BF16)],
        compiler_params=_params(2, vmem),
        name="swa_attention",
    )(sink, z, z, z, g2(gq), g2(gk), cos_t, sin_a, sin_b, cos_t, sin_a, sin_b)


def _mem_attn_kernel(q_ref, k_ref, v_ref, gq_ref, gk_ref, o_ref):
    scale = MEM_HD ** -0.5
    for h in range(MEM_HEADS):
        hs = slice(h * MEM_HD, (h + 1) * MEM_HD)
        q = (_rms(q_ref[:, hs], gq_ref[...]) * scale).astype(BF16)
        k = _rms(k_ref[:, hs], gk_ref[...]).astype(BF16)
        s = _dot_nt(q, k)
        m = jnp.max(s, axis=-1, keepdims=True)
        p = jnp.exp(s - m)
        l = jnp.sum(p, axis=-1, keepdims=True)
        o = _dot(p.astype(BF16), v_ref[:, hs].astype(BF16)) / l
        o_ref[:, hs] = o.astype(o_ref.dtype)


def mem_attention(z, memkv, gq, gk, batch, tq=512):
    nq = SEQ // tq
    w = MEM_HEADS * MEM_HD
    return pl.pallas_call(
        _mem_attn_kernel,
        grid=(batch, nq),
        in_specs=[pl.BlockSpec((tq, w), lambda b, i: (b * nq + i, COL_MEQ // w)),
                  pl.BlockSpec((MEM_LEN, w), lambda b, i: (b, 0)),
                  pl.BlockSpec((MEM_LEN, w), lambda b, i: (b, 1)),
                  pl.BlockSpec((1, MEM_HD), lambda b, i: (0, 0)),
                  pl.BlockSpec((1, MEM_HD), lambda b, i: (0, 0))],
        out_specs=pl.BlockSpec((tq, w), lambda b, i: (b * nq + i, 0)),
        out_shape=jax.ShapeDtypeStruct((batch * SEQ, w), BF16),
        compiler_params=_params(2, 24 << 20),
        name="mem_attention",
    )(z, memkv, memkv, gq.reshape(1, MEM_HD), gk.reshape(1, MEM_HD))


def _merge_kernel(u_ref, b0_ref, b1_ref, b2_ref, b3_ref, wg0_ref, wg1_ref, wg2_ref, wg3_ref,
                  wb_ref, bg_ref, o_ref, wgs_ref, wbs_ref):
    branches = (b0_ref, b1_ref, b2_ref, b3_ref)
    gate_w = (wg0_ref, wg1_ref, wg2_ref, wg3_ref)

    @pl.when(pl.program_id(1) == 0)
    def _():
        for n in range(4):
            _cast_rows(gate_w[n], wgs_ref.at[n], D_MODEL)
            wbs_ref[n] = wb_ref[n].astype(BF16)

    u = u_ref[...]
    acc = None
    for n in range(4):
        gate = _sigmoid(_dot(u, wgs_ref[n]) + bg_ref[n:n + 1, :])
        term = gate * _dot(branches[n][...], wbs_ref[n])
        acc = term if acc is None else acc + term
    o_ref[...] = acc.astype(o_ref.dtype)


def gated_merge(u, branches, w_gate, w_branch, b_gate, tm=1024, tn=256):
    t = u.shape[0]
    bw = branches[0].shape[1]
    nb = D_MODEL // tn
    wg = w_gate.reshape(D_MODEL, 4 * D_MODEL)
    gate_specs = [pl.BlockSpec((D_MODEL, tn), functools.partial(lambda j, i, n: (0, n * nb + j), n=n))
                  for n in range(4)]
    vmem = (2 * (tm * D_MODEL * 2 + 4 * tm * bw * 2 + 4 * D_MODEL * tn * 4 + 4 * bw * tn * 4 + tm * tn * 2)
            + 4 * D_MODEL * tn * 2 + 4 * bw * tn * 2 + 4 * tm * tn * 4 + (6 << 20))
    return pl.pallas_call(
        _merge_kernel,
        grid=(nb, t // tm),
        in_specs=[pl.BlockSpec((tm, D_MODEL), lambda j, i: (i, 0))]
        + [pl.BlockSpec((tm, bw), lambda j, i: (i, 0)) for _ in range(4)]
        + gate_specs
        + [pl.BlockSpec((4, bw, tn), lambda j, i: (0, 0, j)),
           pl.BlockSpec((4, tn), lambda j, i: (0, j))],
        out_specs=pl.BlockSpec((tm, tn), lambda j, i: (i, j)),
        out_shape=jax.ShapeDtypeStruct((t, D_MODEL), BF16),
        scratch_shapes=[pltpu.VMEM((4, D_MODEL, tn), BF16), pltpu.VMEM((4, bw, tn), BF16)],
        compiler_params=_params(2, vmem),
        name="gated_merge",
    )(u, *branches, wg, wg, wg, wg, w_branch, b_gate)


def _router_kernel(h_ref, g_ref, wr_ref, ri_ref, rw_ref, cnt_ref, carry_ref):
    @pl.when(pl.program_id(0) == 0)
    def _():
        carry_ref[...] = jnp.zeros_like(carry_ref)

    tm = h_ref.shape[0]
    hn = _rms(h_ref[...], g_ref[...])
    logits = jnp.dot(hn, wr_ref[...], precision=lax.Precision.HIGHEST, preferred_element_type=F32)
    lane = lax.broadcasted_iota(jnp.int32, (tm, LANES), 1)
    lane_f = lane.astype(F32)
    logits = jnp.where(lane < N_EXPERTS, logits, -jnp.inf)
    m1 = jnp.max(logits, axis=-1, keepdims=True)
    i1 = jnp.min(jnp.where(logits == m1, lane_f, float(LANES)), axis=-1, keepdims=True)
    oh1 = lane_f == i1
    rest = jnp.where(oh1, -jnp.inf, logits)
    m2 = jnp.max(rest, axis=-1, keepdims=True)
    i2 = jnp.min(jnp.where(rest == m2, lane_f, float(LANES)), axis=-1, keepdims=True)
    oh2 = lane_f == i2
    e2 = jnp.exp(m2 - m1)
    w1 = 1.0 / (1.0 + e2)
    w2 = e2 / (1.0 + e2)
    chosen = jnp.where(oh1 | oh2, 1.0, 0.0)
    before = (lax.broadcasted_iota(jnp.int32, (tm, tm), 1)
              < lax.broadcasted_iota(jnp.int32, (tm, tm), 0))
    prefix = _dot(jnp.where(before, 1.0, 0.0).astype(BF16), chosen.astype(BF16)) + carry_ref[...]
    r1 = jnp.sum(jnp.where(oh1, prefix, 0.0), axis=-1, keepdims=True).astype(jnp.int32)
    r2 = jnp.sum(jnp.where(oh2, prefix, 0.0), axis=-1, keepdims=True).astype(jnp.int32)
    carry_ref[...] += jnp.sum(chosen, axis=0, keepdims=True)
    ri_ref[...] = jnp.where(lane == 0, i1.astype(jnp.int32),
                            jnp.where(lane == 1, i2.astype(jnp.int32),
                                      jnp.where(lane == 2, r1, jnp.where(lane == 3, r2, 0))))
    rw_ref[...] = jnp.where(lane == 0, w1, jnp.where(lane == 1, w2, 0.0))
    cnt_ref[...] = carry_ref[...]


def moe_router(h, g, w_router, tm=512):
    t, d = h.shape
    wr = jnp.pad(w_router, ((0, 0), (0, LANES - N_EXPERTS)))
    return pl.pallas_call(
        _router_kernel,
        grid=(t // tm,),
        in_specs=[pl.BlockSpec((tm, d), lambda i: (i, 0)),
                  pl.BlockSpec((1, d), lambda i: (0, 0)),
                  pl.BlockSpec((d, LANES), lambda i: (0, 0))],
        out_specs=[pl.BlockSpec((tm, LANES), lambda i: (i, 0)),
                   pl.BlockSpec((tm, LANES), lambda i: (i, 0)),
                   pl.BlockSpec((1, LANES), lambda i: (0, 0))],
        out_shape=[jax.ShapeDtypeStruct((t, LANES), jnp.int32),
                   jax.ShapeDtypeStruct((t, LANES), F32),
                   jax.ShapeDtypeStruct((1, LANES), F32)],
        scratch_shapes=[pltpu.VMEM((1, LANES), F32)],
        compiler_params=_params(1, 32 << 20),
        name="moe_router",
    )(h, g.reshape(1, d), wr)


def _row_copy(src_hbm, row, dst_vmem, r, sem):
    return pltpu.make_async_copy(src_hbm.at[pl.ds(row, 1), :], dst_vmem.at[pl.ds(r, 1), :], sem)


def _moe_gather_kernel(src_ref, h_hbm, g_ref, o_ref, buf_ref, sem):
    tg = buf_ref.shape[0]
    base = pl.program_id(0) * tg

    def issue(r, carry):
        _row_copy(h_hbm, src_ref[base + r], buf_ref, r, sem).start()
        return carry
    lax.fori_loop(0, tg, issue, 0, unroll=8)

    def drain(r, carry):
        _row_copy(h_hbm, 0, buf_ref, r, sem).wait()
        return carry
    lax.fori_loop(0, tg, drain, 0, unroll=8)
    o_ref[...] = _rms(buf_ref[...], g_ref[...]).astype(o_ref.dtype)


def moe_gather(src_tok, h, g, rows, tg=256):
    d = h.shape[1]
    return pl.pallas_call(
        _moe_gather_kernel,
        grid_spec=pltpu.PrefetchScalarGridSpec(
            num_scalar_prefetch=1,
            grid=(rows // tg,),
            in_specs=[pl.BlockSpec(memory_space=pl.ANY),
                      pl.BlockSpec((1, d), lambda i, s: (0, 0))],
            out_specs=pl.BlockSpec((tg, d), lambda i, s: (i, 0)),
            scratch_shapes=[pltpu.VMEM((tg, d), F32), pltpu.SemaphoreType.DMA(())]),
        out_shape=jax.ShapeDtypeStruct((rows, d), BF16),
        compiler_params=_params(1, 24 << 20),
        name="moe_gather",
    )(src_tok, h, g.reshape(1, d))


def _new_expert(te_ref, i):
    return jnp.logical_or(i == 0, te_ref[i] != te_ref[jnp.maximum(i - 1, 0)])


def _moe_up_kernel(te_ref, nt_ref, x_ref, wg_ref, wu_ref, o_ref, wgb_ref, wub_ref):
    i = pl.program_id(1)
    active = i < nt_ref[0]

    @pl.when(jnp.logical_and(active, _new_expert(te_ref, i)))
    def _():
        _cast_rows(wg_ref.at[0], wgb_ref, D_MODEL)
        _cast_rows(wu_ref.at[0], wub_ref, D_MODEL)

    @pl.when(active)
    def _():
        x = x_ref[...]
        g = _dot(x, wgb_ref[...])
        u = _dot(x, wub_ref[...])
        o_ref[...] = (g * _sigmoid(g) * u).astype(o_ref.dtype)

    @pl.when(jnp.logical_not(active))
    def _():
        o_ref[...] = jnp.zeros_like(o_ref)


def moe_up(xs, w_up, tile_e, n_tiles, tf=512):
    rows, d = xs.shape
    nf = D_FF_EXPERT // tf
    row = lambda j, i, te, nt: (jnp.minimum(i, nt[0] - 1), 0)
    vmem = 2 * (MOE_TM * d * 2 + 2 * d * tf * 4 + MOE_TM * tf * 2) + 2 * d * tf * 2 + 3 * MOE_TM * tf * 4 + (6 << 20)
    return pl.pallas_call(
        _moe_up_kernel,
        grid_spec=pltpu.PrefetchScalarGridSpec(
            num_scalar_prefetch=2,
            grid=(nf, rows // MOE_TM),
            in_specs=[pl.BlockSpec((MOE_TM, d), row),
                      pl.BlockSpec((1, d, tf), lambda j, i, te, nt: (te[i], 0, j)),
                      pl.BlockSpec((1, d, tf), lambda j, i, te, nt: (te[i], 0, j + nf))],
            out_specs=pl.BlockSpec((MOE_TM, tf), lambda j, i, te, nt: (i, j)),
            scratch_shapes=[pltpu.VMEM((d, tf), BF16), pltpu.VMEM((d, tf), BF16)]),
        out_shape=jax.ShapeDtypeStruct((rows, D_FF_EXPERT), BF16),
        compiler_params=_params(2, vmem),
        name="moe_up",
    )(tile_e, n_tiles, xs, w_up, w_up)


def _moe_down_kernel(te_ref, nt_ref, a_ref, w_ref, o_ref, wb_ref):
    i = pl.program_id(1)
    active = i < nt_ref[0]

    @pl.when(jnp.logical_and(active, _new_expert(te_ref, i)))
    def _():
        _cast_rows(w_ref.at[0], wb_ref, D_FF_EXPERT)

    @pl.when(active)
    def _():
        o_ref[...] = _dot(a_ref[...], wb_ref[...])

    @pl.when(jnp.logical_not(active))
    def _():
        o_ref[...] = jnp.zeros_like(o_ref)


def moe_down(act, w_down, tile_e, n_tiles, tn=256):
    rows, f = act.shape
    vmem = 2 * (MOE_TM * f * 2 + f * tn * 4 + MOE_TM * tn * 4) + f * tn * 2 + MOE_TM * tn * 4 + (6 << 20)
    return pl.pallas_call(
        _moe_down_kernel,
        grid_spec=pltpu.PrefetchScalarGridSpec(
            num_scalar_prefetch=2,
            grid=(D_MODEL // tn, rows // MOE_TM),
            in_specs=[pl.BlockSpec((MOE_TM, f), lambda j, i, te, nt: (jnp.minimum(i, nt[0] - 1), 0)),
                      pl.BlockSpec((1, f, tn), lambda j, i, te, nt: (te[i], 0, j))],
            out_specs=pl.BlockSpec((MOE_TM, tn), lambda j, i, te, nt: (i, j)),
            scratch_shapes=[pltpu.VMEM((f, tn), BF16)]),
        out_shape=jax.ShapeDtypeStruct((rows, D_MODEL), F32),
        compiler_params=_params(2, vmem),
        name="moe_down",
    )(tile_e, n_tiles, act, w_down)


def _moe_combine_kernel(pos_ref, h_ref, rw_ref, y_hbm, o_ref, b1_ref, b2_ref, sem):
    tc = b1_ref.shape[0]
    base = pl.program_id(0) * tc

    def issue(r, carry):
        t = base + r
        _row_copy(y_hbm, pos_ref[2 * t], b1_ref, r, sem).start()
        _row_copy(y_hbm, pos_ref[2 * t + 1], b2_ref, r, sem).start()
        return carry
    lax.fori_loop(0, tc, issue, 0, unroll=8)

    def drain(r, carry):
        _row_copy(y_hbm, 0, b1_ref, r, sem).wait()
        _row_copy(y_hbm, 0, b2_ref, r, sem).wait()
        return carry
    lax.fori_loop(0, tc, drain, 0, unroll=8)
    o_ref[...] = h_ref[...] + rw_ref[:, 0:1] * b1_ref[...] + rw_ref[:, 1:2] * b2_ref[...]


def moe_combine(pos_flat, h, rw, y, tc=256):
    t, d = h.shape
    return pl.pallas_call(
        _moe_combine_kernel,
        grid_spec=pltpu.PrefetchScalarGridSpec(
            num_scalar_prefetch=1,
            grid=(t // tc,),
            in_specs=[pl.BlockSpec((tc, d), lambda i, p: (i, 0)),
                      pl.BlockSpec((tc, LANES), lambda i, p: (i, 0)),
                      pl.BlockSpec(memory_space=pl.ANY)],
            out_specs=pl.BlockSpec((tc, d), lambda i, p: (i, 0)),
            scratch_shapes=[pltpu.VMEM((tc, d), F32), pltpu.VMEM((tc, d), F32),
                            pltpu.SemaphoreType.DMA(())]),
        out_shape=jax.ShapeDtypeStruct((t, d), F32),
        compiler_params=_params(1, 32 << 20),
        name="moe_combine",
    )(pos_flat, h, rw, y)


def moe_block(h, g, w_router, w_up, w_down):
    t = h.shape[0]
    rows = 2 * t + N_EXPERTS * MOE_TM
    n_row_tiles = rows // MOE_TM
    ri, rw, cnt = moe_router(h, g, w_router)
    counts = cnt[0, :N_EXPERTS].astype(jnp.int32)
    padded = ((counts + MOE_TM - 1) // MOE_TM) * MOE_TM
    ends = jnp.cumsum(padded)
    starts = ends - padded
    pos = starts[ri[:, 0:2]] + ri[:, 2:4]
    tok = jnp.broadcast_to(jnp.arange(t, dtype=jnp.int32)[:, None], (t, 2))
    src_tok = jnp.zeros((rows,), jnp.int32).at[pos.reshape(-1)].set(tok.reshape(-1))
    n_tiles = (ends[-1] // MOE_TM).astype(jnp.int32)
    tile_first_row = jnp.arange(n_row_tiles, dtype=jnp.int32) * MOE_TM
    tile_e = jnp.sum((tile_first_row[:, None] >= ends[None, :]).astype(jnp.int32), axis=1)
    tile_e = jnp.minimum(tile_e, N_EXPERTS - 1)
    tile_e = jnp.where(jnp.arange(n_row_tiles) < n_tiles, tile_e, tile_e[n_tiles - 1])
    nt = n_tiles.reshape(1)

    xs = moe_gather(src_tok, h, g, rows)
    act = moe_up(xs, w_up, tile_e, nt)
    y = moe_down(act, w_down, tile_e, nt)
    return moe_combine(pos.reshape(-1).astype(jnp.int32), h, rw, y)


def _pad_in_proj(w_in):
    segs = np.cumsum([0, 512, 512, 512, 768, 256, 64, 512, 128, 128, 512])
    na_q, na_k, na_v, c_q, c_kv, k_pe, sw_q, sw_k, sw_v, me_q = [
        w_in[:, segs[n]:segs[n + 1]] for n in range(10)]
    d = w_in.shape[0]
    return jnp.concatenate(
        [na_q, na_k, na_v, c_q, c_kv, sw_q, me_q, sw_k, sw_v, k_pe,
         jnp.zeros((d, IN_COLS_PAD - COL_KPE - MLA_ROPE), w_in.dtype)], axis=1)


def _pad_heads(x, n_heads, width):
    lead = x.shape[:-1]
    x = x.reshape(lead + (n_heads, width))
    x = jnp.pad(x, [(0, 0)] * len(lead) + [(0, 0), (0, MLA_PAD - width)])
    return x.reshape(lead + (n_heads * MLA_PAD,))


def kernel(x, mem, norm_mix, w_in, na_q_norm, na_k_norm, na_rpb, mla_cq_norm, mla_w_uq, mla_ckv_norm, mla_w_ukv, mla_q_norm, mla_k_norm, swa_q_norm, swa_k_norm, swa_sink, mem_norm, mem_w_kv, mem_q_norm, mem_k_norm, w_branch, w_gate, b_gate, w_o, norm_ffn, ffn_w_up, ffn_w_down, moe_router, moe_w_up, moe_w_down):
    batch, seq, d = x.shape
    assert (seq, d) == (SEQ, D_MODEL) and mem.shape[1] == MEM_LEN
    t = batch * seq
    depth = w_in.shape[0]
    h = x.reshape(t, d)
    mem2 = mem.reshape(batch * MEM_LEN, d)
    mla_tabs = _rope_tables(MLA_PAD, MLA_ROPE, MLA_NOPE)
    swa_tabs = _rope_tables(LANES, SWA_HD, 0)

    for l in range(depth):
        u = rmsnorm_bf16(h, norm_mix[l])
        z = ws_matmul(u, _pad_in_proj(w_in[l]), tm=1024, tn=512, out_dtype=F32, name="in_proj")

        o_na = na_attention(z, _na_bias_table(na_rpb[l]), na_q_norm[l], na_k_norm[l], batch)

        q_mla, k_mla, v_mla = mla_prep(
            z, _pad_heads(mla_w_uq[l], MLA_HEADS, MLA_QK).astype(BF16), mla_w_ukv[l].astype(BF16),
            mla_cq_norm[l], mla_ckv_norm[l],
            jnp.pad(mla_q_norm[l], (0, MLA_PAD - MLA_QK)).reshape(1, MLA_PAD),
            jnp.pad(mla_k_norm[l], (0, MLA_PAD - MLA_QK)).reshape(1, MLA_PAD), mla_tabs)
        o_mla = mla_attention(q_mla, k_mla, v_mla, batch)

        o_swa = swa_attention(z, swa_sink[l], swa_q_norm[l], swa_k_norm[l], swa_tabs, batch)

        memn = rmsnorm_bf16(mem2, mem_norm[l])
        memkv = ws_matmul(memn, mem_w_kv[l], tm=1024, tn=512, out_dtype=F32, name="mem_kv")
        o_mem = mem_attention(z, memkv, mem_q_norm[l], mem_k_norm[l], batch)

        merged = gated_merge(u, (o_na, o_mla, o_swa, o_mem), w_gate[l], w_branch[l], b_gate[l])
        h = ws_matmul(merged, w_o[l], tm=1024, tn=512, out_dtype=F32, residual=h, name="out_proj")

        if l % 2 == 0:
            hn = rmsnorm_bf16(h, norm_ffn[l])
            act = ws_swiglu(hn, ffn_w_up[l // 2], D_FF, tm=1024, tn=512, name="ffn_up")
            h = ws_matmul(act, ffn_w_down[l // 2], tm=512, tn=512, out_dtype=F32, residual=h, name="ffn_down")
        else:
            h = moe_block(h, norm_ffn[l], moe_router[l // 2], moe_w_up[l // 2], moe_w_down[l // 2])
    return h.reshape(batch, seq, d)
```

```python
import functools

import jax
import jax.numpy as jnp
import numpy as np
from jax import lax
from jax.experimental import pallas as pl
from jax.experimental.pallas import tpu as pltpu

F32 = jnp.float32
BF16 = jnp.bfloat16

D_MODEL = 2048
SEQ = 2048
MEM_LEN = 256
GRID_W = 64
ROPE_THETA = 10000.0
EPS = 1e-6
NEG = -1e30

NA_HEADS = 4
NA_HD = 128
NA_WIN_R = 8
NA_WIN_C = 16
NA_QBLK = 128
NA_KWIN = 640

MLA_HEADS = 4
MLA_NOPE = 128
MLA_ROPE = 64
MLA_QK = MLA_NOPE + MLA_ROPE
MLA_PAD = 256
MLA_Q_RANK = 768
MLA_KV_RANK = 256

SWA_HEADS = 8
SWA_KV_HEADS = 2
SWA_HD = 64
SWA_BLOCK = 128

MEM_HEADS = 4
MEM_HD = 128

D_FF = 5632
N_EXPERTS = 8
D_FF_EXPERT = 7168
MOE_TM = 512

LANES = 128
VMEM_CAP = 60000 * 1024

COL_NA_Q, COL_NA_K, COL_NA_V = 0, 512, 1024
COL_CQ, COL_CKV = 1536, 2304
COL_SWQ, COL_MEQ, COL_SWK, COL_SWV, COL_KPE = 2560, 3072, 3584, 3712, 3840
IN_COLS_PAD = 4096


def _params(n_axes, vmem_bytes):
    return pltpu.CompilerParams(
        dimension_semantics=("arbitrary",) * n_axes,
        vmem_limit_bytes=int(min(VMEM_CAP, vmem_bytes)))


def _rms(x, g):
    ms = jnp.mean(x * x, axis=-1, keepdims=True)
    return x * lax.rsqrt(ms + EPS) * g


def _sigmoid(x):
    return 1.0 / (1.0 + jnp.exp(-x))


def _cast_rows(src_ref, dst_ref, rows, chunk=256):
    def body(c, carry):
        r = pl.multiple_of(c * chunk, chunk)
        dst_ref[pl.ds(r, chunk), :] = src_ref[pl.ds(r, chunk), :].astype(BF16)
        return carry
    lax.fori_loop(0, rows // chunk, body, 0)


def _dot(a, b):
    return jnp.dot(a, b, preferred_element_type=F32)


def _dot_nt(a, b):
    return lax.dot_general(a, b, (((1,), (1,)), ((), ())), preferred_element_type=F32)


def _rmsnorm_kernel(x_ref, g_ref, o_ref):
    o_ref[...] = _rms(x_ref[...], g_ref[...]).astype(o_ref.dtype)


def rmsnorm_bf16(x, g, tm=512):
    m, d = x.shape
    return pl.pallas_call(
        _rmsnorm_kernel,
        grid=(m // tm,),
        in_specs=[pl.BlockSpec((tm, d), lambda i: (i, 0)),
                  pl.BlockSpec((1, d), lambda i: (0, 0))],
        out_specs=pl.BlockSpec((tm, d), lambda i: (i, 0)),
        out_shape=jax.ShapeDtypeStruct((m, d), BF16),
        compiler_params=_params(1, 4 * tm * d * 6 + (8 << 20)),
        name="rmsnorm_bf16",
    )(x, g.reshape(1, d))


def _ws_plain_kernel(x_ref, w_ref, o_ref, wb_ref):
    @pl.when(pl.program_id(1) == 0)
    def _():
        _cast_rows(w_ref, wb_ref, w_ref.shape[0])
    o_ref[...] = _dot(x_ref[...], wb_ref[...]).astype(o_ref.dtype)


def _ws_residual_kernel(x_ref, w_ref, r_ref, o_ref, wb_ref):
    @pl.when(pl.program_id(1) == 0)
    def _():
        _cast_rows(w_ref, wb_ref, w_ref.shape[0])
    o_ref[...] = r_ref[...] + _dot(x_ref[...], wb_ref[...])


def _ws_swiglu_kernel(x_ref, wg_ref, wu_ref, o_ref, wgb_ref, wub_ref):
    @pl.when(pl.program_id(1) == 0)
    def _():
        _cast_rows(wg_ref, wgb_ref, wg_ref.shape[0])
        _cast_rows(wu_ref, wub_ref, wu_ref.shape[0])
    x = x_ref[...]
    g = _dot(x, wgb_ref[...])
    u = _dot(x, wub_ref[...])
    o_ref[...] = (g * _sigmoid(g) * u).astype(o_ref.dtype)


def ws_matmul(x, w, layer, *, tm, tn, out_dtype, residual=None, name):
    m, k = x.shape
    n = w.shape[2]
    in_specs = [pl.BlockSpec((tm, k), lambda j, i: (i, 0)),
                pl.BlockSpec((None, k, tn), lambda j, i: (layer, 0, j))]
    args = [x, w]
    kern = _ws_plain_kernel
    if residual is not None:
        in_specs.append(pl.BlockSpec((tm, tn), lambda j, i: (i, j)))
        args.append(residual)
        kern = _ws_residual_kernel
    vmem = 2 * (tm * k * 2 + k * tn * 4 + 2 * tm * tn * 4) + k * tn * 2 + tm * tn * 4 + (6 << 20)
    return pl.pallas_call(
        kern,
        grid=(n // tn, m // tm),
        in_specs=in_specs,
        out_specs=pl.BlockSpec((tm, tn), lambda j, i: (i, j)),
        out_shape=jax.ShapeDtypeStruct((m, n), out_dtype),
        scratch_shapes=[pltpu.VMEM((k, tn), BF16)],
        compiler_params=_params(2, vmem),
        name=name,
    )(*args)


def ws_swiglu(x, w_up, layer, d_ff, *, tm, tn, name):
    m, k = x.shape
    nb = d_ff // tn
    vmem = 2 * (tm * k * 2 + 2 * k * tn * 4 + tm * tn * 2) + 2 * k * tn * 2 + 3 * tm * tn * 4 + (6 << 20)
    return pl.pallas_call(
        _ws_swiglu_kernel,
        grid=(nb, m // tm),
        in_specs=[pl.BlockSpec((tm, k), lambda j, i: (i, 0)),
                  pl.BlockSpec((None, k, tn), lambda j, i: (layer, 0, j)),
                  pl.BlockSpec((None, k, tn), lambda j, i: (layer, 0, j + nb))],
        out_specs=pl.BlockSpec((tm, tn), lambda j, i: (i, j)),
        out_shape=jax.ShapeDtypeStruct((m, d_ff), BF16),
        scratch_shapes=[pltpu.VMEM((k, tn), BF16), pltpu.VMEM((k, tn), BF16)],
        compiler_params=_params(2, vmem),
        name=name,
    )(x, w_up, w_up)


NA_ROWS = SEQ // GRID_W
NA_NBLK = SEQ // NA_QBLK
NA_PAIRS = NA_KWIN // NA_QBLK
NA_PATTERN_BLOCKS = (0, 1, 2, NA_NBLK - 2, NA_NBLK - 1)


def _na_key_start(blk):
    return min(max(blk - 2, 0), NA_NBLK - NA_PAIRS)


def _na_pattern(i):
    return jnp.where(i < 2, i, jnp.where(i <= NA_NBLK - 3, 2, i - (NA_NBLK - 5)))


def _na_bias_table(rpb):
    hh = rpb.shape[0]
    span = GRID_W - 1
    left = span - (NA_WIN_C - 1)
    right = 2 * span + 1 - left - (2 * NA_WIN_C - 1)
    ext = jnp.pad(rpb, ((0, 0), (0, 0), (left, right)), constant_values=NEG)
    toep = jnp.stack([ext[:, :, span - qc:span - qc + GRID_W] for qc in range(GRID_W)], axis=2)
    qc = np.arange(GRID_W)
    cs = np.clip(qc - NA_WIN_C // 2, 0, GRID_W - NA_WIN_C)
    kc = np.arange(GRID_W)
    col_ok = (kc[None, :] >= cs[:, None]) & (kc[None, :] < cs[:, None] + NA_WIN_C)
    toep = jnp.where(col_ok[None, None], toep, NEG)
    neg = jnp.full((hh, GRID_W, GRID_W), NEG, rpb.dtype)
    patterns = []
    for blk in NA_PATTERN_BLOCKS:
        q_rows = []
        for a in range(NA_QBLK // GRID_W):
            qr = 2 * blk + a
            rs = min(max(qr - NA_WIN_R // 2, 0), NA_ROWS - NA_WIN_R)
            k_blocks = []
            for c in range(NA_KWIN // GRID_W):
                kr = 2 * _na_key_start(blk) + c
                ok = rs <= kr < rs + NA_WIN_R
                k_blocks.append(toep[:, kr - qr + NA_WIN_R - 1] if ok else neg)
            q_rows.append(jnp.concatenate(k_blocks, axis=-1))
        patterns.append(jnp.concatenate(q_rows, axis=-2))
    return jnp.stack(patterns, axis=0)


def _na_kernel(q_ref, k_ref, v_ref, bias_ref, gq_ref, gk_ref, o_ref, kn_ref, vb_ref):
    i = pl.program_id(1)

    @pl.when(i == 0)
    def _():
        def body(c, carry):
            r = pl.multiple_of(c * 256, 256)
            for h in range(NA_HEADS):
                hs = slice(h * NA_HD, (h + 1) * NA_HD)
                kn_ref[pl.ds(r, 256), hs] = _rms(k_ref[pl.ds(r, 256), hs], gk_ref[...]).astype(BF16)
            vb_ref[pl.ds(r, 256), :] = v_ref[pl.ds(r, 256), :].astype(BF16)
            return carry
        lax.fori_loop(0, SEQ // 256, body, 0)

    start = pl.multiple_of(jnp.clip(i - 2, 0, NA_NBLK - NA_PAIRS) * NA_QBLK, NA_QBLK)
    scale = NA_HD ** -0.5
    for h in range(NA_HEADS):
        hs = slice(h * NA_HD, (h + 1) * NA_HD)
        q = (_rms(q_ref[:, hs], gq_ref[...]) * scale).astype(BF16)
        s = _dot_nt(q, kn_ref[pl.ds(start, NA_KWIN), hs]) + bias_ref[0, h]
        m = jnp.max(s, axis=-1, keepdims=True)
        p = jnp.exp(s - m)
        l = jnp.sum(p, axis=-1, keepdims=True)
        o = _dot(p.astype(BF16), vb_ref[pl.ds(start, NA_KWIN), hs]) / l
        o_ref[:, hs] = o.astype(o_ref.dtype)


def na_attention(z, bias, gq, gk, batch):
    nblk = SEQ // NA_QBLK
    w = NA_HEADS * NA_HD
    vmem = (2 * (NA_QBLK * w * 4 + 2 * SEQ * w * 4 + NA_HEADS * NA_QBLK * NA_KWIN * 4 + NA_QBLK * w * 2)
            + 2 * SEQ * w * 2 + (8 << 20))
    return pl.pallas_call(
        _na_kernel,
        grid=(batch, nblk),
        in_specs=[pl.BlockSpec((NA_QBLK, w), lambda b, i: (b * nblk + i, COL_NA_Q // w)),
                  pl.BlockSpec((SEQ, w), lambda b, i: (b, COL_NA_K // w)),
                  pl.BlockSpec((SEQ, w), lambda b, i: (b, COL_NA_V // w)),
                  pl.BlockSpec((1, NA_HEADS, NA_QBLK, NA_KWIN), lambda b, i: (_na_pattern(i), 0, 0, 0)),
                  pl.BlockSpec((1, NA_HD), lambda b, i: (0, 0)),
                  pl.BlockSpec((1, NA_HD), lambda b, i: (0, 0))],
        out_specs=pl.BlockSpec((NA_QBLK, w), lambda b, i: (b * nblk + i, 0)),
        out_shape=jax.ShapeDtypeStruct((batch * SEQ, w), BF16),
        scratch_shapes=[pltpu.VMEM((SEQ, w), BF16), pltpu.VMEM((SEQ, w), BF16)],
        compiler_params=_params(2, vmem),
        name="na_attention",
    )(z, z, z, bias, gq.reshape(1, NA_HD), gk.reshape(1, NA_HD))


def _rope_tables(width, head_dim, first_lane):
    half = head_dim // 2
    freqs = ROPE_THETA ** (-2.0 * np.arange(half, dtype=np.float32) / head_dim)
    ang = jnp.arange(SEQ, dtype=F32)[:, None] * jnp.asarray(freqs, F32)[None, :]
    cos, sin = jnp.cos(ang), jnp.sin(ang)
    lane = np.arange(width)
    rel = lane - first_lane
    in_rope = (rel >= 0) & (rel < (width - first_lane if first_lane == 0 else head_dim))
    p = np.where(in_rope, rel % head_dim, 0)
    j = p % half
    first_half = in_rope & (p < half)
    second_half = in_rope & (p >= half)
    cos_t = jnp.where(in_rope[None, :], cos[:, j], 1.0)
    sin_a = jnp.where(first_half[None, :], -sin[:, j], 0.0)
    sin_b = jnp.where(second_half[None, :], sin[:, j], 0.0)
    return cos_t, sin_a, sin_b


def _apply_rope(x, cos_t, sin_a, sin_b, half):
    n = x.shape[-1]
    return (x * cos_t + pltpu.roll(x, n - half, axis=1) * sin_a
            + pltpu.roll(x, half, axis=1) * sin_b)


def _mla_prep_kernel(cq_ref, ckv_ref, kpe_ref, wq_ref, wkv_ref, gcq_ref, gckv_ref,
                     gqn_ref, gkn_ref, cos_ref, sa_ref, sb_ref, q_ref, k_ref, v_ref):
    half = MLA_ROPE // 2
    cos_t, sin_a, sin_b = cos_ref[...], sa_ref[...], sb_ref[...]
    scale = MLA_QK ** -0.5
    cq = _rms(cq_ref[...], gcq_ref[...]).astype(BF16)
    q_raw = _dot(cq, wq_ref[...])
    ckv = _rms(ckv_ref[...], gckv_ref[...]).astype(BF16)
    kv_raw = _dot(ckv, wkv_ref[...])
    kpe = kpe_ref[...]
    kpe_ss = jnp.sum(kpe * kpe, axis=-1, keepdims=True)
    for h in range(MLA_HEADS):
        cs = slice(h * MLA_PAD, (h + 1) * MLA_PAD)
        qc = q_raw[:, cs]
        ms = jnp.sum(qc * qc, axis=-1, keepdims=True) * (1.0 / MLA_QK)
        qn = qc * lax.rsqrt(ms + EPS) * gqn_ref[...]
        q_ref[:, cs] = (_apply_rope(qn, cos_t, sin_a, sin_b, half) * scale).astype(BF16)
        kn = kv_raw[:, h * MLA_PAD:h * MLA_PAD + MLA_NOPE]
        ms = (jnp.sum(kn * kn, axis=-1, keepdims=True) + kpe_ss) * (1.0 / MLA_QK)
        kc = jnp.concatenate([kn, kpe], axis=-1) * lax.rsqrt(ms + EPS) * gkn_ref[...]
        k_ref[:, cs] = _apply_rope(kc, cos_t, sin_a, sin_b, half).astype(BF16)
        v_ref[:, h * MLA_NOPE:(h + 1) * MLA_NOPE] = kv_raw[:, h * MLA_PAD + MLA_NOPE:(h + 1) * MLA_PAD].astype(BF16)


def mla_prep(z, wq_pad, wkv, gcq, gckv, gqn_pad, gkn_pad, tabs, tm=512):
    t = z.shape[0]
    sb = SEQ // tm
    qw = MLA_HEADS * MLA_PAD
    row = lambda i: (i, 0)
    const = lambda i: (0, 0)
    pos = lambda i: (i % sb, 0)
    return pl.pallas_call(
        _mla_prep_kernel,
        grid=(t // tm,),
        in_specs=[pl.BlockSpec((tm, MLA_Q_RANK), lambda i: (i, COL_CQ // MLA_Q_RANK)),
                  pl.BlockSpec((tm, MLA_KV_RANK), lambda i: (i, COL_CKV // MLA_KV_RANK)),
                  pl.BlockSpec((tm, LANES), lambda i: (i, COL_KPE // LANES)),
                  pl.BlockSpec((MLA_Q_RANK, qw), const),
                  pl.BlockSpec((MLA_KV_RANK, qw), const),
                  pl.BlockSpec((1, MLA_Q_RANK), const),
                  pl.BlockSpec((1, MLA_KV_RANK), const),
                  pl.BlockSpec((1, MLA_PAD), const),
                  pl.BlockSpec((1, MLA_PAD), const),
                  pl.BlockSpec((tm, MLA_PAD), pos),
                  pl.BlockSpec((tm, MLA_PAD), pos),
                  pl.BlockSpec((tm, MLA_PAD), pos)],
        out_specs=[pl.BlockSpec((tm, qw), row), pl.BlockSpec((tm, qw), row),
                   pl.BlockSpec((tm, MLA_HEADS * MLA_NOPE), row)],
        out_shape=[jax.ShapeDtypeStruct((t, qw), BF16), jax.ShapeDtypeStruct((t, qw), BF16),
                   jax.ShapeDtypeStruct((t, MLA_HEADS * MLA_NOPE), BF16)],
        compiler_params=_params(1, 40 << 20),
        name="mla_prep",
    )(z, z, z, wq_pad, wkv, gcq.reshape(1, -1), gckv.reshape(1, -1), gqn_pad, gkn_pad, *tabs)


def _mla_attn_kernel(q_ref, k_ref, v_ref, o_ref):
    for h in range(MLA_HEADS):
        cs = slice(h * MLA_PAD, (h + 1) * MLA_PAD)
        vs = slice(h * MLA_NOPE, (h + 1) * MLA_NOPE)
        s = _dot_nt(q_ref[:, cs], k_ref[:, cs])
        m = jnp.max(s, axis=-1, keepdims=True)
        p = jnp.exp(s - m)
        l = jnp.sum(p, axis=-1, keepdims=True)
        o = _dot(p.astype(BF16), v_ref[:, vs]) / l
        o_ref[:, vs] = o.astype(o_ref.dtype)


def mla_attention(q, k, v, batch, tq=256):
    nq = SEQ // tq
    qw = MLA_HEADS * MLA_PAD
    vw = MLA_HEADS * MLA_NOPE
    vmem = 2 * (tq * qw * 2 + SEQ * qw * 2 + SEQ * vw * 2 + tq * vw * 2) + 4 * tq * SEQ * 4 + (8 << 20)
    return pl.pallas_call(
        _mla_attn_kernel,
        grid=(batch, nq),
        in_specs=[pl.BlockSpec((tq, qw), lambda b, i: (b * nq + i, 0)),
                  pl.BlockSpec((SEQ, qw), lambda b, i: (b, 0)),
                  pl.BlockSpec((SEQ, vw), lambda b, i: (b, 0))],
        out_specs=pl.BlockSpec((tq, vw), lambda b, i: (b * nq + i, 0)),
        out_shape=jax.ShapeDtypeStruct((batch * SEQ, vw), BF16),
        compiler_params=_params(2, vmem),
        name="mla_attention",
    )(q, k, v)


def _rms_halves(x, g):
    lo = lax.broadcasted_iota(jnp.int32, x.shape, 1) < SWA_HD
    x2 = x * x
    s_lo = jnp.sum(jnp.where(lo, x2, 0.0), axis=-1, keepdims=True)
    s_hi = jnp.sum(jnp.where(lo, 0.0, x2), axis=-1, keepdims=True)
    ms = jnp.where(lo, s_lo, s_hi) * (1.0 / SWA_HD)
    return x * lax.rsqrt(ms + EPS) * g


def _swa_kernel(sink_ref, q_ref, k_ref, v_ref, gq_ref, gk_ref, cosq_ref, saq_ref, sbq_ref,
                cosk_ref, sak_ref, sbk_ref, o_ref, kk_ref, vv_ref):
    n = pl.program_id(1)
    nblk = SEQ // SWA_BLOCK
    half = SWA_HD // 2

    @pl.when(n == 0)
    def _():
        zeros = jnp.zeros((SWA_BLOCK, LANES), BF16)
        for c in range(4):
            kk_ref[c, pl.ds(0, SWA_BLOCK), :] = zeros
            kk_ref[c, pl.ds(SEQ + SWA_BLOCK, SWA_BLOCK), :] = zeros
            vv_ref[c, pl.ds(0, SWA_BLOCK), :] = zeros
            vv_ref[c, pl.ds(SEQ + SWA_BLOCK, SWA_BLOCK), :] = zeros

        def body(c, carry):
            r = pl.multiple_of(c * 256, 256)
            dst = pl.ds(r + SWA_BLOCK, 256)
            lo = lax.broadcasted_iota(jnp.int32, (256, LANES), 1) < SWA_HD
            kr = _apply_rope(_rms_halves(k_ref[pl.ds(r, 256), :], gk_ref[...]),
                             cosk_ref[pl.ds(r, 256), :], sak_ref[pl.ds(r, 256), :],
                             sbk_ref[pl.ds(r, 256), :], half)
            ks = pltpu.roll(kr, SWA_HD, axis=1)
            kk_ref[0, dst, :] = jnp.where(lo, kr, 0.0).astype(BF16)
            kk_ref[1, dst, :] = jnp.where(lo, 0.0, ks).astype(BF16)
            kk_ref[2, dst, :] = jnp.where(lo, ks, 0.0).astype(BF16)
            kk_ref[3, dst, :] = jnp.where(lo, 0.0, kr).astype(BF16)
            vr = v_ref[pl.ds(r, 256), :]
            vs = pltpu.roll(vr, SWA_HD, axis=1)
            vv_ref[0, dst, :] = jnp.where(lo, vr, 0.0).astype(BF16)
            vv_ref[1, dst, :] = jnp.where(lo, 0.0, vs).astype(BF16)
            vv_ref[2, dst, :] = jnp.where(lo, vs, 0.0).astype(BF16)
            vv_ref[3, dst, :] = jnp.where(lo, 0.0, vr).astype(BF16)
            return carry
        lax.fori_loop(0, SEQ // 256, body, 0)

    band = pl.ds(pl.multiple_of(n * SWA_BLOCK, SWA_BLOCK), 3 * SWA_BLOCK)
    a = lax.broadcasted_iota(jnp.int32, (SWA_BLOCK, 3 * SWA_BLOCK), 0)
    c = lax.broadcasted_iota(jnp.int32, (SWA_BLOCK, 3 * SWA_BLOCK), 1)
    c_min = jnp.where(n == 0, SWA_BLOCK, 0)
    c_max = jnp.where(n == nblk - 1, 2 * SWA_BLOCK, 3 * SWA_BLOCK)
    valid = (c >= a) & (c <= a + 2 * SWA_BLOCK) & (c >= c_min) & (c < c_max)
    scale = SWA_HD ** -0.5
    for pair in range(SWA_HEADS // 2):
        ps = slice(pair * LANES, (pair + 1) * LANES)
        grp = pair // 2
        q = _apply_rope(_rms_halves(q_ref[:, ps], gq_ref[...]),
                        cosq_ref[...], saq_ref[...], sbq_ref[...], half)
        q = (q * scale).astype(BF16)
        acc = None
        for hf in range(2):
            sink = sink_ref[2 * pair + hf]
            s = _dot_nt(q, kk_ref[2 * grp + hf, band, :])
            s = jnp.where(valid, s, NEG)
            m = jnp.maximum(jnp.max(s, axis=-1, keepdims=True), sink)
            p = jnp.exp(s - m)
            den = jnp.sum(p, axis=-1, keepdims=True) + jnp.exp(sink - m)
            o = _dot(p.astype(BF16), vv_ref[2 * grp + hf, band, :]) / den
            acc = o if acc is None else acc + o
        o_ref[:, ps] = acc.astype(o_ref.dtype)


def swa_attention(z, sink, gq, gk, tabs, batch):
    nblk = SEQ // SWA_BLOCK
    qw = SWA_HEADS * SWA_HD
    cos_t, sin_a, sin_b = tabs
    g2 = lambda g: jnp.concatenate([g, g]).reshape(1, LANES)
    blk = lambda b, n: (n, 0)
    full = lambda b, n: (0, 0)
    tab_blk = pl.BlockSpec((SWA_BLOCK, LANES), blk)
    tab_full = pl.BlockSpec((SEQ, LANES), full)
    vmem = (2 * (SWA_BLOCK * qw * 6 + 2 * SEQ * LANES * 4 + 3 * SEQ * LANES * 4 + 3 * SWA_BLOCK * LANES * 4)
            + 8 * (SEQ + 2 * SWA_BLOCK) * LANES * 2 + (8 << 20))
    return pl.pallas_call(
        _swa_kernel,
        grid=(batch, nblk),
        in_specs=[pl.BlockSpec(memory_space=pltpu.SMEM),
                  pl.BlockSpec((SWA_BLOCK, qw), lambda b, n: (b * nblk + n, COL_SWQ // qw)),
                  pl.BlockSpec((SEQ, LANES), lambda b, n: (b, COL_SWK // LANES)),
                  pl.BlockSpec((SEQ, LANES), lambda b, n: (b, COL_SWV // LANES)),
                  pl.BlockSpec((1, LANES), full), pl.BlockSpec((1, LANES), full),
                  tab_blk, tab_blk, tab_blk, tab_full, tab_full, tab_full],
        out_specs=pl.BlockSpec((SWA_BLOCK, qw), lambda b, n: (b * nblk + n, 0)),
        out_shape=jax.ShapeDtypeStruct((batch * SEQ, qw), BF16),
        scratch_shapes=[pltpu.VMEM((4, SEQ + 2 * SWA_BLOCK, LANES), BF16),
                        pltpu.VMEM((4, SEQ + 2 * SWA_BLOCK, LANES), BF16)],
        compiler_params=_params(2, vmem),
        name="swa_attention",
    )(sink, z, z, z, g2(gq), g2(gk), cos_t, sin_a, sin_b, cos_t, sin_a, sin_b)


def _mem_attn_kernel(q_ref, k_ref, v_ref, gq_ref, gk_ref, o_ref):
    scale = MEM_HD ** -0.5
    for h in range(MEM_HEADS):
        hs = slice(h * MEM_HD, (h + 1) * MEM_HD)
        q = (_rms(q_ref[:, hs], gq_ref[...]) * scale).astype(BF16)
        k = _rms(k_ref[:, hs], gk_ref[...]).astype(BF16)
        s = _dot_nt(q, k)
        m = jnp.max(s, axis=-1, keepdims=True)
        p = jnp.exp(s - m)
        l = jnp.sum(p, axis=-1, keepdims=True)
        o = _dot(p.astype(BF16), v_ref[:, hs].astype(BF16)) / l
        o_ref[:, hs] = o.astype(o_ref.dtype)


def mem_attention(z, memkv, gq, gk, batch, tq=512):
    nq = SEQ // tq
    w = MEM_HEADS * MEM_HD
    return pl.pallas_call(
        _mem_attn_kernel,
        grid=(batch, nq),
        in_specs=[pl.BlockSpec((tq, w), lambda b, i: (b * nq + i, COL_MEQ // w)),
                  pl.BlockSpec((MEM_LEN, w), lambda b, i: (b, 0)),
                  pl.BlockSpec((MEM_LEN, w), lambda b, i: (b, 1)),
                  pl.BlockSpec((1, MEM_HD), lambda b, i: (0, 0)),
                  pl.BlockSpec((1, MEM_HD), lambda b, i: (0, 0))],
        out_specs=pl.BlockSpec((tq, w), lambda b, i: (b * nq + i, 0)),
        out_shape=jax.ShapeDtypeStruct((batch * SEQ, w), BF16),
        compiler_params=_params(2, 24 << 20),
        name="mem_attention",
    )(z, memkv, memkv, gq.reshape(1, MEM_HD), gk.reshape(1, MEM_HD))


def _merge_kernel(u_ref, b0_ref, b1_ref, b2_ref, b3_ref, wg_ref, wb_ref, bg_ref, o_ref, wgs_ref, wbs_ref):
    branches = (b0_ref, b1_ref, b2_ref, b3_ref)

    @pl.when(pl.program_id(1) == 0)
    def _():
        def body(c, carry):
            r = pl.multiple_of(c * 256, 256)
            for n in range(4):
                wgs_ref[n, pl.ds(r, 256), :] = wg_ref[pl.ds(r, 256), n, :].astype(BF16)
            return carry
        lax.fori_loop(0, D_MODEL // 256, body, 0)
        for n in range(4):
            wbs_ref[n] = wb_ref[n].astype(BF16)

    u = u_ref[...]
    acc = None
    for n in range(4):
        gate = _sigmoid(_dot(u, wgs_ref[n]) + bg_ref[n:n + 1, :])
        term = gate * _dot(branches[n][...], wbs_ref[n])
        acc = term if acc is None else acc + term
    o_ref[...] = acc.astype(o_ref.dtype)


def gated_merge(u, branches, w_gate, w_branch, b_gate, layer, tm=1024, tn=256):
    t = u.shape[0]
    bw = branches[0].shape[1]
    vmem = (2 * (tm * D_MODEL * 2 + 4 * tm * bw * 2 + 4 * D_MODEL * tn * 4 + 4 * bw * tn * 4 + tm * tn * 2)
            + 4 * D_MODEL * tn * 2 + 4 * bw * tn * 2 + 4 * tm * tn * 4 + (6 << 20))
    return pl.pallas_call(
        _merge_kernel,
        grid=(D_MODEL // tn, t // tm),
        in_specs=[pl.BlockSpec((tm, D_MODEL), lambda j, i: (i, 0))]
        + [pl.BlockSpec((tm, bw), lambda j, i: (i, 0)) for _ in range(4)]
        + [pl.BlockSpec((None, D_MODEL, 4, tn), lambda j, i: (layer, 0, 0, j)),
           pl.BlockSpec((None, 4, bw, tn), lambda j, i: (layer, 0, 0, j)),
           pl.BlockSpec((None, 4, tn), lambda j, i: (layer, 0, j))],
        out_specs=pl.BlockSpec((tm, tn), lambda j, i: (i, j)),
        out_shape=jax.ShapeDtypeStruct((t, D_MODEL), BF16),
        scratch_shapes=[pltpu.VMEM((4, D_MODEL, tn), BF16), pltpu.VMEM((4, bw, tn), BF16)],
        compiler_params=_params(2, vmem),
        name="gated_merge",
    )(u, *branches, w_gate, w_branch, b_gate)


def _router_kernel(h_ref, g_ref, wr_ref, ri_ref, rw_ref, cnt_ref, carry_ref):
    @pl.when(pl.program_id(0) == 0)
    def _():
        carry_ref[...] = jnp.zeros_like(carry_ref)

    tm = h_ref.shape[0]
    hn = _rms(h_ref[...], g_ref[...])
    logits = jnp.dot(hn, wr_ref[...], precision=lax.Precision.HIGHEST, preferred_element_type=F32)
    lane = lax.broadcasted_iota(jnp.int32, (tm, LANES), 1)
    lane_f = lane.astype(F32)
    logits = jnp.where(lane < N_EXPERTS, logits, -jnp.inf)
    m1 = jnp.max(logits, axis=-1, keepdims=True)
    i1 = jnp.min(jnp.where(logits == m1, lane_f, float(LANES)), axis=-1, keepdims=True)
    oh1 = lane_f == i1
    rest = jnp.where(oh1, -jnp.inf, logits)
    m2 = jnp.max(rest, axis=-1, keepdims=True)
    i2 = jnp.min(jnp.where(rest == m2, lane_f, float(LANES)), axis=-1, keepdims=True)
    oh2 = lane_f == i2
    e2 = jnp.exp(m2 - m1)
    w1 = 1.0 / (1.0 + e2)
    w2 = e2 / (1.0 + e2)
    chosen = jnp.where(oh1 | oh2, 1.0, 0.0)
    before = (lax.broadcasted_iota(jnp.int32, (tm, tm), 1)
              < lax.broadcasted_iota(jnp.int32, (tm, tm), 0))
    prefix = _dot(jnp.where(before, 1.0, 0.0).astype(BF16), chosen.astype(BF16)) + carry_ref[...]
    r1 = jnp.sum(jnp.where(oh1, prefix, 0.0), axis=-1, keepdims=True).astype(jnp.int32)
    r2 = jnp.sum(jnp.where(oh2, prefix, 0.0), axis=-1, keepdims=True).astype(jnp.int32)
    carry_ref[...] += jnp.sum(chosen, axis=0, keepdims=True)
    ri_ref[...] = jnp.where(lane == 0, i1.astype(jnp.int32),
                            jnp.where(lane == 1, i2.astype(jnp.int32),
                                      jnp.where(lane == 2, r1, jnp.where(lane == 3, r2, 0))))
    rw_ref[...] = jnp.where(lane == 0, w1, jnp.where(lane == 1, w2, 0.0))
    cnt_ref[...] = carry_ref[...]


def moe_router(h, g, w_router, tm=512):
    t, d = h.shape
    wr = jnp.pad(w_router, ((0, 0), (0, LANES - N_EXPERTS)))
    return pl.pallas_call(
        _router_kernel,
        grid=(t // tm,),
        in_specs=[pl.BlockSpec((tm, d), lambda i: (i, 0)),
                  pl.BlockSpec((1, d), lambda i: (0, 0)),
                  pl.BlockSpec((d, LANES), lambda i: (0, 0))],
        out_specs=[pl.BlockSpec((tm, LANES), lambda i: (i, 0)),
                   pl.BlockSpec((tm, LANES), lambda i: (i, 0)),
                   pl.BlockSpec((1, LANES), lambda i: (0, 0))],
        out_shape=[jax.ShapeDtypeStruct((t, LANES), jnp.int32),
                   jax.ShapeDtypeStruct((t, LANES), F32),
                   jax.ShapeDtypeStruct((1, LANES), F32)],
        scratch_shapes=[pltpu.VMEM((1, LANES), F32)],
        compiler_params=_params(1, 32 << 20),
        name="moe_router",
    )(h, g.reshape(1, d), wr)


def _row_copy(src_hbm, row, dst_vmem, r, sem):
    return pltpu.make_async_copy(src_hbm.at[pl.ds(row, 1), :], dst_vmem.at[pl.ds(r, 1), :], sem)


def _moe_gather_kernel(src_ref, h_hbm, g_ref, o_ref, buf_ref, sem):
    tg = buf_ref.shape[1]
    i = pl.program_id(0)
    slot = i % 2

    def start_rows(step, dst_slot):
        def issue(r, carry):
            _row_copy(h_hbm, src_ref[step * tg + r], buf_ref.at[dst_slot], r, sem.at[dst_slot]).start()
            return carry
        lax.fori_loop(0, tg, issue, 0, unroll=8)

    @pl.when(i == 0)
    def _():
        start_rows(0, 0)

    @pl.when(i + 1 < pl.num_programs(0))
    def _():
        start_rows(i + 1, 1 - slot)

    def drain(r, carry):
        _row_copy(h_hbm, 0, buf_ref.at[slot], r, sem.at[slot]).wait()
        return carry
    lax.fori_loop(0, tg, drain, 0, unroll=8)
    o_ref[...] = _rms(buf_ref[slot], g_ref[...]).astype(o_ref.dtype)


def moe_gather(src_tok, h, g, rows, tg=256):
    d = h.shape[1]
    return pl.pallas_call(
        _moe_gather_kernel,
        grid_spec=pltpu.PrefetchScalarGridSpec(
            num_scalar_prefetch=1,
            grid=(rows // tg,),
            in_specs=[pl.BlockSpec(memory_space=pl.ANY),
                      pl.BlockSpec((1, d), lambda i, s: (0, 0))],
            out_specs=pl.BlockSpec((tg, d), lambda i, s: (i, 0)),
            scratch_shapes=[pltpu.VMEM((2, tg, d), F32), pltpu.SemaphoreType.DMA((2,))]),
        out_shape=jax.ShapeDtypeStruct((rows, d), BF16),
        compiler_params=_params(1, 24 << 20),
        name="moe_gather",
    )(src_tok, h, g.reshape(1, d))


def _new_expert(te_ref, i):
    return jnp.logical_or(i == 0, te_ref[i] != te_ref[jnp.maximum(i - 1, 0)])


def _moe_up_kernel(te_ref, nt_ref, x_ref, wg_ref, wu_ref, o_ref, wgb_ref, wub_ref):
    i = pl.program_id(1)
    active = i < nt_ref[0]

    @pl.when(jnp.logical_and(active, _new_expert(te_ref, i)))
    def _():
        _cast_rows(wg_ref.at[0], wgb_ref, D_MODEL)
        _cast_rows(wu_ref.at[0], wub_ref, D_MODEL)

    @pl.when(active)
    def _():
        x = x_ref[...]
        g = _dot(x, wgb_ref[...])
        u = _dot(x, wub_ref[...])
        o_ref[...] = (g * _sigmoid(g) * u).astype(o_ref.dtype)

    @pl.when(jnp.logical_not(active))
    def _():
        o_ref[...] = jnp.zeros_like(o_ref)


def moe_up(xs, w_up, tile_e, n_tiles, tf=512):
    rows, d = xs.shape
    nf = D_FF_EXPERT // tf
    row = lambda j, i, te, nt: (jnp.maximum(jnp.minimum(i, nt[0] - 1), 0), 0)
    vmem = 2 * (MOE_TM * d * 2 + 2 * d * tf * 4 + MOE_TM * tf * 2) + 2 * d * tf * 2 + 3 * MOE_TM * tf * 4 + (6 << 20)
    return pl.pallas_call(
        _moe_up_kernel,
        grid_spec=pltpu.PrefetchScalarGridSpec(
            num_scalar_prefetch=2,
            grid=(nf, rows // MOE_TM),
            in_specs=[pl.BlockSpec((MOE_TM, d), row),
                      pl.BlockSpec((1, d, tf), lambda j, i, te, nt: (te[i], 0, j)),
                      pl.BlockSpec((1, d, tf), lambda j, i, te, nt: (te[i], 0, j + nf))],
            out_specs=pl.BlockSpec((MOE_TM, tf), lambda j, i, te, nt: (i, j)),
            scratch_shapes=[pltpu.VMEM((d, tf), BF16), pltpu.VMEM((d, tf), BF16)]),
        out_shape=jax.ShapeDtypeStruct((rows, D_FF_EXPERT), BF16),
        compiler_params=_params(2, vmem),
        name="moe_up",
    )(tile_e, n_tiles, xs, w_up, w_up)


def _moe_down_kernel(te_ref, nt_ref, a_ref, w_ref, *rest):
    if len(rest) == 3:
        part_ref, o_ref, wb_ref = rest
    else:
        part_ref, (o_ref, wb_ref) = None, rest
    i = pl.program_id(1)
    active = i < nt_ref[0]

    @pl.when(jnp.logical_and(active, _new_expert(te_ref, i)))
    def _():
        _cast_rows(w_ref.at[0], wb_ref, w_ref.shape[1])

    @pl.when(active)
    def _():
        y = _dot(a_ref[...], wb_ref[...])
        o_ref[...] = y if part_ref is None else part_ref[...] + y

    @pl.when(jnp.logical_not(active))
    def _():
        o_ref[...] = jnp.zeros_like(o_ref)


def moe_down(act, w_down, tile_e, n_tiles, k_half, partial=None, tn=512):
    rows, f = act.shape
    fk = f // 2
    in_specs = [pl.BlockSpec((MOE_TM, fk), lambda j, i, te, nt: (jnp.maximum(jnp.minimum(i, nt[0] - 1), 0), k_half)),
                pl.BlockSpec((1, fk, tn), lambda j, i, te, nt: (te[i], k_half, j))]
    args = [tile_e, n_tiles, act, w_down]
    if partial is not None:
        in_specs.append(pl.BlockSpec((MOE_TM, tn), lambda j, i, te, nt: (i, j)))
        args.append(partial)
    vmem = 2 * (MOE_TM * fk * 2 + fk * tn * 4 + 2 * MOE_TM * tn * 4) + fk * tn * 2 + MOE_TM * tn * 4 + (6 << 20)
    return pl.pallas_call(
        _moe_down_kernel,
        grid_spec=pltpu.PrefetchScalarGridSpec(
            num_scalar_prefetch=2,
            grid=(D_MODEL // tn, rows // MOE_TM),
            in_specs=in_specs,
            out_specs=pl.BlockSpec((MOE_TM, tn), lambda j, i, te, nt: (i, j)),
            scratch_shapes=[pltpu.VMEM((fk, tn), BF16)]),
        out_shape=jax.ShapeDtypeStruct((rows, D_MODEL), F32),
        compiler_params=_params(2, vmem),
        name="moe_down_%d" % k_half,
    )(*args)


def _moe_combine_kernel(pos_ref, h_ref, rw_ref, y_hbm, o_ref, b1_ref, b2_ref, sem):
    tc = b1_ref.shape[1]
    i = pl.program_id(0)
    slot = i % 2

    def start_rows(step, dst_slot):
        def issue(r, carry):
            t = step * tc + r
            _row_copy(y_hbm, pos_ref[2 * t], b1_ref.at[dst_slot], r, sem.at[dst_slot]).start()
            _row_copy(y_hbm, pos_ref[2 * t + 1], b2_ref.at[dst_slot], r, sem.at[dst_slot]).start()
            return carry
        lax.fori_loop(0, tc, issue, 0, unroll=8)

    @pl.when(i == 0)
    def _():
        start_rows(0, 0)

    @pl.when(i + 1 < pl.num_programs(0))
    def _():
        start_rows(i + 1, 1 - slot)

    def drain(r, carry):
        _row_copy(y_hbm, 0, b1_ref.at[slot], r, sem.at[slot]).wait()
        _row_copy(y_hbm, 0, b2_ref.at[slot], r, sem.at[slot]).wait()
        return carry
    lax.fori_loop(0, tc, drain, 0, unroll=8)
    o_ref[...] = h_ref[...] + rw_ref[:, 0:1] * b1_ref[slot] + rw_ref[:, 1:2] * b2_ref[slot]


def moe_combine(pos_flat, h, rw, y, tc=256):
    t, d = h.shape
    return pl.pallas_call(
        _moe_combine_kernel,
        grid_spec=pltpu.PrefetchScalarGridSpec(
            num_scalar_prefetch=1,
            grid=(t // tc,),
            in_specs=[pl.BlockSpec((tc, d), lambda i, p: (i, 0)),
                      pl.BlockSpec((tc, LANES), lambda i, p: (i, 0)),
                      pl.BlockSpec(memory_space=pl.ANY)],
            out_specs=pl.BlockSpec((tc, d), lambda i, p: (i, 0)),
            scratch_shapes=[pltpu.VMEM((2, tc, d), F32), pltpu.VMEM((2, tc, d), F32),
                            pltpu.SemaphoreType.DMA((2,))]),
        out_shape=jax.ShapeDtypeStruct((t, d), F32),
        compiler_params=_params(1, 32 << 20),
        name="moe_combine",
    )(pos_flat, h, rw, y)


def moe_block(h, g, w_router, w_up, w_down):
    t = h.shape[0]
    rows = 2 * t + N_EXPERTS * MOE_TM
    n_row_tiles = rows // MOE_TM
    ri, rw, cnt = moe_router(h, g, w_router)
    counts = cnt[0, :N_EXPERTS].astype(jnp.int32)
    padded = ((counts + MOE_TM - 1) // MOE_TM) * MOE_TM
    ends = jnp.cumsum(padded)
    starts = ends - padded
    pos = starts[ri[:, 0:2]] + ri[:, 2:4]
    tok = jnp.broadcast_to(jnp.arange(t, dtype=jnp.int32)[:, None], (t, 2))
    src_tok = jnp.zeros((rows,), jnp.int32).at[pos.reshape(-1)].set(tok.reshape(-1))
    n_tiles = (ends[-1] // MOE_TM).astype(jnp.int32)
    tile_first_row = jnp.arange(n_row_tiles, dtype=jnp.int32) * MOE_TM
    tile_e = jnp.sum((tile_first_row[:, None] >= ends[None, :]).astype(jnp.int32), axis=1)
    tile_e = jnp.minimum(tile_e, N_EXPERTS - 1)
    tile_e = jnp.where(jnp.arange(n_row_tiles) < n_tiles, tile_e, tile_e[n_tiles - 1])
    nt = n_tiles.reshape(1)

    xs = moe_gather(src_tok, h, g, rows)
    act = moe_up(xs, w_up, tile_e, nt)
    y = moe_down(act, w_down, tile_e, nt, 0)
    y = moe_down(act, w_down, tile_e, nt, 1, partial=y)
    return moe_combine(pos.reshape(-1).astype(jnp.int32), h, rw, y)


def _pad_in_proj(w_in):
    segs = np.cumsum([0, 512, 512, 512, 768, 256, 64, 512, 128, 128, 512])
    na_q, na_k, na_v, c_q, c_kv, k_pe, sw_q, sw_k, sw_v, me_q = [
        w_in[:, :, segs[n]:segs[n + 1]] for n in range(10)]
    pad = jnp.zeros(w_in.shape[:2] + (IN_COLS_PAD - COL_KPE - MLA_ROPE,), w_in.dtype)
    return jnp.concatenate([na_q, na_k, na_v, c_q, c_kv, sw_q, me_q, sw_k, sw_v, k_pe, pad], axis=2)


def _pad_heads(x, n_heads, width):
    lead = x.shape[:-1]
    x = x.reshape(lead + (n_heads, width))
    x = jnp.pad(x, [(0, 0)] * len(lead) + [(0, 0), (0, MLA_PAD - width)])
    return x.reshape(lead + (n_heads * MLA_PAD,))


def kernel(x, mem, norm_mix, w_in, na_q_norm, na_k_norm, na_rpb, mla_cq_norm, mla_w_uq, mla_ckv_norm, mla_w_ukv, mla_q_norm, mla_k_norm, swa_q_norm, swa_k_norm, swa_sink, mem_norm, mem_w_kv, mem_q_norm, mem_k_norm, w_branch, w_gate, b_gate, w_o, norm_ffn, ffn_w_up, ffn_w_down, moe_router, moe_w_up, moe_w_down):
    batch, seq, d = x.shape
    assert (seq, d) == (SEQ, D_MODEL) and mem.shape[1] == MEM_LEN
    t = batch * seq
    depth = w_in.shape[0]
    h = x.reshape(t, d)
    mem2 = mem.reshape(batch * MEM_LEN, d)
    mla_tabs = _rope_tables(MLA_PAD, MLA_ROPE, MLA_NOPE)
    swa_tabs = _rope_tables(LANES, SWA_HD, 0)
    w_in_pad = _pad_in_proj(w_in)
    wq_pad = _pad_heads(mla_w_uq, MLA_HEADS, MLA_QK).astype(BF16)
    wkv_bf = mla_w_ukv.astype(BF16)

    for l in range(depth):
        u = rmsnorm_bf16(h, norm_mix[l])
        z = ws_matmul(u, w_in_pad, l, tm=1024, tn=512, out_dtype=F32, name="in_proj")

        o_na = na_attention(z, _na_bias_table(na_rpb[l]), na_q_norm[l], na_k_norm[l], batch)

        q_mla, k_mla, v_mla = mla_prep(
            z, wq_pad[l], wkv_bf[l], mla_cq_norm[l], mla_ckv_norm[l],
            jnp.pad(mla_q_norm[l], (0, MLA_PAD - MLA_QK)).reshape(1, MLA_PAD),
            jnp.pad(mla_k_norm[l], (0, MLA_PAD - MLA_QK)).reshape(1, MLA_PAD), mla_tabs)
        o_mla = mla_attention(q_mla, k_mla, v_mla, batch)

        o_swa = swa_attention(z, swa_sink[l], swa_q_norm[l], swa_k_norm[l], swa_tabs, batch)

        memn = rmsnorm_bf16(mem2, mem_norm[l])
        memkv = ws_matmul(memn, mem_w_kv, l, tm=1024, tn=512, out_dtype=F32, name="mem_kv")
        o_mem = mem_attention(z, memkv, mem_q_norm[l], mem_k_norm[l], batch)

        merged = gated_merge(u, (o_na, o_mla, o_swa, o_mem), w_gate, w_branch, b_gate, l)
        h = ws_matmul(merged, w_o, l, tm=1024, tn=512, out_dtype=F32, residual=h, name="out_proj")

        if l % 2 == 0:
            hn = rmsnorm_bf16(h, norm_ffn[l])
            act = ws_swiglu(hn, ffn_w_up, l // 2, D_FF, tm=1024, tn=512, name="ffn_up")
            h = ws_matmul(act, ffn_w_down, l // 2, tm=512, tn=512, out_dtype=F32, residual=h, name="ffn_down")
        else:
            h = moe_block(h, norm_ffn[l], moe_router[l // 2], moe_w_up[l // 2], moe_w_down[l // 2])
    return h.reshape(batch, seq, d)
```

```python
import functools

import jax
import jax.numpy as jnp
import numpy as np
from jax import lax
from jax.experimental import pallas as pl
from jax.experimental.pallas import tpu as pltpu

F32 = jnp.float32
BF16 = jnp.bfloat16

D_MODEL = 2048
SEQ = 2048
MEM_LEN = 256
GRID_W = 64
ROPE_THETA = 10000.0
EPS = 1e-6
NEG = -1e30

NA_HEADS = 4
NA_HD = 128
NA_WIN_R = 8
NA_WIN_C = 16
NA_QBLK = 128
NA_KWIN = 640

MLA_HEADS = 4
MLA_NOPE = 128
MLA_ROPE = 64
MLA_QK = MLA_NOPE + MLA_ROPE
MLA_PAD = 256
MLA_Q_RANK = 768
MLA_KV_RANK = 256

SWA_HEADS = 8
SWA_KV_HEADS = 2
SWA_HD = 64
SWA_BLOCK = 128

MEM_HEADS = 4
MEM_HD = 128

D_FF = 5632
N_EXPERTS = 8
D_FF_EXPERT = 7168
MOE_TM = 512

LANES = 128
VMEM_CAP = 60000 * 1024

COL_NA_Q, COL_NA_K, COL_NA_V = 0, 512, 1024
COL_CQ, COL_CKV = 1536, 2304
COL_SWQ, COL_MEQ, COL_SWK, COL_SWV, COL_KPE = 2560, 3072, 3584, 3712, 3840
IN_COLS_PAD = 4096


def _params(n_axes, vmem_bytes):
    return pltpu.CompilerParams(
        dimension_semantics=("arbitrary",) * n_axes,
        vmem_limit_bytes=int(min(VMEM_CAP, vmem_bytes)))


def _rms(x, g):
    ms = jnp.mean(x * x, axis=-1, keepdims=True)
    return x * lax.rsqrt(ms + EPS) * g


def _sigmoid(x):
    return 1.0 / (1.0 + jnp.exp(-x))


def _cast_rows(src_ref, dst_ref, rows, chunk=256):
    def body(c, carry):
        r = pl.multiple_of(c * chunk, chunk)
        dst_ref[pl.ds(r, chunk), :] = src_ref[pl.ds(r, chunk), :].astype(BF16)
        return carry
    lax.fori_loop(0, rows // chunk, body, 0)


def _dot(a, b):
    return jnp.dot(a, b, preferred_element_type=F32)


def _dot_nt(a, b):
    return lax.dot_general(a, b, (((1,), (1,)), ((), ())), preferred_element_type=F32)


def _rmsnorm_kernel(x_ref, g_ref, o_ref):
    o_ref[...] = _rms(x_ref[...], g_ref[...]).astype(o_ref.dtype)


def rmsnorm_bf16(x, g, tm=512):
    m, d = x.shape
    return pl.pallas_call(
        _rmsnorm_kernel,
        grid=(m // tm,),
        in_specs=[pl.BlockSpec((tm, d), lambda i: (i, 0)),
                  pl.BlockSpec((1, d), lambda i: (0, 0))],
        out_specs=pl.BlockSpec((tm, d), lambda i: (i, 0)),
        out_shape=jax.ShapeDtypeStruct((m, d), BF16),
        compiler_params=_params(1, 4 * tm * d * 6 + (8 << 20)),
        name="rmsnorm_bf16",
    )(x, g.reshape(1, d))


def _ws_plain_kernel(x_ref, w_ref, o_ref, wb_ref):
    @pl.when(pl.program_id(1) == 0)
    def _():
        _cast_rows(w_ref, wb_ref, w_ref.shape[0])
    o_ref[...] = _dot(x_ref[...], wb_ref[...]).astype(o_ref.dtype)


def _ws_bf16w_kernel(x_ref, w_ref, o_ref):
    o_ref[...] = _dot(x_ref[...], w_ref[...]).astype(o_ref.dtype)


def _ws_residual_kernel(x_ref, w_ref, r_ref, o_ref, wb_ref):
    @pl.when(pl.program_id(1) == 0)
    def _():
        _cast_rows(w_ref, wb_ref, w_ref.shape[0])
    o_ref[...] = r_ref[...] + _dot(x_ref[...], wb_ref[...])


def _ws_swiglu_kernel(x_ref, wg_ref, wu_ref, o_ref, wgb_ref, wub_ref):
    @pl.when(pl.program_id(1) == 0)
    def _():
        _cast_rows(wg_ref, wgb_ref, wg_ref.shape[0])
        _cast_rows(wu_ref, wub_ref, wu_ref.shape[0])
    x = x_ref[...]
    g = _dot(x, wgb_ref[...])
    u = _dot(x, wub_ref[...])
    o_ref[...] = (g * _sigmoid(g) * u).astype(o_ref.dtype)


def ws_matmul(x, w, layer, *, tm, tn, out_dtype, residual=None, name):
    m, k = x.shape
    n = w.shape[2]
    in_specs = [pl.BlockSpec((tm, k), lambda j, i: (i, 0)),
                pl.BlockSpec((None, k, tn), lambda j, i: (layer, 0, j))]
    args = [x, w]
    kern = _ws_plain_kernel
    scratch = [pltpu.VMEM((k, tn), BF16)]
    if residual is not None:
        in_specs.append(pl.BlockSpec((tm, tn), lambda j, i: (i, j)))
        args.append(residual)
        kern = _ws_residual_kernel
    elif w.dtype == BF16:
        kern, scratch = _ws_bf16w_kernel, []
    vmem = 2 * (tm * k * 2 + k * tn * 4 + 2 * tm * tn * 4) + k * tn * 2 + tm * tn * 4 + (6 << 20)
    return pl.pallas_call(
        kern,
        grid=(n // tn, m // tm),
        in_specs=in_specs,
        out_specs=pl.BlockSpec((tm, tn), lambda j, i: (i, j)),
        out_shape=jax.ShapeDtypeStruct((m, n), out_dtype),
        scratch_shapes=scratch,
        compiler_params=_params(2, vmem),
        name=name,
    )(*args)


def ws_swiglu(x, w_up, layer, d_ff, *, tm, tn, name):
    m, k = x.shape
    nb = d_ff // tn
    vmem = 2 * (tm * k * 2 + 2 * k * tn * 4 + tm * tn * 2) + 2 * k * tn * 2 + 3 * tm * tn * 4 + (6 << 20)
    return pl.pallas_call(
        _ws_swiglu_kernel,
        grid=(nb, m // tm),
        in_specs=[pl.BlockSpec((tm, k), lambda j, i: (i, 0)),
                  pl.BlockSpec((None, k, tn), lambda j, i: (layer, 0, j)),
                  pl.BlockSpec((None, k, tn), lambda j, i: (layer, 0, j + nb))],
        out_specs=pl.BlockSpec((tm, tn), lambda j, i: (i, j)),
        out_shape=jax.ShapeDtypeStruct((m, d_ff), BF16),
        scratch_shapes=[pltpu.VMEM((k, tn), BF16), pltpu.VMEM((k, tn), BF16)],
        compiler_params=_params(2, vmem),
        name=name,
    )(x, w_up, w_up)


NA_ROWS = SEQ // GRID_W
NA_NBLK = SEQ // NA_QBLK
NA_PAIRS = NA_KWIN // NA_QBLK
NA_PATTERN_BLOCKS = (0, 1, 2, NA_NBLK - 2, NA_NBLK - 1)


def _na_key_start(blk):
    return min(max(blk - 2, 0), NA_NBLK - NA_PAIRS)


def _na_pattern(i):
    return jnp.where(i < 2, i, jnp.where(i <= NA_NBLK - 3, 2, i - (NA_NBLK - 5)))


def _na_bias_table(rpb):
    hh = rpb.shape[0]
    span = GRID_W - 1
    left = span - (NA_WIN_C - 1)
    right = 2 * span + 1 - left - (2 * NA_WIN_C - 1)
    ext = jnp.pad(rpb, ((0, 0), (0, 0), (left, right)), constant_values=NEG)
    toep = jnp.stack([ext[:, :, span - qc:span - qc + GRID_W] for qc in range(GRID_W)], axis=2)
    qc = np.arange(GRID_W)
    cs = np.clip(qc - NA_WIN_C // 2, 0, GRID_W - NA_WIN_C)
    kc = np.arange(GRID_W)
    col_ok = (kc[None, :] >= cs[:, None]) & (kc[None, :] < cs[:, None] + NA_WIN_C)
    toep = jnp.where(col_ok[None, None], toep, NEG)
    neg = jnp.full((hh, GRID_W, GRID_W), NEG, rpb.dtype)
    patterns = []
    for blk in NA_PATTERN_BLOCKS:
        q_rows = []
        for a in range(NA_QBLK // GRID_W):
            qr = 2 * blk + a
            rs = min(max(qr - NA_WIN_R // 2, 0), NA_ROWS - NA_WIN_R)
            k_blocks = []
            for c in range(NA_KWIN // GRID_W):
                kr = 2 * _na_key_start(blk) + c
                ok = rs <= kr < rs + NA_WIN_R
                k_blocks.append(toep[:, kr - qr + NA_WIN_R - 1] if ok else neg)
            q_rows.append(jnp.concatenate(k_blocks, axis=-1))
        patterns.append(jnp.concatenate(q_rows, axis=-2))
    return jnp.stack(patterns, axis=0)


def _na_kernel(q_ref, k_ref, v_ref, bias_ref, gq_ref, gk_ref, o_ref, kn_ref, vb_ref):
    i = pl.program_id(1)

    @pl.when(i == 0)
    def _():
        def body(c, carry):
            r = pl.multiple_of(c * 256, 256)
            for h in range(NA_HEADS):
                hs = slice(h * NA_HD, (h + 1) * NA_HD)
                kn_ref[pl.ds(r, 256), hs] = _rms(k_ref[pl.ds(r, 256), hs], gk_ref[...]).astype(BF16)
            vb_ref[pl.ds(r, 256), :] = v_ref[pl.ds(r, 256), :].astype(BF16)
            return carry
        lax.fori_loop(0, SEQ // 256, body, 0)

    start = pl.multiple_of(jnp.clip(i - 2, 0, NA_NBLK - NA_PAIRS) * NA_QBLK, NA_QBLK)
    scale = NA_HD ** -0.5
    for h in range(NA_HEADS):
        hs = slice(h * NA_HD, (h + 1) * NA_HD)
        q = (_rms(q_ref[:, hs], gq_ref[...]) * scale).astype(BF16)
        s = _dot_nt(q, kn_ref[pl.ds(start, NA_KWIN), hs]) + bias_ref[0, h]
        m = jnp.max(s, axis=-1, keepdims=True)
        p = jnp.exp(s - m)
        l = jnp.sum(p, axis=-1, keepdims=True)
        o = _dot(p.astype(BF16), vb_ref[pl.ds(start, NA_KWIN), hs]) / l
        o_ref[:, hs] = o.astype(o_ref.dtype)


def na_attention(z, bias, gq, gk, batch):
    nblk = SEQ // NA_QBLK
    w = NA_HEADS * NA_HD
    vmem = (2 * (NA_QBLK * w * 4 + 2 * SEQ * w * 4 + NA_HEADS * NA_QBLK * NA_KWIN * 4 + NA_QBLK * w * 2)
            + 2 * SEQ * w * 2 + (8 << 20))
    return pl.pallas_call(
        _na_kernel,
        grid=(batch, nblk),
        in_specs=[pl.BlockSpec((NA_QBLK, w), lambda b, i: (b * nblk + i, COL_NA_Q // w)),
                  pl.BlockSpec((SEQ, w), lambda b, i: (b, COL_NA_K // w)),
                  pl.BlockSpec((SEQ, w), lambda b, i: (b, COL_NA_V // w)),
                  pl.BlockSpec((1, NA_HEADS, NA_QBLK, NA_KWIN), lambda b, i: (_na_pattern(i), 0, 0, 0)),
                  pl.BlockSpec((1, NA_HD), lambda b, i: (0, 0)),
                  pl.BlockSpec((1, NA_HD), lambda b, i: (0, 0))],
        out_specs=pl.BlockSpec((NA_QBLK, w), lambda b, i: (b * nblk + i, 0)),
        out_shape=jax.ShapeDtypeStruct((batch * SEQ, w), BF16),
        scratch_shapes=[pltpu.VMEM((SEQ, w), BF16), pltpu.VMEM((SEQ, w), BF16)],
        compiler_params=_params(2, vmem),
        name="na_attention",
    )(z, z, z, bias, gq.reshape(1, NA_HD), gk.reshape(1, NA_HD))


def _rope_tables(width, head_dim, first_lane):
    half = head_dim // 2
    freqs = ROPE_THETA ** (-2.0 * np.arange(half, dtype=np.float32) / head_dim)
    ang = jnp.arange(SEQ, dtype=F32)[:, None] * jnp.asarray(freqs, F32)[None, :]
    cos, sin = jnp.cos(ang), jnp.sin(ang)
    lane = np.arange(width)
    rel = lane - first_lane
    in_rope = (rel >= 0) & (rel < (width - first_lane if first_lane == 0 else head_dim))
    p = np.where(in_rope, rel % head_dim, 0)
    j = p % half
    first_half = in_rope & (p < half)
    second_half = in_rope & (p >= half)
    cos_t = jnp.where(in_rope[None, :], cos[:, j], 1.0)
    sin_a = jnp.where(first_half[None, :], -sin[:, j], 0.0)
    sin_b = jnp.where(second_half[None, :], sin[:, j], 0.0)
    return cos_t, sin_a, sin_b


def _apply_rope(x, cos_t, sin_a, sin_b, half):
    n = x.shape[-1]
    return (x * cos_t + pltpu.roll(x, n - half, axis=1) * sin_a
            + pltpu.roll(x, half, axis=1) * sin_b)


def _mla_prep_kernel(cq_ref, ckv_ref, kpe_ref, wq_ref, wkv_ref, gcq_ref, gckv_ref,
                     gqn_ref, gkn_ref, cos_ref, sa_ref, sb_ref, q_ref, k_ref, v_ref):
    half = MLA_ROPE // 2
    cos_t, sin_a, sin_b = cos_ref[...], sa_ref[...], sb_ref[...]
    scale = MLA_QK ** -0.5
    cq = _rms(cq_ref[...], gcq_ref[...]).astype(BF16)
    q_raw = _dot(cq, wq_ref[...])
    ckv = _rms(ckv_ref[...], gckv_ref[...]).astype(BF16)
    kv_raw = _dot(ckv, wkv_ref[...])
    kpe = kpe_ref[...]
    kpe_ss = jnp.sum(kpe * kpe, axis=-1, keepdims=True)
    for h in range(MLA_HEADS):
        cs = slice(h * MLA_PAD, (h + 1) * MLA_PAD)
        qc = q_raw[:, cs]
        ms = jnp.sum(qc * qc, axis=-1, keepdims=True) * (1.0 / MLA_QK)
        qn = qc * lax.rsqrt(ms + EPS) * gqn_ref[...]
        q_ref[:, cs] = (_apply_rope(qn, cos_t, sin_a, sin_b, half) * scale).astype(BF16)
        kn = kv_raw[:, h * MLA_PAD:h * MLA_PAD + MLA_NOPE]
        ms = (jnp.sum(kn * kn, axis=-1, keepdims=True) + kpe_ss) * (1.0 / MLA_QK)
        kc = jnp.concatenate([kn, kpe], axis=-1) * lax.rsqrt(ms + EPS) * gkn_ref[...]
        k_ref[:, cs] = _apply_rope(kc, cos_t, sin_a, sin_b, half).astype(BF16)
        v_ref[:, h * MLA_NOPE:(h + 1) * MLA_NOPE] = kv_raw[:, h * MLA_PAD + MLA_NOPE:(h + 1) * MLA_PAD].astype(BF16)


def mla_prep(z, wq_pad, wkv, gcq, gckv, gqn_pad, gkn_pad, tabs, tm=512):
    t = z.shape[0]
    sb = SEQ // tm
    qw = MLA_HEADS * MLA_PAD
    row = lambda i: (i, 0)
    const = lambda i: (0, 0)
    pos = lambda i: (i % sb, 0)
    return pl.pallas_call(
        _mla_prep_kernel,
        grid=(t // tm,),
        in_specs=[pl.BlockSpec((tm, MLA_Q_RANK), lambda i: (i, COL_CQ // MLA_Q_RANK)),
                  pl.BlockSpec((tm, MLA_KV_RANK), lambda i: (i, COL_CKV // MLA_KV_RANK)),
                  pl.BlockSpec((tm, LANES), lambda i: (i, COL_KPE // LANES)),
                  pl.BlockSpec((MLA_Q_RANK, qw), const),
                  pl.BlockSpec((MLA_KV_RANK, qw), const),
                  pl.BlockSpec((1, MLA_Q_RANK), const),
                  pl.BlockSpec((1, MLA_KV_RANK), const),
                  pl.BlockSpec((1, MLA_PAD), const),
                  pl.BlockSpec((1, MLA_PAD), const),
                  pl.BlockSpec((tm, MLA_PAD), pos),
                  pl.BlockSpec((tm, MLA_PAD), pos),
                  pl.BlockSpec((tm, MLA_PAD), pos)],
        out_specs=[pl.BlockSpec((tm, qw), row), pl.BlockSpec((tm, qw), row),
                   pl.BlockSpec((tm, MLA_HEADS * MLA_NOPE), row)],
        out_shape=[jax.ShapeDtypeStruct((t, qw), BF16), jax.ShapeDtypeStruct((t, qw), BF16),
                   jax.ShapeDtypeStruct((t, MLA_HEADS * MLA_NOPE), BF16)],
        compiler_params=_params(1, 40 << 20),
        name="mla_prep",
    )(z, z, z, wq_pad, wkv, gcq.reshape(1, -1), gckv.reshape(1, -1), gqn_pad, gkn_pad, *tabs)


def _mla_attn_kernel(q_ref, k_ref, v_ref, o_ref):
    for h in range(MLA_HEADS):
        cs = slice(h * MLA_PAD, (h + 1) * MLA_PAD)
        vs = slice(h * MLA_NOPE, (h + 1) * MLA_NOPE)
        s = _dot_nt(q_ref[:, cs], k_ref[:, cs])
        m = jnp.max(s, axis=-1, keepdims=True)
        p = jnp.exp(s - m)
        l = jnp.sum(p, axis=-1, keepdims=True)
        o = _dot(p.astype(BF16), v_ref[:, vs]) / l
        o_ref[:, vs] = o.astype(o_ref.dtype)


def mla_attention(q, k, v, batch, tq=256):
    nq = SEQ // tq
    qw = MLA_HEADS * MLA_PAD
    vw = MLA_HEADS * MLA_NOPE
    vmem = 2 * (tq * qw * 2 + SEQ * qw * 2 + SEQ * vw * 2 + tq * vw * 2) + 4 * tq * SEQ * 4 + (8 << 20)
    return pl.pallas_call(
        _mla_attn_kernel,
        grid=(batch, nq),
        in_specs=[pl.BlockSpec((tq, qw), lambda b, i: (b * nq + i, 0)),
                  pl.BlockSpec((SEQ, qw), lambda b, i: (b, 0)),
                  pl.BlockSpec((SEQ, vw), lambda b, i: (b, 0))],
        out_specs=pl.BlockSpec((tq, vw), lambda b, i: (b * nq + i, 0)),
        out_shape=jax.ShapeDtypeStruct((batch * SEQ, vw), BF16),
        compiler_params=_params(2, vmem),
        name="mla_attention",
    )(q, k, v)


def _rms_halves(x, g):
    lo = lax.broadcasted_iota(jnp.int32, x.shape, 1) < SWA_HD
    x2 = x * x
    s_lo = jnp.sum(jnp.where(lo, x2, 0.0), axis=-1, keepdims=True)
    s_hi = jnp.sum(jnp.where(lo, 0.0, x2), axis=-1, keepdims=True)
    ms = jnp.where(lo, s_lo, s_hi) * (1.0 / SWA_HD)
    return x * lax.rsqrt(ms + EPS) * g


def _swa_kernel(sink_ref, q_ref, k_ref, v_ref, gq_ref, gk_ref, cosq_ref, saq_ref, sbq_ref,
                cosk_ref, sak_ref, sbk_ref, o_ref, kk_ref, vv_ref):
    n = pl.program_id(1)
    nblk = SEQ // SWA_BLOCK
    half = SWA_HD // 2

    @pl.when(n == 0)
    def _():
        zeros = jnp.zeros((SWA_BLOCK, LANES), BF16)
        for c in range(4):
            kk_ref[c, pl.ds(0, SWA_BLOCK), :] = zeros
            kk_ref[c, pl.ds(SEQ + SWA_BLOCK, SWA_BLOCK), :] = zeros
            vv_ref[c, pl.ds(0, SWA_BLOCK), :] = zeros
            vv_ref[c, pl.ds(SEQ + SWA_BLOCK, SWA_BLOCK), :] = zeros

        def body(c, carry):
            r = pl.multiple_of(c * 256, 256)
            dst = pl.ds(r + SWA_BLOCK, 256)
            lo = lax.broadcasted_iota(jnp.int32, (256, LANES), 1) < SWA_HD
            kr = _apply_rope(_rms_halves(k_ref[pl.ds(r, 256), :], gk_ref[...]),
                             cosk_ref[pl.ds(r, 256), :], sak_ref[pl.ds(r, 256), :],
                             sbk_ref[pl.ds(r, 256), :], half)
            ks = pltpu.roll(kr, SWA_HD, axis=1)
            kk_ref[0, dst, :] = jnp.where(lo, kr, 0.0).astype(BF16)
            kk_ref[1, dst, :] = jnp.where(lo, 0.0, ks).astype(BF16)
            kk_ref[2, dst, :] = jnp.where(lo, ks, 0.0).astype(BF16)
            kk_ref[3, dst, :] = jnp.where(lo, 0.0, kr).astype(BF16)
            vr = v_ref[pl.ds(r, 256), :]
            vs = pltpu.roll(vr, SWA_HD, axis=1)
            vv_ref[0, dst, :] = jnp.where(lo, vr, 0.0).astype(BF16)
            vv_ref[1, dst, :] = jnp.where(lo, 0.0, vs).astype(BF16)
            vv_ref[2, dst, :] = jnp.where(lo, vs, 0.0).astype(BF16)
            vv_ref[3, dst, :] = jnp.where(lo, 0.0, vr).astype(BF16)
            return carry
        lax.fori_loop(0, SEQ // 256, body, 0)

    band = pl.ds(pl.multiple_of(n * SWA_BLOCK, SWA_BLOCK), 3 * SWA_BLOCK)
    a = lax.broadcasted_iota(jnp.int32, (SWA_BLOCK, 3 * SWA_BLOCK), 0)
    c = lax.broadcasted_iota(jnp.int32, (SWA_BLOCK, 3 * SWA_BLOCK), 1)
    c_min = jnp.where(n == 0, SWA_BLOCK, 0)
    c_max = jnp.where(n == nblk - 1, 2 * SWA_BLOCK, 3 * SWA_BLOCK)
    valid = (c >= a) & (c <= a + 2 * SWA_BLOCK) & (c >= c_min) & (c < c_max)
    scale = SWA_HD ** -0.5
    for pair in range(SWA_HEADS // 2):
        ps = slice(pair * LANES, (pair + 1) * LANES)
        grp = pair // 2
        q = _apply_rope(_rms_halves(q_ref[:, ps], gq_ref[...]),
                        cosq_ref[...], saq_ref[...], sbq_ref[...], half)
        q = (q * scale).astype(BF16)
        acc = None
        for hf in range(2):
            sink = sink_ref[2 * pair + hf]
            s = _dot_nt(q, kk_ref[2 * grp + hf, band, :])
            s = jnp.where(valid, s, NEG)
            m = jnp.maximum(jnp.max(s, axis=-1, keepdims=True), sink)
            p = jnp.exp(s - m)
            den = jnp.sum(p, axis=-1, keepdims=True) + jnp.exp(sink - m)
            o = _dot(p.astype(BF16), vv_ref[2 * grp + hf, band, :]) / den
            acc = o if acc is None else acc + o
        o_ref[:, ps] = acc.astype(o_ref.dtype)


def swa_attention(z, sink, gq, gk, tabs, batch):
    nblk = SEQ // SWA_BLOCK
    qw = SWA_HEADS * SWA_HD
    cos_t, sin_a, sin_b = tabs
    g2 = lambda g: jnp.concatenate([g, g]).reshape(1, LANES)
    blk = lambda b, n: (n, 0)
    full = lambda b, n: (0, 0)
    tab_blk = pl.BlockSpec((SWA_BLOCK, LANES), blk)
    tab_full = pl.BlockSpec((SEQ, LANES), full)
    vmem = (2 * (SWA_BLOCK * qw * 6 + 2 * SEQ * LANES * 4 + 3 * SEQ * LANES * 4 + 3 * SWA_BLOCK * LANES * 4)
            + 8 * (SEQ + 2 * SWA_BLOCK) * LANES * 2 + (8 << 20))
    return pl.pallas_call(
        _swa_kernel,
        grid=(batch, nblk),
        in_specs=[pl.BlockSpec(memory_space=pltpu.SMEM),
                  pl.BlockSpec((SWA_BLOCK, qw), lambda b, n: (b * nblk + n, COL_SWQ // qw)),
                  pl.BlockSpec((SEQ, LANES), lambda b, n: (b, COL_SWK // LANES)),
                  pl.BlockSpec((SEQ, LANES), lambda b, n: (b, COL_SWV // LANES)),
                  pl.BlockSpec((1, LANES), full), pl.BlockSpec((1, LANES), full),
                  tab_blk, tab_blk, tab_blk, tab_full, tab_full, tab_full],
        out_specs=pl.BlockSpec((SWA_BLOCK, qw), lambda b, n: (b * nblk + n, 0)),
        out_shape=jax.ShapeDtypeStruct((batch * SEQ, qw), BF16),
        scratch_shapes=[pltpu.VMEM((4, SEQ + 2 * SWA_BLOCK, LANES), BF16),
                        pltpu.VMEM((4, SEQ + 2 * SWA_BLOCK, LANES), BF16)],
        compiler_params=_params(2, vmem),
        name="swa_attention",
    )(sink, z, z, z, g2(gq), g2(gk), cos_t, sin_a, sin_b, cos_t, sin_a, sin_b)


def _mem_attn_kernel(q_ref, k_ref, v_ref, gq_ref, gk_ref, o_ref):
    scale = MEM_HD ** -0.5
    for h in range(MEM_HEADS):
        hs = slice(h * MEM_HD, (h + 1) * MEM_HD)
        q = (_rms(q_ref[:, hs], gq_ref[...]) * scale).astype(BF16)
        k = _rms(k_ref[:, hs], gk_ref[...]).astype(BF16)
        s = _dot_nt(q, k)
        m = jnp.max(s, axis=-1, keepdims=True)
        p = jnp.exp(s - m)
        l = jnp.sum(p, axis=-1, keepdims=True)
        o = _dot(p.astype(BF16), v_ref[:, hs].astype(BF16)) / l
        o_ref[:, hs] = o.astype(o_ref.dtype)


def mem_attention(z, memkv, gq, gk, batch, tq=512):
    nq = SEQ // tq
    w = MEM_HEADS * MEM_HD
    return pl.pallas_call(
        _mem_attn_kernel,
        grid=(batch, nq),
        in_specs=[pl.BlockSpec((tq, w), lambda b, i: (b * nq + i, COL_MEQ // w)),
                  pl.BlockSpec((MEM_LEN, w), lambda b, i: (b, 0)),
                  pl.BlockSpec((MEM_LEN, w), lambda b, i: (b, 1)),
                  pl.BlockSpec((1, MEM_HD), lambda b, i: (0, 0)),
                  pl.BlockSpec((1, MEM_HD), lambda b, i: (0, 0))],
        out_specs=pl.BlockSpec((tq, w), lambda b, i: (b * nq + i, 0)),
        out_shape=jax.ShapeDtypeStruct((batch * SEQ, w), BF16),
        compiler_params=_params(2, 24 << 20),
        name="mem_attention",
    )(z, memkv, memkv, gq.reshape(1, MEM_HD), gk.reshape(1, MEM_HD))


def _merge_kernel(layer, u_ref, b0_ref, b1_ref, b2_ref, b3_ref, wg_hbm, wb_ref, bg_ref, o_ref,
                  stage_ref, wgs_ref, wbs_ref, sem):
    branches = (b0_ref, b1_ref, b2_ref, b3_ref)
    j, i = pl.program_id(0), pl.program_id(1)
    tn = wgs_ref.shape[2]
    slot = j % 2

    def copies(jj, s):
        return [pltpu.make_async_copy(wg_hbm.at[layer, :, n, pl.ds(pl.multiple_of(jj * tn, tn), tn)],
                                      stage_ref.at[s, n], sem.at[s]) for n in range(4)]

    @pl.when(jnp.logical_and(j == 0, i == 0))
    def _():
        for c in copies(0, 0):
            c.start()

    @pl.when(i == 0)
    def _():
        for c in copies(j, slot):
            c.wait()

        @pl.when(j + 1 < pl.num_programs(0))
        def _():
            for c in copies(j + 1, 1 - slot):
                c.start()
        for n in range(4):
            _cast_rows(stage_ref.at[slot, n], wgs_ref.at[n], D_MODEL)
            wbs_ref[n] = wb_ref[n].astype(BF16)

    u = u_ref[...]
    acc = None
    for n in range(4):
        gate = _sigmoid(_dot(u, wgs_ref[n]) + bg_ref[n:n + 1, :])
        term = gate * _dot(branches[n][...], wbs_ref[n])
        acc = term if acc is None else acc + term
    o_ref[...] = acc.astype(o_ref.dtype)


def gated_merge(u, branches, w_gate, w_branch, b_gate, layer, tm=1024, tn=256):
    t = u.shape[0]
    bw = branches[0].shape[1]
    vmem = (2 * (tm * D_MODEL * 2 + 4 * tm * bw * 2 + 4 * D_MODEL * tn * 4 + 4 * bw * tn * 4 + tm * tn * 2)
            + 4 * D_MODEL * tn * 2 + 4 * bw * tn * 2 + 4 * tm * tn * 4 + (6 << 20))
    return pl.pallas_call(
        functools.partial(_merge_kernel, layer),
        grid=(D_MODEL // tn, t // tm),
        in_specs=[pl.BlockSpec((tm, D_MODEL), lambda j, i: (i, 0))]
        + [pl.BlockSpec((tm, bw), lambda j, i: (i, 0)) for _ in range(4)]
        + [pl.BlockSpec(memory_space=pl.ANY),
           pl.BlockSpec((None, 4, bw, tn), lambda j, i: (layer, 0, 0, j)),
           pl.BlockSpec((None, 4, tn), lambda j, i: (layer, 0, j))],
        out_specs=pl.BlockSpec((tm, tn), lambda j, i: (i, j)),
        out_shape=jax.ShapeDtypeStruct((t, D_MODEL), BF16),
        scratch_shapes=[pltpu.VMEM((2, 4, D_MODEL, tn), F32), pltpu.VMEM((4, D_MODEL, tn), BF16),
                        pltpu.VMEM((4, bw, tn), BF16), pltpu.SemaphoreType.DMA((2,))],
        compiler_params=_params(2, vmem),
        name="gated_merge",
    )(u, *branches, w_gate, w_branch, b_gate)


def _router_kernel(h_ref, g_ref, wr_ref, ri_ref, rw_ref, cnt_ref, carry_ref):
    @pl.when(pl.program_id(0) == 0)
    def _():
        carry_ref[...] = jnp.zeros_like(carry_ref)

    tm = h_ref.shape[0]
    hn = _rms(h_ref[...], g_ref[...])
    logits = jnp.dot(hn, wr_ref[...], precision=lax.Precision.HIGHEST, preferred_element_type=F32)
    lane = lax.broadcasted_iota(jnp.int32, (tm, LANES), 1)
    lane_f = lane.astype(F32)
    logits = jnp.where(lane < N_EXPERTS, logits, -jnp.inf)
    m1 = jnp.max(logits, axis=-1, keepdims=True)
    i1 = jnp.min(jnp.where(logits == m1, lane_f, float(LANES)), axis=-1, keepdims=True)
    oh1 = lane_f == i1
    rest = jnp.where(oh1, -jnp.inf, logits)
    m2 = jnp.max(rest, axis=-1, keepdims=True)
    i2 = jnp.min(jnp.where(rest == m2, lane_f, float(LANES)), axis=-1, keepdims=True)
    oh2 = lane_f == i2
    e2 = jnp.exp(m2 - m1)
    w1 = 1.0 / (1.0 + e2)
    w2 = e2 / (1.0 + e2)
    chosen = jnp.where(oh1 | oh2, 1.0, 0.0)
    before = (lax.broadcasted_iota(jnp.int32, (tm, tm), 1)
              < lax.broadcasted_iota(jnp.int32, (tm, tm), 0))
    prefix = _dot(jnp.where(before, 1.0, 0.0).astype(BF16), chosen.astype(BF16)) + carry_ref[...]
    r1 = jnp.sum(jnp.where(oh1, prefix, 0.0), axis=-1, keepdims=True).astype(jnp.int32)
    r2 = jnp.sum(jnp.where(oh2, prefix, 0.0), axis=-1, keepdims=True).astype(jnp.int32)
    carry_ref[...] += jnp.sum(chosen, axis=0, keepdims=True)
    ri_ref[...] = jnp.where(lane == 0, i1.astype(jnp.int32),
                            jnp.where(lane == 1, i2.astype(jnp.int32),
                                      jnp.where(lane == 2, r1, jnp.where(lane == 3, r2, 0))))
    rw_ref[...] = jnp.where(lane == 0, w1, jnp.where(lane == 1, w2, 0.0))
    cnt_ref[...] = carry_ref[...]


def moe_router(h, g, w_router, tm=512):
    t, d = h.shape
    wr = jnp.pad(w_router, ((0, 0), (0, LANES - N_EXPERTS)))
    return pl.pallas_call(
        _router_kernel,
        grid=(t // tm,),
        in_specs=[pl.BlockSpec((tm, d), lambda i: (i, 0)),
                  pl.BlockSpec((1, d), lambda i: (0, 0)),
                  pl.BlockSpec((d, LANES), lambda i: (0, 0))],
        out_specs=[pl.BlockSpec((tm, LANES), lambda i: (i, 0)),
                   pl.BlockSpec((tm, LANES), lambda i: (i, 0)),
                   pl.BlockSpec((1, LANES), lambda i: (0, 0))],
        out_shape=[jax.ShapeDtypeStruct((t, LANES), jnp.int32),
                   jax.ShapeDtypeStruct((t, LANES), F32),
                   jax.ShapeDtypeStruct((1, LANES), F32)],
        scratch_shapes=[pltpu.VMEM((1, LANES), F32)],
        compiler_params=_params(1, 32 << 20),
        name="moe_router",
    )(h, g.reshape(1, d), wr)


ROW_DMA_UNROLL = 8


def _row_copy(src_hbm, row, dst_vmem, r, sem):
    return pltpu.make_async_copy(src_hbm.at[pl.ds(row, 1), :], dst_vmem.at[pl.ds(r, 1), :], sem)


def _moe_gather_kernel(src_ref, h_hbm, g_ref, o_ref, buf_ref, sem):
    tg = buf_ref.shape[1]
    i = pl.program_id(0)
    slot = i % 2

    def start_rows(step, dst_slot):
        def issue(grp, carry):
            for k in range(ROW_DMA_UNROLL):
                r = grp * ROW_DMA_UNROLL + k
                _row_copy(h_hbm, src_ref[step * tg + r], buf_ref.at[dst_slot], r,
                          sem.at[dst_slot]).start(priority=k % 2)
            return carry
        lax.fori_loop(0, tg // ROW_DMA_UNROLL, issue, 0)

    @pl.when(i == 0)
    def _():
        start_rows(0, 0)

    @pl.when(i + 1 < pl.num_programs(0))
    def _():
        start_rows(i + 1, 1 - slot)

    def drain(r, carry):
        _row_copy(h_hbm, 0, buf_ref.at[slot], r, sem.at[slot]).wait()
        return carry
    lax.fori_loop(0, tg, drain, 0, unroll=8)
    o_ref[...] = _rms(buf_ref[slot], g_ref[...]).astype(o_ref.dtype)


def moe_gather(src_tok, h, g, rows, tg=256):
    d = h.shape[1]
    return pl.pallas_call(
        _moe_gather_kernel,
        grid_spec=pltpu.PrefetchScalarGridSpec(
            num_scalar_prefetch=1,
            grid=(rows // tg,),
            in_specs=[pl.BlockSpec(memory_space=pl.ANY),
                      pl.BlockSpec((1, d), lambda i, s: (0, 0))],
            out_specs=pl.BlockSpec((tg, d), lambda i, s: (i, 0)),
            scratch_shapes=[pltpu.VMEM((2, tg, d), F32), pltpu.SemaphoreType.DMA((2,))]),
        out_shape=jax.ShapeDtypeStruct((rows, d), BF16),
        compiler_params=_params(1, 24 << 20),
        name="moe_gather",
    )(src_tok, h, g.reshape(1, d))


def _stream_expert_weights(te_ref, first_ref, nxt_ref, meta_ref, cnt_ref, make_copies, cast_slot):
    j, i = pl.program_id(0), pl.program_id(1)

    @pl.when(jnp.logical_and(j == 0, i == 0))
    def _():
        cnt_ref[0] = 0
        for c in make_copies(0, te_ref[0], 0):
            c.start()

    @pl.when(first_ref[i] == 1)
    def _():
        slot = cnt_ref[0] % 2
        for c in make_copies(j, te_ref[i], slot):
            c.wait()
        wraps = nxt_ref[i] < 0
        j_next = jnp.where(wraps, j + 1, j)
        e_next = jnp.where(wraps, meta_ref[1], nxt_ref[i])

        @pl.when(j_next < pl.num_programs(0))
        def _():
            for c in make_copies(j_next, e_next, 1 - slot):
                c.start()
        cast_slot(slot)
        cnt_ref[0] = cnt_ref[0] + 1


def _moe_up_kernel(te_ref, first_ref, nxt_ref, meta_ref, x_ref, w_hbm, o_ref,
                   stage_ref, wgb_ref, wub_ref, sem, cnt_ref):
    i = pl.program_id(1)
    active = i < meta_ref[0]
    tf = wgb_ref.shape[1]
    nf = pl.num_programs(0)

    def make_copies(jj, e, slot):
        return [pltpu.make_async_copy(w_hbm.at[e, :, pl.ds(pl.multiple_of((jj + half * nf) * tf, tf), tf)],
                                      stage_ref.at[slot, half], sem.at[slot]) for half in range(2)]

    def cast_slot(slot):
        _cast_rows(stage_ref.at[slot, 0], wgb_ref, D_MODEL)
        _cast_rows(stage_ref.at[slot, 1], wub_ref, D_MODEL)

    _stream_expert_weights(te_ref, first_ref, nxt_ref, meta_ref, cnt_ref, make_copies, cast_slot)

    @pl.when(active)
    def _():
        x = x_ref[...]
        g = _dot(x, wgb_ref[...])
        u = _dot(x, wub_ref[...])
        o_ref[...] = (g * _sigmoid(g) * u).astype(o_ref.dtype)

    @pl.when(jnp.logical_not(active))
    def _():
        o_ref[...] = jnp.zeros_like(o_ref)


def _active_row(j, i, te, first, nxt, meta):
    return jnp.maximum(jnp.minimum(i, meta[0] - 1), 0)


def moe_up(xs, w_up, sched, tf=512):
    rows, d = xs.shape
    nf = D_FF_EXPERT // tf
    vmem = (2 * (MOE_TM * d * 2 + MOE_TM * tf * 2) + 4 * d * tf * 4 + 2 * d * tf * 2
            + 3 * MOE_TM * tf * 4 + (6 << 20))
    return pl.pallas_call(
        _moe_up_kernel,
        grid_spec=pltpu.PrefetchScalarGridSpec(
            num_scalar_prefetch=4,
            grid=(nf, rows // MOE_TM),
            in_specs=[pl.BlockSpec((MOE_TM, d), lambda j, i, *s: (_active_row(j, i, *s), 0)),
                      pl.BlockSpec(memory_space=pl.ANY)],
            out_specs=pl.BlockSpec((MOE_TM, tf), lambda j, i, *s: (i, j)),
            scratch_shapes=[pltpu.VMEM((2, 2, d, tf), F32), pltpu.VMEM((d, tf), BF16),
                            pltpu.VMEM((d, tf), BF16), pltpu.SemaphoreType.DMA((2,)),
                            pltpu.SMEM((1,), jnp.int32)]),
        out_shape=jax.ShapeDtypeStruct((rows, D_FF_EXPERT), BF16),
        compiler_params=_params(2, vmem),
        name="moe_up",
    )(*sched, xs, w_up)


def _moe_down_kernel(k_half, te_ref, first_ref, nxt_ref, meta_ref, a_ref, w_hbm, *rest):
    if len(rest) == 6:
        part_ref, o_ref, stage_ref, wb_ref, sem, cnt_ref = rest
    else:
        part_ref, (o_ref, stage_ref, wb_ref, sem, cnt_ref) = None, rest
    i = pl.program_id(1)
    active = i < meta_ref[0]
    fk, tn = wb_ref.shape

    def make_copies(jj, e, slot):
        return [pltpu.make_async_copy(
            w_hbm.at[e, pl.ds(k_half * fk, fk), pl.ds(pl.multiple_of(jj * tn, tn), tn)],
            stage_ref.at[slot], sem.at[slot])]

    def cast_slot(slot):
        _cast_rows(stage_ref.at[slot], wb_ref, fk)

    _stream_expert_weights(te_ref, first_ref, nxt_ref, meta_ref, cnt_ref, make_copies, cast_slot)

    @pl.when(active)
    def _():
        y = _dot(a_ref[...], wb_ref[...])
        o_ref[...] = y if part_ref is None else part_ref[...] + y

    @pl.when(jnp.logical_not(active))
    def _():
        o_ref[...] = jnp.zeros_like(o_ref)


def moe_down(act, w_down, sched, k_half, partial=None, tn=512):
    rows, f = act.shape
    fk = f // 2
    in_specs = [pl.BlockSpec((MOE_TM, fk), lambda j, i, *s: (_active_row(j, i, *s), k_half)),
                pl.BlockSpec(memory_space=pl.ANY)]
    args = [*sched, act, w_down]
    if partial is not None:
        in_specs.append(pl.BlockSpec((MOE_TM, tn), lambda j, i, *s: (i, j)))
        args.append(partial)
    vmem = 2 * (MOE_TM * fk * 2 + fk * tn * 4 + 2 * MOE_TM * tn * 4) + fk * tn * 2 + MOE_TM * tn * 4 + (6 << 20)
    return pl.pallas_call(
        functools.partial(_moe_down_kernel, k_half),
        grid_spec=pltpu.PrefetchScalarGridSpec(
            num_scalar_prefetch=4,
            grid=(D_MODEL // tn, rows // MOE_TM),
            in_specs=in_specs,
            out_specs=pl.BlockSpec((MOE_TM, tn), lambda j, i, *s: (i, j)),
            scratch_shapes=[pltpu.VMEM((2, fk, tn), F32), pltpu.VMEM((fk, tn), BF16),
                            pltpu.SemaphoreType.DMA((2,)), pltpu.SMEM((1,), jnp.int32)]),
        out_shape=jax.ShapeDtypeStruct((rows, D_MODEL), F32),
        compiler_params=_params(2, vmem),
        name="moe_down_%d" % k_half,
    )(*args)


def _moe_combine_kernel(pos_ref, h_ref, rw_ref, y_hbm, o_ref, b1_ref, b2_ref, sem):
    tc = b1_ref.shape[1]
    i = pl.program_id(0)
    slot = i % 2

    def start_rows(step, dst_slot):
        def issue(grp, carry):
            for k in range(ROW_DMA_UNROLL):
                r = grp * ROW_DMA_UNROLL + k
                t = step * tc + r
                _row_copy(y_hbm, pos_ref[2 * t], b1_ref.at[dst_slot], r, sem.at[dst_slot]).start(priority=0)
                _row_copy(y_hbm, pos_ref[2 * t + 1], b2_ref.at[dst_slot], r, sem.at[dst_slot]).start(priority=1)
            return carry
        lax.fori_loop(0, tc // ROW_DMA_UNROLL, issue, 0)

    @pl.when(i == 0)
    def _():
        start_rows(0, 0)

    @pl.when(i + 1 < pl.num_programs(0))
    def _():
        start_rows(i + 1, 1 - slot)

    def drain(r, carry):
        _row_copy(y_hbm, 0, b1_ref.at[slot], r, sem.at[slot]).wait()
        _row_copy(y_hbm, 0, b2_ref.at[slot], r, sem.at[slot]).wait()
        return carry
    lax.fori_loop(0, tc, drain, 0, unroll=8)
    o_ref[...] = h_ref[...] + rw_ref[:, 0:1] * b1_ref[slot] + rw_ref[:, 1:2] * b2_ref[slot]


def moe_combine(pos_flat, h, rw, y, tc=256):
    t, d = h.shape
    return pl.pallas_call(
        _moe_combine_kernel,
        grid_spec=pltpu.PrefetchScalarGridSpec(
            num_scalar_prefetch=1,
            grid=(t // tc,),
            in_specs=[pl.BlockSpec((tc, d), lambda i, p: (i, 0)),
                      pl.BlockSpec((tc, LANES), lambda i, p: (i, 0)),
                      pl.BlockSpec(memory_space=pl.ANY)],
            out_specs=pl.BlockSpec((tc, d), lambda i, p: (i, 0)),
            scratch_shapes=[pltpu.VMEM((2, tc, d), F32), pltpu.VMEM((2, tc, d), F32),
                            pltpu.SemaphoreType.DMA((2,))]),
        out_shape=jax.ShapeDtypeStruct((t, d), F32),
        compiler_params=_params(1, 32 << 20),
        name="moe_combine",
    )(pos_flat, h, rw, y)


def moe_block(h, g, w_router, w_up, w_down):
    t = h.shape[0]
    rows = 2 * t + N_EXPERTS * MOE_TM
    n_row_tiles = rows // MOE_TM
    ri, rw, cnt = moe_router(h, g, w_router)
    counts = cnt[0, :N_EXPERTS].astype(jnp.int32)
    padded = ((counts + MOE_TM - 1) // MOE_TM) * MOE_TM
    ends = jnp.cumsum(padded)
    starts = ends - padded
    pos = starts[ri[:, 0:2]] + ri[:, 2:4]
    tok = jnp.broadcast_to(jnp.arange(t, dtype=jnp.int32)[:, None], (t, 2))
    src_tok = jnp.zeros((rows,), jnp.int32).at[pos.reshape(-1)].set(tok.reshape(-1))
    n_tiles = (ends[-1] // MOE_TM).astype(jnp.int32)
    tile_first_row = jnp.arange(n_row_tiles, dtype=jnp.int32) * MOE_TM
    tile_e = jnp.sum((tile_first_row[:, None] >= ends[None, :]).astype(jnp.int32), axis=1)
    tile_e = jnp.minimum(tile_e, N_EXPERTS - 1)
    tile_idx = jnp.arange(n_row_tiles, dtype=jnp.int32)
    is_active = tile_idx < n_tiles
    tile_e = jnp.where(is_active, tile_e, tile_e[n_tiles - 1]).astype(jnp.int32)
    first = (is_active & ((tile_idx == 0) | (tile_e != jnp.roll(tile_e, 1)))).astype(jnp.int32)
    eidx = jnp.arange(N_EXPERTS, dtype=jnp.int32)
    later = (eidx[None, :] > eidx[:, None]) & (padded > 0)[None, :]
    nxt = jnp.min(jnp.where(later, eidx[None, :], N_EXPERTS), axis=1)
    nxt = jnp.where(nxt == N_EXPERTS, -1, nxt).astype(jnp.int32)
    sched = (tile_e, first, nxt[tile_e], jnp.stack([n_tiles, tile_e[0]]).astype(jnp.int32))

    xs = moe_gather(src_tok, h, g, rows)
    act = moe_up(xs, w_up, sched)
    y = moe_down(act, w_down, sched, 0)
    y = moe_down(act, w_down, sched, 1, partial=y)
    return moe_combine(pos.reshape(-1).astype(jnp.int32), h, rw, y)


def _pad_in_proj(w_in):
    segs = np.cumsum([0, 512, 512, 512, 768, 256, 64, 512, 128, 128, 512])
    na_q, na_k, na_v, c_q, c_kv, k_pe, sw_q, sw_k, sw_v, me_q = [
        w_in[:, :, segs[n]:segs[n + 1]] for n in range(10)]
    pad = jnp.zeros(w_in.shape[:2] + (IN_COLS_PAD - COL_KPE - MLA_ROPE,), w_in.dtype)
    return jnp.concatenate([na_q, na_k, na_v, c_q, c_kv, sw_q, me_q, sw_k, sw_v, k_pe, pad], axis=2)


def _pad_heads(x, n_heads, width):
    lead = x.shape[:-1]
    x = x.reshape(lead + (n_heads, width))
    x = jnp.pad(x, [(0, 0)] * len(lead) + [(0, 0), (0, MLA_PAD - width)])
    return x.reshape(lead + (n_heads * MLA_PAD,))


def kernel(x, mem, norm_mix, w_in, na_q_norm, na_k_norm, na_rpb, mla_cq_norm, mla_w_uq, mla_ckv_norm, mla_w_ukv, mla_q_norm, mla_k_norm, swa_q_norm, swa_k_norm, swa_sink, mem_norm, mem_w_kv, mem_q_norm, mem_k_norm, w_branch, w_gate, b_gate, w_o, norm_ffn, ffn_w_up, ffn_w_down, moe_router, moe_w_up, moe_w_down):
    batch, seq, d = x.shape
    assert (seq, d) == (SEQ, D_MODEL) and mem.shape[1] == MEM_LEN
    t = batch * seq
    depth = w_in.shape[0]
    h = x.reshape(t, d)
    mem2 = mem.reshape(batch * MEM_LEN, d)
    mla_tabs = _rope_tables(MLA_PAD, MLA_ROPE, MLA_NOPE)
    swa_tabs = _rope_tables(LANES, SWA_HD, 0)
    w_in_pad = _pad_in_proj(w_in).astype(BF16)
    wq_pad = _pad_heads(mla_w_uq, MLA_HEADS, MLA_QK).astype(BF16)
    wkv_bf = mla_w_ukv.astype(BF16)

    for l in range(depth):
        u = rmsnorm_bf16(h, norm_mix[l])
        z = ws_matmul(u, w_in_pad, l, tm=1024, tn=512, out_dtype=F32, name="in_proj")

        o_na = na_attention(z, _na_bias_table(na_rpb[l]), na_q_norm[l], na_k_norm[l], batch)

        q_mla, k_mla, v_mla = mla_prep(
            z, wq_pad[l], wkv_bf[l], mla_cq_norm[l], mla_ckv_norm[l],
            jnp.pad(mla_q_norm[l], (0, MLA_PAD - MLA_QK)).reshape(1, MLA_PAD),
            jnp.pad(mla_k_norm[l], (0, MLA_PAD - MLA_QK)).reshape(1, MLA_PAD), mla_tabs)
        o_mla = mla_attention(q_mla, k_mla, v_mla, batch)

        o_swa = swa_attention(z, swa_sink[l], swa_q_norm[l], swa_k_norm[l], swa_tabs, batch)

        memn = rmsnorm_bf16(mem2, mem_norm[l])
        memkv = ws_matmul(memn, mem_w_kv, l, tm=1024, tn=512, out_dtype=F32, name="mem_kv")
        o_mem = mem_attention(z, memkv, mem_q_norm[l], mem_k_norm[l], batch)

        merged = gated_merge(u, (o_na, o_mla, o_swa, o_mem), w_gate, w_branch, b_gate, l)
        h = ws_matmul(merged, w_o, l, tm=1024, tn=512, out_dtype=F32, residual=h, name="out_proj")

        if l % 2 == 0:
            hn = rmsnorm_bf16(h, norm_ffn[l])
            act = ws_swiglu(hn, ffn_w_up, l // 2, D_FF, tm=1024, tn=512, name="ffn_up")
            h = ws_matmul(act, ffn_w_down, l // 2, tm=512, tn=512, out_dtype=F32, residual=h, name="ffn_down")
        else:
            h = moe_block(h, norm_ffn[l], moe_router[l // 2], moe_w_up[l // 2], moe_w_down[l // 2])
    return h.reshape(batch, seq, d)
```

```python
import functools

import jax
import jax.numpy as jnp
import numpy as np
from jax import lax
from jax.experimental import pallas as pl
from jax.experimental.pallas import tpu as pltpu

F32 = jnp.float32
BF16 = jnp.bfloat16

D_MODEL = 2048
SEQ = 2048
MEM_LEN = 256
GRID_W = 64
ROPE_THETA = 10000.0
EPS = 1e-6
NEG = -1e30

NA_HEADS = 4
NA_HD = 128
NA_WIN_R = 8
NA_WIN_C = 16
NA_QBLK = 128
NA_KWIN = 640

MLA_HEADS = 4
MLA_NOPE = 128
MLA_ROPE = 64
MLA_QK = MLA_NOPE + MLA_ROPE
MLA_PAD = 256
MLA_Q_RANK = 768
MLA_KV_RANK = 256

SWA_HEADS = 8
SWA_KV_HEADS = 2
SWA_HD = 64
SWA_BLOCK = 128

MEM_HEADS = 4
MEM_HD = 128

D_FF = 5632
N_EXPERTS = 8
D_FF_EXPERT = 7168
MOE_TM = 512

LANES = 128
VMEM_CAP = 60000 * 1024

COL_NA_Q, COL_NA_K, COL_NA_V = 0, 512, 1024
COL_CQ, COL_CKV = 1536, 2304
COL_SWQ, COL_MEQ, COL_SWK, COL_SWV, COL_KPE = 2560, 3072, 3584, 3712, 3840
IN_COLS_PAD = 4096


def _params(n_axes, vmem_bytes):
    return pltpu.CompilerParams(
        dimension_semantics=("arbitrary",) * n_axes,
        vmem_limit_bytes=int(min(VMEM_CAP, vmem_bytes)))


def _rms(x, g):
    ms = jnp.mean(x * x, axis=-1, keepdims=True)
    return x * lax.rsqrt(ms + EPS) * g


def _sigmoid(x):
    return 1.0 / (1.0 + jnp.exp(-x))


def _cast_rows(src_ref, dst_ref, rows, chunk=256):
    def body(c, carry):
        r = pl.multiple_of(c * chunk, chunk)
        dst_ref[pl.ds(r, chunk), :] = src_ref[pl.ds(r, chunk), :].astype(BF16)
        return carry
    lax.fori_loop(0, rows // chunk, body, 0)


def _dot(a, b):
    return jnp.dot(a, b, preferred_element_type=F32)


def _dot_nt(a, b):
    return lax.dot_general(a, b, (((1,), (1,)), ((), ())), preferred_element_type=F32)


def _rmsnorm_kernel(x_ref, g_ref, o_ref):
    o_ref[...] = _rms(x_ref[...], g_ref[...]).astype(o_ref.dtype)


def rmsnorm_bf16(x, g, tm=512):
    m, d = x.shape
    return pl.pallas_call(
        _rmsnorm_kernel,
        grid=(m // tm,),
        in_specs=[pl.BlockSpec((tm, d), lambda i: (i, 0)),
                  pl.BlockSpec((1, d), lambda i: (0, 0))],
        out_specs=pl.BlockSpec((tm, d), lambda i: (i, 0)),
        out_shape=jax.ShapeDtypeStruct((m, d), BF16),
        compiler_params=_params(1, 4 * tm * d * 6 + (8 << 20)),
        name="rmsnorm_bf16",
    )(x, g.reshape(1, d))


def _ws_plain_kernel(x_ref, w_ref, o_ref, wb_ref):
    @pl.when(pl.program_id(1) == 0)
    def _():
        _cast_rows(w_ref, wb_ref, w_ref.shape[0])
    o_ref[...] = _dot(x_ref[...], wb_ref[...]).astype(o_ref.dtype)


def _ws_bf16w_kernel(x_ref, w_ref, o_ref):
    o_ref[...] = _dot(x_ref[...], w_ref[...]).astype(o_ref.dtype)


def _ws_residual_kernel(x_ref, w_ref, r_ref, o_ref, wb_ref):
    @pl.when(pl.program_id(1) == 0)
    def _():
        _cast_rows(w_ref, wb_ref, w_ref.shape[0])
    o_ref[...] = r_ref[...] + _dot(x_ref[...], wb_ref[...])


def _ws_swiglu_kernel(x_ref, wg_ref, wu_ref, o_ref, wgb_ref, wub_ref):
    @pl.when(pl.program_id(1) == 0)
    def _():
        _cast_rows(wg_ref, wgb_ref, wg_ref.shape[0])
        _cast_rows(wu_ref, wub_ref, wu_ref.shape[0])
    x = x_ref[...]
    g = _dot(x, wgb_ref[...])
    u = _dot(x, wub_ref[...])
    o_ref[...] = (g * _sigmoid(g) * u).astype(o_ref.dtype)


def ws_matmul(x, w, layer, *, tm, tn, out_dtype, residual=None, name):
    m, k = x.shape
    n = w.shape[2]
    in_specs = [pl.BlockSpec((tm, k), lambda j, i: (i, 0)),
                pl.BlockSpec((None, k, tn), lambda j, i: (layer, 0, j))]
    args = [x, w]
    kern = _ws_plain_kernel
    scratch = [pltpu.VMEM((k, tn), BF16)]
    if residual is not None:
        in_specs.append(pl.BlockSpec((tm, tn), lambda j, i: (i, j)))
        args.append(residual)
        kern = _ws_residual_kernel
    elif w.dtype == BF16:
        kern, scratch = _ws_bf16w_kernel, []
    vmem = 2 * (tm * k * 2 + k * tn * 4 + 2 * tm * tn * 4) + k * tn * 2 + tm * tn * 4 + (6 << 20)
    return pl.pallas_call(
        kern,
        grid=(n // tn, m // tm),
        in_specs=in_specs,
        out_specs=pl.BlockSpec((tm, tn), lambda j, i: (i, j)),
        out_shape=jax.ShapeDtypeStruct((m, n), out_dtype),
        scratch_shapes=scratch,
        compiler_params=_params(2, vmem),
        name=name,
    )(*args)


def ws_swiglu(x, w_up, layer, d_ff, *, tm, tn, name):
    m, k = x.shape
    nb = d_ff // tn
    vmem = 2 * (tm * k * 2 + 2 * k * tn * 4 + tm * tn * 2) + 2 * k * tn * 2 + 3 * tm * tn * 4 + (6 << 20)
    return pl.pallas_call(
        _ws_swiglu_kernel,
        grid=(nb, m // tm),
        in_specs=[pl.BlockSpec((tm, k), lambda j, i: (i, 0)),
                  pl.BlockSpec((None, k, tn), lambda j, i: (layer, 0, j)),
                  pl.BlockSpec((None, k, tn), lambda j, i: (layer, 0, j + nb))],
        out_specs=pl.BlockSpec((tm, tn), lambda j, i: (i, j)),
        out_shape=jax.ShapeDtypeStruct((m, d_ff), BF16),
        scratch_shapes=[pltpu.VMEM((k, tn), BF16), pltpu.VMEM((k, tn), BF16)],
        compiler_params=_params(2, vmem),
        name=name,
    )(x, w_up, w_up)


NA_ROWS = SEQ // GRID_W
NA_NBLK = SEQ // NA_QBLK
NA_PAIRS = NA_KWIN // NA_QBLK
NA_PATTERN_BLOCKS = (0, 1, 2, NA_NBLK - 2, NA_NBLK - 1)


def _na_key_start(blk):
    return min(max(blk - 2, 0), NA_NBLK - NA_PAIRS)


def _na_pattern(i):
    return jnp.where(i < 2, i, jnp.where(i <= NA_NBLK - 3, 2, i - (NA_NBLK - 5)))


def _na_bias_table(rpb):
    hh = rpb.shape[0]
    span = GRID_W - 1
    left = span - (NA_WIN_C - 1)
    right = 2 * span + 1 - left - (2 * NA_WIN_C - 1)
    ext = jnp.pad(rpb, ((0, 0), (0, 0), (left, right)), constant_values=NEG)
    toep = jnp.stack([ext[:, :, span - qc:span - qc + GRID_W] for qc in range(GRID_W)], axis=2)
    qc = np.arange(GRID_W)
    cs = np.clip(qc - NA_WIN_C // 2, 0, GRID_W - NA_WIN_C)
    kc = np.arange(GRID_W)
    col_ok = (kc[None, :] >= cs[:, None]) & (kc[None, :] < cs[:, None] + NA_WIN_C)
    toep = jnp.where(col_ok[None, None], toep, NEG)
    neg = jnp.full((hh, GRID_W, GRID_W), NEG, rpb.dtype)
    patterns = []
    for blk in NA_PATTERN_BLOCKS:
        q_rows = []
        for a in range(NA_QBLK // GRID_W):
            qr = 2 * blk + a
            rs = min(max(qr - NA_WIN_R // 2, 0), NA_ROWS - NA_WIN_R)
            k_blocks = []
            for c in range(NA_KWIN // GRID_W):
                kr = 2 * _na_key_start(blk) + c
                ok = rs <= kr < rs + NA_WIN_R
                k_blocks.append(toep[:, kr - qr + NA_WIN_R - 1] if ok else neg)
            q_rows.append(jnp.concatenate(k_blocks, axis=-1))
        patterns.append(jnp.concatenate(q_rows, axis=-2))
    return jnp.stack(patterns, axis=0)


def _na_kernel(q_ref, k_ref, v_ref, bias_ref, gq_ref, gk_ref, o_ref, kn_ref, vb_ref):
    i = pl.program_id(1)

    @pl.when(i == 0)
    def _():
        def body(c, carry):
            r = pl.multiple_of(c * 256, 256)
            for h in range(NA_HEADS):
                hs = slice(h * NA_HD, (h + 1) * NA_HD)
                kn_ref[pl.ds(r, 256), hs] = _rms(k_ref[pl.ds(r, 256), hs], gk_ref[...]).astype(BF16)
            vb_ref[pl.ds(r, 256), :] = v_ref[pl.ds(r, 256), :].astype(BF16)
            return carry
        lax.fori_loop(0, SEQ // 256, body, 0)

    start = pl.multiple_of(jnp.clip(i - 2, 0, NA_NBLK - NA_PAIRS) * NA_QBLK, NA_QBLK)
    scale = NA_HD ** -0.5
    for h in range(NA_HEADS):
        hs = slice(h * NA_HD, (h + 1) * NA_HD)
        q = (_rms(q_ref[:, hs], gq_ref[...]) * scale).astype(BF16)
        s = _dot_nt(q, kn_ref[pl.ds(start, NA_KWIN), hs]) + bias_ref[0, h]
        m = jnp.max(s, axis=-1, keepdims=True)
        p = jnp.exp(s - m)
        l = jnp.sum(p, axis=-1, keepdims=True)
        o = _dot(p.astype(BF16), vb_ref[pl.ds(start, NA_KWIN), hs]) / l
        o_ref[:, hs] = o.astype(o_ref.dtype)


def na_attention(z, bias, gq, gk, batch):
    nblk = SEQ // NA_QBLK
    w = NA_HEADS * NA_HD
    vmem = (2 * (NA_QBLK * w * 4 + 2 * SEQ * w * 4 + NA_HEADS * NA_QBLK * NA_KWIN * 4 + NA_QBLK * w * 2)
            + 2 * SEQ * w * 2 + (8 << 20))
    return pl.pallas_call(
        _na_kernel,
        grid=(batch, nblk),
        in_specs=[pl.BlockSpec((NA_QBLK, w), lambda b, i: (b * nblk + i, COL_NA_Q // w)),
                  pl.BlockSpec((SEQ, w), lambda b, i: (b, COL_NA_K // w)),
                  pl.BlockSpec((SEQ, w), lambda b, i: (b, COL_NA_V // w)),
                  pl.BlockSpec((1, NA_HEADS, NA_QBLK, NA_KWIN), lambda b, i: (_na_pattern(i), 0, 0, 0)),
                  pl.BlockSpec((1, NA_HD), lambda b, i: (0, 0)),
                  pl.BlockSpec((1, NA_HD), lambda b, i: (0, 0))],
        out_specs=pl.BlockSpec((NA_QBLK, w), lambda b, i: (b * nblk + i, 0)),
        out_shape=jax.ShapeDtypeStruct((batch * SEQ, w), BF16),
        scratch_shapes=[pltpu.VMEM((SEQ, w), BF16), pltpu.VMEM((SEQ, w), BF16)],
        compiler_params=_params(2, vmem),
        name="na_attention",
    )(z, z, z, bias, gq.reshape(1, NA_HD), gk.reshape(1, NA_HD))


def _rope_tables(width, head_dim, first_lane):
    half = head_dim // 2
    freqs = ROPE_THETA ** (-2.0 * np.arange(half, dtype=np.float32) / head_dim)
    ang = jnp.arange(SEQ, dtype=F32)[:, None] * jnp.asarray(freqs, F32)[None, :]
    cos, sin = jnp.cos(ang), jnp.sin(ang)
    lane = np.arange(width)
    rel = lane - first_lane
    in_rope = (rel >= 0) & (rel < (width - first_lane if first_lane == 0 else head_dim))
    p = np.where(in_rope, rel % head_dim, 0)
    j = p % half
    first_half = in_rope & (p < half)
    second_half = in_rope & (p >= half)
    cos_t = jnp.where(in_rope[None, :], cos[:, j], 1.0)
    sin_a = jnp.where(first_half[None, :], -sin[:, j], 0.0)
    sin_b = jnp.where(second_half[None, :], sin[:, j], 0.0)
    return cos_t, sin_a, sin_b


def _apply_rope(x, cos_t, sin_a, sin_b, half):
    n = x.shape[-1]
    return (x * cos_t + pltpu.roll(x, n - half, axis=1) * sin_a
            + pltpu.roll(x, half, axis=1) * sin_b)


def _mla_prep_kernel(cq_ref, ckv_ref, kpe_ref, wq_ref, wkv_ref, gcq_ref, gckv_ref,
                     gqn_ref, gkn_ref, cos_ref, sa_ref, sb_ref, q_ref, k_ref, v_ref):
    half = MLA_ROPE // 2
    cos_t, sin_a, sin_b = cos_ref[...], sa_ref[...], sb_ref[...]
    scale = MLA_QK ** -0.5
    cq = _rms(cq_ref[...], gcq_ref[...]).astype(BF16)
    q_raw = _dot(cq, wq_ref[...])
    ckv = _rms(ckv_ref[...], gckv_ref[...]).astype(BF16)
    kv_raw = _dot(ckv, wkv_ref[...])
    kpe = kpe_ref[...]
    kpe_ss = jnp.sum(kpe * kpe, axis=-1, keepdims=True)
    for h in range(MLA_HEADS):
        cs = slice(h * MLA_PAD, (h + 1) * MLA_PAD)
        qc = q_raw[:, cs]
        ms = jnp.sum(qc * qc, axis=-1, keepdims=True) * (1.0 / MLA_QK)
        qn = qc * lax.rsqrt(ms + EPS) * gqn_ref[...]
        q_ref[:, cs] = (_apply_rope(qn, cos_t, sin_a, sin_b, half) * scale).astype(BF16)
        kn = kv_raw[:, h * MLA_PAD:h * MLA_PAD + MLA_NOPE]
        ms = (jnp.sum(kn * kn, axis=-1, keepdims=True) + kpe_ss) * (1.0 / MLA_QK)
        kc = jnp.concatenate([kn, kpe], axis=-1) * lax.rsqrt(ms + EPS) * gkn_ref[...]
        k_ref[:, cs] = _apply_rope(kc, cos_t, sin_a, sin_b, half).astype(BF16)
        v_ref[:, h * MLA_NOPE:(h + 1) * MLA_NOPE] = kv_raw[:, h * MLA_PAD + MLA_NOPE:(h + 1) * MLA_PAD].astype(BF16)


def mla_prep(z, wq_pad, wkv, gcq, gckv, gqn_pad, gkn_pad, tabs, tm=512):
    t = z.shape[0]
    sb = SEQ // tm
    qw = MLA_HEADS * MLA_PAD
    row = lambda i: (i, 0)
    const = lambda i: (0, 0)
    pos = lambda i: (i % sb, 0)
    return pl.pallas_call(
        _mla_prep_kernel,
        grid=(t // tm,),
        in_specs=[pl.BlockSpec((tm, MLA_Q_RANK), lambda i: (i, COL_CQ // MLA_Q_RANK)),
                  pl.BlockSpec((tm, MLA_KV_RANK), lambda i: (i, COL_CKV // MLA_KV_RANK)),
                  pl.BlockSpec((tm, LANES), lambda i: (i, COL_KPE // LANES)),
                  pl.BlockSpec((MLA_Q_RANK, qw), const),
                  pl.BlockSpec((MLA_KV_RANK, qw), const),
                  pl.BlockSpec((1, MLA_Q_RANK), const),
                  pl.BlockSpec((1, MLA_KV_RANK), const),
                  pl.BlockSpec((1, MLA_PAD), const),
                  pl.BlockSpec((1, MLA_PAD), const),
                  pl.BlockSpec((tm, MLA_PAD), pos),
                  pl.BlockSpec((tm, MLA_PAD), pos),
                  pl.BlockSpec((tm, MLA_PAD), pos)],
        out_specs=[pl.BlockSpec((tm, qw), row), pl.BlockSpec((tm, qw), row),
                   pl.BlockSpec((tm, MLA_HEADS * MLA_NOPE), row)],
        out_shape=[jax.ShapeDtypeStruct((t, qw), BF16), jax.ShapeDtypeStruct((t, qw), BF16),
                   jax.ShapeDtypeStruct((t, MLA_HEADS * MLA_NOPE), BF16)],
        compiler_params=_params(1, 40 << 20),
        name="mla_prep",
    )(z, z, z, wq_pad, wkv, gcq.reshape(1, -1), gckv.reshape(1, -1), gqn_pad, gkn_pad, *tabs)


def _mla_attn_kernel(q_ref, k_ref, v_ref, o_ref):
    for h in range(MLA_HEADS):
        cs = slice(h * MLA_PAD, (h + 1) * MLA_PAD)
        vs = slice(h * MLA_NOPE, (h + 1) * MLA_NOPE)
        s = _dot_nt(q_ref[:, cs], k_ref[:, cs])
        m = jnp.max(s, axis=-1, keepdims=True)
        p = jnp.exp(s - m)
        l = jnp.sum(p, axis=-1, keepdims=True)
        o = _dot(p.astype(BF16), v_ref[:, vs]) / l
        o_ref[:, vs] = o.astype(o_ref.dtype)


def mla_attention(q, k, v, batch, tq=256):
    nq = SEQ // tq
    qw = MLA_HEADS * MLA_PAD
    vw = MLA_HEADS * MLA_NOPE
    vmem = 2 * (tq * qw * 2 + SEQ * qw * 2 + SEQ * vw * 2 + tq * vw * 2) + 4 * tq * SEQ * 4 + (8 << 20)
    return pl.pallas_call(
        _mla_attn_kernel,
        grid=(batch, nq),
        in_specs=[pl.BlockSpec((tq, qw), lambda b, i: (b * nq + i, 0)),
                  pl.BlockSpec((SEQ, qw), lambda b, i: (b, 0)),
                  pl.BlockSpec((SEQ, vw), lambda b, i: (b, 0))],
        out_specs=pl.BlockSpec((tq, vw), lambda b, i: (b * nq + i, 0)),
        out_shape=jax.ShapeDtypeStruct((batch * SEQ, vw), BF16),
        compiler_params=_params(2, vmem),
        name="mla_attention",
    )(q, k, v)


def _rms_halves(x, g):
    lo = lax.broadcasted_iota(jnp.int32, x.shape, 1) < SWA_HD
    x2 = x * x
    s_lo = jnp.sum(jnp.where(lo, x2, 0.0), axis=-1, keepdims=True)
    s_hi = jnp.sum(jnp.where(lo, 0.0, x2), axis=-1, keepdims=True)
    ms = jnp.where(lo, s_lo, s_hi) * (1.0 / SWA_HD)
    return x * lax.rsqrt(ms + EPS) * g


def _swa_kernel(sink_ref, q_ref, k_ref, v_ref, gq_ref, gk_ref, cosq_ref, saq_ref, sbq_ref,
                cosk_ref, sak_ref, sbk_ref, o_ref, kk_ref, vv_ref):
    n = pl.program_id(1)
    nblk = SEQ // SWA_BLOCK
    half = SWA_HD // 2

    @pl.when(n == 0)
    def _():
        zeros = jnp.zeros((SWA_BLOCK, LANES), BF16)
        for c in range(4):
            kk_ref[c, pl.ds(0, SWA_BLOCK), :] = zeros
            kk_ref[c, pl.ds(SEQ + SWA_BLOCK, SWA_BLOCK), :] = zeros
            vv_ref[c, pl.ds(0, SWA_BLOCK), :] = zeros
            vv_ref[c, pl.ds(SEQ + SWA_BLOCK, SWA_BLOCK), :] = zeros

        def body(c, carry):
            r = pl.multiple_of(c * 256, 256)
            dst = pl.ds(r + SWA_BLOCK, 256)
            lo = lax.broadcasted_iota(jnp.int32, (256, LANES), 1) < SWA_HD
            kr = _apply_rope(_rms_halves(k_ref[pl.ds(r, 256), :], gk_ref[...]),
                             cosk_ref[pl.ds(r, 256), :], sak_ref[pl.ds(r, 256), :],
                             sbk_ref[pl.ds(r, 256), :], half)
            ks = pltpu.roll(kr, SWA_HD, axis=1)
            kk_ref[0, dst, :] = jnp.where(lo, kr, 0.0).astype(BF16)
            kk_ref[1, dst, :] = jnp.where(lo, 0.0, ks).astype(BF16)
            kk_ref[2, dst, :] = jnp.where(lo, ks, 0.0).astype(BF16)
            kk_ref[3, dst, :] = jnp.where(lo, 0.0, kr).astype(BF16)
            vr = v_ref[pl.ds(r, 256), :]
            vs = pltpu.roll(vr, SWA_HD, axis=1)
            vv_ref[0, dst, :] = jnp.where(lo, vr, 0.0).astype(BF16)
            vv_ref[1, dst, :] = jnp.where(lo, 0.0, vs).astype(BF16)
            vv_ref[2, dst, :] = jnp.where(lo, vs, 0.0).astype(BF16)
            vv_ref[3, dst, :] = jnp.where(lo, 0.0, vr).astype(BF16)
            return carry
        lax.fori_loop(0, SEQ // 256, body, 0)

    band = pl.ds(pl.multiple_of(n * SWA_BLOCK, SWA_BLOCK), 3 * SWA_BLOCK)
    a = lax.broadcasted_iota(jnp.int32, (SWA_BLOCK, 3 * SWA_BLOCK), 0)
    c = lax.broadcasted_iota(jnp.int32, (SWA_BLOCK, 3 * SWA_BLOCK), 1)
    c_min = jnp.where(n == 0, SWA_BLOCK, 0)
    c_max = jnp.where(n == nblk - 1, 2 * SWA_BLOCK, 3 * SWA_BLOCK)
    valid = (c >= a) & (c <= a + 2 * SWA_BLOCK) & (c >= c_min) & (c < c_max)
    scale = SWA_HD ** -0.5
    for pair in range(SWA_HEADS // 2):
        ps = slice(pair * LANES, (pair + 1) * LANES)
        grp = pair // 2
        q = _apply_rope(_rms_halves(q_ref[:, ps], gq_ref[...]),
                        cosq_ref[...], saq_ref[...], sbq_ref[...], half)
        q = (q * scale).astype(BF16)
        acc = None
        for hf in range(2):
            sink = sink_ref[2 * pair + hf]
            s = _dot_nt(q, kk_ref[2 * grp + hf, band, :])
            s = jnp.where(valid, s, NEG)
            m = jnp.maximum(jnp.max(s, axis=-1, keepdims=True), sink)
            p = jnp.exp(s - m)
            den = jnp.sum(p, axis=-1, keepdims=True) + jnp.exp(sink - m)
            o = _dot(p.astype(BF16), vv_ref[2 * grp + hf, band, :]) / den
            acc = o if acc is None else acc + o
        o_ref[:, ps] = acc.astype(o_ref.dtype)


def swa_attention(z, sink, gq, gk, tabs, batch):
    nblk = SEQ // SWA_BLOCK
    qw = SWA_HEADS * SWA_HD
    cos_t, sin_a, sin_b = tabs
    g2 = lambda g: jnp.concatenate([g, g]).reshape(1, LANES)
    blk = lambda b, n: (n, 0)
    full = lambda b, n: (0, 0)
    tab_blk = pl.BlockSpec((SWA_BLOCK, LANES), blk)
    tab_full = pl.BlockSpec((SEQ, LANES), full)
    vmem = (2 * (SWA_BLOCK * qw * 6 + 2 * SEQ * LANES * 4 + 3 * SEQ * LANES * 4 + 3 * SWA_BLOCK * LANES * 4)
            + 8 * (SEQ + 2 * SWA_BLOCK) * LANES * 2 + (8 << 20))
    return pl.pallas_call(
        _swa_kernel,
        grid=(batch, nblk),
        in_specs=[pl.BlockSpec(memory_space=pltpu.SMEM),
                  pl.BlockSpec((SWA_BLOCK, qw), lambda b, n: (b * nblk + n, COL_SWQ // qw)),
                  pl.BlockSpec((SEQ, LANES), lambda b, n: (b, COL_SWK // LANES)),
                  pl.BlockSpec((SEQ, LANES), lambda b, n: (b, COL_SWV // LANES)),
                  pl.BlockSpec((1, LANES), full), pl.BlockSpec((1, LANES), full),
                  tab_blk, tab_blk, tab_blk, tab_full, tab_full, tab_full],
        out_specs=pl.BlockSpec((SWA_BLOCK, qw), lambda b, n: (b * nblk + n, 0)),
        out_shape=jax.ShapeDtypeStruct((batch * SEQ, qw), BF16),
        scratch_shapes=[pltpu.VMEM((4, SEQ + 2 * SWA_BLOCK, LANES), BF16),
                        pltpu.VMEM((4, SEQ + 2 * SWA_BLOCK, LANES), BF16)],
        compiler_params=_params(2, vmem),
        name="swa_attention",
    )(sink, z, z, z, g2(gq), g2(gk), cos_t, sin_a, sin_b, cos_t, sin_a, sin_b)


def _mem_attn_kernel(q_ref, k_ref, v_ref, gq_ref, gk_ref, o_ref):
    scale = MEM_HD ** -0.5
    for h in range(MEM_HEADS):
        hs = slice(h * MEM_HD, (h + 1) * MEM_HD)
        q = (_rms(q_ref[:, hs], gq_ref[...]) * scale).astype(BF16)
        k = _rms(k_ref[:, hs], gk_ref[...]).astype(BF16)
        s = _dot_nt(q, k)
        m = jnp.max(s, axis=-1, keepdims=True)
        p = jnp.exp(s - m)
        l = jnp.sum(p, axis=-1, keepdims=True)
        o = _dot(p.astype(BF16), v_ref[:, hs].astype(BF16)) / l
        o_ref[:, hs] = o.astype(o_ref.dtype)


def mem_attention(z, memkv, gq, gk, batch, tq=512):
    nq = SEQ // tq
    w = MEM_HEADS * MEM_HD
    return pl.pallas_call(
        _mem_attn_kernel,
        grid=(batch, nq),
        in_specs=[pl.BlockSpec((tq, w), lambda b, i: (b * nq + i, COL_MEQ // w)),
                  pl.BlockSpec((MEM_LEN, w), lambda b, i: (b, 0)),
                  pl.BlockSpec((MEM_LEN, w), lambda b, i: (b, 1)),
                  pl.BlockSpec((1, MEM_HD), lambda b, i: (0, 0)),
                  pl.BlockSpec((1, MEM_HD), lambda b, i: (0, 0))],
        out_specs=pl.BlockSpec((tq, w), lambda b, i: (b * nq + i, 0)),
        out_shape=jax.ShapeDtypeStruct((batch * SEQ, w), BF16),
        compiler_params=_params(2, 24 << 20),
        name="mem_attention",
    )(z, memkv, memkv, gq.reshape(1, MEM_HD), gk.reshape(1, MEM_HD))


def _merge_kernel(layer, u_ref, b0_ref, b1_ref, b2_ref, b3_ref, wg_hbm, wb_ref, bg_ref, o_ref,
                  stage_ref, wgs_ref, wbs_ref, sem):
    branches = (b0_ref, b1_ref, b2_ref, b3_ref)
    j, i = pl.program_id(0), pl.program_id(1)
    tn = wgs_ref.shape[2]
    slot = j % 2

    def copies(jj, s):
        return [pltpu.make_async_copy(wg_hbm.at[layer, :, n, pl.ds(pl.multiple_of(jj * tn, tn), tn)],
                                      stage_ref.at[s, n], sem.at[s]) for n in range(4)]

    @pl.when(jnp.logical_and(j == 0, i == 0))
    def _():
        for c in copies(0, 0):
            c.start()

    @pl.when(i == 0)
    def _():
        for c in copies(j, slot):
            c.wait()

        @pl.when(j + 1 < pl.num_programs(0))
        def _():
            for c in copies(j + 1, 1 - slot):
                c.start()
        for n in range(4):
            _cast_rows(stage_ref.at[slot, n], wgs_ref.at[n], D_MODEL)
            wbs_ref[n] = wb_ref[n].astype(BF16)

    u = u_ref[...]
    acc = None
    for n in range(4):
        gate = _sigmoid(_dot(u, wgs_ref[n]) + bg_ref[n:n + 1, :])
        term = gate * _dot(branches[n][...], wbs_ref[n])
        acc = term if acc is None else acc + term
    o_ref[...] = acc.astype(o_ref.dtype)


def gated_merge(u, branches, w_gate, w_branch, b_gate, layer, tm=1024, tn=256):
    t = u.shape[0]
    bw = branches[0].shape[1]
    vmem = (2 * (tm * D_MODEL * 2 + 4 * tm * bw * 2 + 4 * D_MODEL * tn * 4 + 4 * bw * tn * 4 + tm * tn * 2)
            + 4 * D_MODEL * tn * 2 + 4 * bw * tn * 2 + 4 * tm * tn * 4 + (6 << 20))
    return pl.pallas_call(
        functools.partial(_merge_kernel, layer),
        grid=(D_MODEL // tn, t // tm),
        in_specs=[pl.BlockSpec((tm, D_MODEL), lambda j, i: (i, 0))]
        + [pl.BlockSpec((tm, bw), lambda j, i: (i, 0)) for _ in range(4)]
        + [pl.BlockSpec(memory_space=pl.ANY),
           pl.BlockSpec((None, 4, bw, tn), lambda j, i: (layer, 0, 0, j)),
           pl.BlockSpec((None, 4, tn), lambda j, i: (layer, 0, j))],
        out_specs=pl.BlockSpec((tm, tn), lambda j, i: (i, j)),
        out_shape=jax.ShapeDtypeStruct((t, D_MODEL), BF16),
        scratch_shapes=[pltpu.VMEM((2, 4, D_MODEL, tn), F32), pltpu.VMEM((4, D_MODEL, tn), BF16),
                        pltpu.VMEM((4, bw, tn), BF16), pltpu.SemaphoreType.DMA((2,))],
        compiler_params=_params(2, vmem),
        name="gated_merge",
    )(u, *branches, w_gate, w_branch, b_gate)


def _router_kernel(h_ref, g_ref, wr_ref, ri_ref, rw_ref, cnt_ref, carry_ref):
    @pl.when(pl.program_id(0) == 0)
    def _():
        carry_ref[...] = jnp.zeros_like(carry_ref)

    tm = h_ref.shape[0]
    hn = _rms(h_ref[...], g_ref[...])
    logits = jnp.dot(hn, wr_ref[...], precision=lax.Precision.HIGHEST, preferred_element_type=F32)
    lane = lax.broadcasted_iota(jnp.int32, (tm, LANES), 1)
    lane_f = lane.astype(F32)
    logits = jnp.where(lane < N_EXPERTS, logits, -jnp.inf)
    m1 = jnp.max(logits, axis=-1, keepdims=True)
    i1 = jnp.min(jnp.where(logits == m1, lane_f, float(LANES)), axis=-1, keepdims=True)
    oh1 = lane_f == i1
    rest = jnp.where(oh1, -jnp.inf, logits)
    m2 = jnp.max(rest, axis=-1, keepdims=True)
    i2 = jnp.min(jnp.where(rest == m2, lane_f, float(LANES)), axis=-1, keepdims=True)
    oh2 = lane_f == i2
    e2 = jnp.exp(m2 - m1)
    w1 = 1.0 / (1.0 + e2)
    w2 = e2 / (1.0 + e2)
    chosen = jnp.where(oh1 | oh2, 1.0, 0.0)
    before = (lax.broadcasted_iota(jnp.int32, (tm, tm), 1)
              < lax.broadcasted_iota(jnp.int32, (tm, tm), 0))
    prefix = _dot(jnp.where(before, 1.0, 0.0).astype(BF16), chosen.astype(BF16)) + carry_ref[...]
    r1 = jnp.sum(jnp.where(oh1, prefix, 0.0), axis=-1, keepdims=True).astype(jnp.int32)
    r2 = jnp.sum(jnp.where(oh2, prefix, 0.0), axis=-1, keepdims=True).astype(jnp.int32)
    carry_ref[...] += jnp.sum(chosen, axis=0, keepdims=True)
    ri_ref[...] = jnp.where(lane == 0, i1.astype(jnp.int32),
                            jnp.where(lane == 1, i2.astype(jnp.int32),
                                      jnp.where(lane == 2, r1, jnp.where(lane == 3, r2, 0))))
    rw_ref[...] = jnp.where(lane == 0, w1, jnp.where(lane == 1, w2, 0.0))
    cnt_ref[...] = carry_ref[...]


def moe_router(h, g, w_router, tm=512):
    t, d = h.shape
    wr = jnp.pad(w_router, ((0, 0), (0, LANES - N_EXPERTS)))
    return pl.pallas_call(
        _router_kernel,
        grid=(t // tm,),
        in_specs=[pl.BlockSpec((tm, d), lambda i: (i, 0)),
                  pl.BlockSpec((1, d), lambda i: (0, 0)),
                  pl.BlockSpec((d, LANES), lambda i: (0, 0))],
        out_specs=[pl.BlockSpec((tm, LANES), lambda i: (i, 0)),
                   pl.BlockSpec((tm, LANES), lambda i: (i, 0)),
                   pl.BlockSpec((1, LANES), lambda i: (0, 0))],
        out_shape=[jax.ShapeDtypeStruct((t, LANES), jnp.int32),
                   jax.ShapeDtypeStruct((t, LANES), F32),
                   jax.ShapeDtypeStruct((1, LANES), F32)],
        scratch_shapes=[pltpu.VMEM((1, LANES), F32)],
        compiler_params=_params(1, 32 << 20),
        name="moe_router",
    )(h, g.reshape(1, d), wr)


ROW_DMA_UNROLL = 8


def _row_copy(src_hbm, row, dst_vmem, r, sem):
    return pltpu.make_async_copy(src_hbm.at[pl.ds(row, 1), :], dst_vmem.at[pl.ds(r, 1), :], sem)


HI16 = 0xFFFF0000


def _pack_bf16_pairs(x):
    half = x.shape[1] // 2
    bits = lax.bitcast_convert_type(x.astype(BF16).astype(F32), jnp.uint32)
    return (bits[:, :half] >> 16) | (bits[:, half:] & jnp.uint32(HI16))


def _unpack_bf16_pairs(u):
    lo = lax.bitcast_convert_type(u << 16, F32).astype(BF16)
    hi = lax.bitcast_convert_type(u & jnp.uint32(HI16), F32).astype(BF16)
    return lo, hi


def _moe_scatter_kernel(pos_ref, h_ref, g_ref, xs_in_hbm, xs_hbm, pk_ref, sem):
    del xs_in_hbm
    tg = h_ref.shape[0]
    i = pl.program_id(0)
    last = pl.num_programs(0) - 1
    slot = i % 2

    def row_copy(s, r, dst_row):
        return pltpu.make_async_copy(pk_ref.at[s, pl.ds(r, 1), :], xs_hbm.at[pl.ds(dst_row, 1), :], sem.at[s])

    def wait_slot(s):
        def drain(r, carry):
            row_copy(s, r, 0).wait()
            row_copy(s, r, 0).wait()
            return carry
        lax.fori_loop(0, tg, drain, 0, unroll=8)

    @pl.when(i >= 2)
    def _():
        wait_slot(slot)

    pk_ref[slot] = _pack_bf16_pairs(_rms(h_ref[...], g_ref[...]))

    def issue(grp, carry):
        for k in range(ROW_DMA_UNROLL):
            r = grp * ROW_DMA_UNROLL + k
            t = i * tg + r
            row_copy(slot, r, pos_ref[2 * t]).start(priority=0)
            row_copy(slot, r, pos_ref[2 * t + 1]).start(priority=1)
        return carry
    lax.fori_loop(0, tg // ROW_DMA_UNROLL, issue, 0)

    @pl.when(jnp.logical_and(i == last, i >= 1))
    def _():
        wait_slot(1 - slot)

    @pl.when(i == last)
    def _():
        wait_slot(slot)


def moe_scatter(pos_flat, h, g, rows, tg=256):
    t, d = h.shape
    xs0 = jnp.zeros((rows, d // 2), jnp.uint32)
    return pl.pallas_call(
        _moe_scatter_kernel,
        grid_spec=pltpu.PrefetchScalarGridSpec(
            num_scalar_prefetch=1,
            grid=(t // tg,),
            in_specs=[pl.BlockSpec((tg, d), lambda i, p: (i, 0)),
                      pl.BlockSpec((1, d), lambda i, p: (0, 0)),
                      pl.BlockSpec(memory_space=pl.ANY)],
            out_specs=pl.BlockSpec(memory_space=pl.ANY),
            scratch_shapes=[pltpu.VMEM((2, tg, d // 2), jnp.uint32), pltpu.SemaphoreType.DMA((2,))]),
        out_shape=jax.ShapeDtypeStruct((rows, d // 2), jnp.uint32),
        input_output_aliases={3: 0},
        compiler_params=_params(1, 24 << 20),
        name="moe_scatter",
    )(pos_flat, h, g.reshape(1, d), xs0)


def _stream_expert_weights(te_ref, first_ref, nxt_ref, meta_ref, cnt_ref, make_copies, cast_slot):
    j, i = pl.program_id(0), pl.program_id(1)

    @pl.when(jnp.logical_and(j == 0, i == 0))
    def _():
        cnt_ref[0] = 0
        for c in make_copies(0, te_ref[0], 0):
            c.start()

    @pl.when(first_ref[i] == 1)
    def _():
        slot = cnt_ref[0] % 2
        for c in make_copies(j, te_ref[i], slot):
            c.wait()
        wraps = nxt_ref[i] < 0
        j_next = jnp.where(wraps, j + 1, j)
        e_next = jnp.where(wraps, meta_ref[1], nxt_ref[i])

        @pl.when(j_next < pl.num_programs(0))
        def _():
            for c in make_copies(j_next, e_next, 1 - slot):
                c.start()
        cast_slot(slot)
        cnt_ref[0] = cnt_ref[0] + 1


def _moe_up_kernel(te_ref, first_ref, nxt_ref, meta_ref, x_ref, w_hbm, o_ref,
                   stage_ref, wgb_ref, wub_ref, sem, cnt_ref):
    i = pl.program_id(1)
    active = i < meta_ref[0]
    tf = wgb_ref.shape[1]
    nf = pl.num_programs(0)

    def make_copies(jj, e, slot):
        return [pltpu.make_async_copy(w_hbm.at[e, :, pl.ds(pl.multiple_of((jj + half * nf) * tf, tf), tf)],
                                      stage_ref.at[slot, half], sem.at[slot]) for half in range(2)]

    def cast_slot(slot):
        _cast_rows(stage_ref.at[slot, 0], wgb_ref, D_MODEL)
        _cast_rows(stage_ref.at[slot, 1], wub_ref, D_MODEL)

    _stream_expert_weights(te_ref, first_ref, nxt_ref, meta_ref, cnt_ref, make_copies, cast_slot)

    @pl.when(active)
    def _():
        half = D_MODEL // 2
        x_lo, x_hi = _unpack_bf16_pairs(x_ref[...])
        g = _dot(x_lo, wgb_ref[pl.ds(0, half), :]) + _dot(x_hi, wgb_ref[pl.ds(half, half), :])
        u = _dot(x_lo, wub_ref[pl.ds(0, half), :]) + _dot(x_hi, wub_ref[pl.ds(half, half), :])
        o_ref[...] = (g * _sigmoid(g) * u).astype(o_ref.dtype)

    @pl.when(jnp.logical_not(active))
    def _():
        o_ref[...] = jnp.zeros_like(o_ref)


def _active_row(j, i, te, first, nxt, meta):
    return jnp.maximum(jnp.minimum(i, meta[0] - 1), 0)


def moe_up(xs, w_up, sched, tf=512):
    rows, dp = xs.shape
    d = 2 * dp
    nf = D_FF_EXPERT // tf
    vmem = (2 * (MOE_TM * d * 2 + MOE_TM * tf * 2) + 4 * d * tf * 4 + 2 * d * tf * 2
            + 3 * MOE_TM * tf * 4 + 2 * MOE_TM * d * 2 + (6 << 20))
    return pl.pallas_call(
        _moe_up_kernel,
        grid_spec=pltpu.PrefetchScalarGridSpec(
            num_scalar_prefetch=4,
            grid=(nf, rows // MOE_TM),
            in_specs=[pl.BlockSpec((MOE_TM, dp), lambda j, i, *s: (_active_row(j, i, *s), 0)),
                      pl.BlockSpec(memory_space=pl.ANY)],
            out_specs=pl.BlockSpec((MOE_TM, tf), lambda j, i, *s: (i, j)),
            scratch_shapes=[pltpu.VMEM((2, 2, d, tf), F32), pltpu.VMEM((d, tf), BF16),
                            pltpu.VMEM((d, tf), BF16), pltpu.SemaphoreType.DMA((2,)),
                            pltpu.SMEM((1,), jnp.int32)]),
        out_shape=jax.ShapeDtypeStruct((rows, D_FF_EXPERT), BF16),
        compiler_params=_params(2, vmem),
        name="moe_up",
    )(*sched, xs, w_up)


def _moe_down_kernel(k_half, te_ref, first_ref, nxt_ref, meta_ref, a_ref, w_hbm, *rest):
    if len(rest) == 6:
        part_ref, o_ref, stage_ref, wb_ref, sem, cnt_ref = rest
    else:
        part_ref, (o_ref, stage_ref, wb_ref, sem, cnt_ref) = None, rest
    i = pl.program_id(1)
    active = i < meta_ref[0]
    fk, tn = wb_ref.shape

    def make_copies(jj, e, slot):
        return [pltpu.make_async_copy(
            w_hbm.at[e, pl.ds(k_half * fk, fk), pl.ds(pl.multiple_of(jj * tn, tn), tn)],
            stage_ref.at[slot], sem.at[slot])]

    def cast_slot(slot):
        _cast_rows(stage_ref.at[slot], wb_ref, fk)

    _stream_expert_weights(te_ref, first_ref, nxt_ref, meta_ref, cnt_ref, make_copies, cast_slot)

    @pl.when(active)
    def _():
        y = _dot(a_ref[...], wb_ref[...])
        o_ref[...] = y if part_ref is None else part_ref[...] + y

    @pl.when(jnp.logical_not(active))
    def _():
        o_ref[...] = jnp.zeros_like(o_ref)


def moe_down(act, w_down, sched, k_half, partial=None, tn=512):
    rows, f = act.shape
    fk = f // 2
    in_specs = [pl.BlockSpec((MOE_TM, fk), lambda j, i, *s: (_active_row(j, i, *s), k_half)),
                pl.BlockSpec(memory_space=pl.ANY)]
    args = [*sched, act, w_down]
    if partial is not None:
        in_specs.append(pl.BlockSpec((MOE_TM, tn), lambda j, i, *s: (i, j)))
        args.append(partial)
    vmem = 2 * (MOE_TM * fk * 2 + fk * tn * 4 + 2 * MOE_TM * tn * 4) + fk * tn * 2 + MOE_TM * tn * 4 + (6 << 20)
    return pl.pallas_call(
        functools.partial(_moe_down_kernel, k_half),
        grid_spec=pltpu.PrefetchScalarGridSpec(
            num_scalar_prefetch=4,
            grid=(D_MODEL // tn, rows // MOE_TM),
            in_specs=in_specs,
            out_specs=pl.BlockSpec((MOE_TM, tn), lambda j, i, *s: (i, j)),
            scratch_shapes=[pltpu.VMEM((2, fk, tn), F32), pltpu.VMEM((fk, tn), BF16),
                            pltpu.SemaphoreType.DMA((2,)), pltpu.SMEM((1,), jnp.int32)]),
        out_shape=jax.ShapeDtypeStruct((rows, D_MODEL), F32),
        compiler_params=_params(2, vmem),
        name="moe_down_%d" % k_half,
    )(*args)


def _moe_combine_kernel(pos_ref, h_ref, rw_ref, y_hbm, o_ref, b1_ref, b2_ref, sem):
    tc = b1_ref.shape[1]
    i = pl.program_id(0)
    slot = i % 2

    def start_rows(step, dst_slot):
        def issue(grp, carry):
            for k in range(ROW_DMA_UNROLL):
                r = grp * ROW_DMA_UNROLL + k
                t = step * tc + r
                _row_copy(y_hbm, pos_ref[2 * t], b1_ref.at[dst_slot], r, sem.at[dst_slot]).start(priority=0)
                _row_copy(y_hbm, pos_ref[2 * t + 1], b2_ref.at[dst_slot], r, sem.at[dst_slot]).start(priority=1)
            return carry
        lax.fori_loop(0, tc // ROW_DMA_UNROLL, issue, 0)

    @pl.when(i == 0)
    def _():
        start_rows(0, 0)

    @pl.when(i + 1 < pl.num_programs(0))
    def _():
        start_rows(i + 1, 1 - slot)

    def drain(r, carry):
        _row_copy(y_hbm, 0, b1_ref.at[slot], r, sem.at[slot]).wait()
        _row_copy(y_hbm, 0, b2_ref.at[slot], r, sem.at[slot]).wait()
        return carry
    lax.fori_loop(0, tc, drain, 0, unroll=8)
    o_ref[...] = h_ref[...] + rw_ref[:, 0:1] * b1_ref[slot] + rw_ref[:, 1:2] * b2_ref[slot]


def moe_combine(pos_flat, h, rw, y, tc=256):
    t, d = h.shape
    return pl.pallas_call(
        _moe_combine_kernel,
        grid_spec=pltpu.PrefetchScalarGridSpec(
            num_scalar_prefetch=1,
            grid=(t // tc,),
            in_specs=[pl.BlockSpec((tc, d), lambda i, p: (i, 0)),
                      pl.BlockSpec((tc, LANES), lambda i, p: (i, 0)),
                      pl.BlockSpec(memory_space=pl.ANY)],
            out_specs=pl.BlockSpec((tc, d), lambda i, p: (i, 0)),
            scratch_shapes=[pltpu.VMEM((2, tc, d), F32), pltpu.VMEM((2, tc, d), F32),
                            pltpu.SemaphoreType.DMA((2,))]),
        out_shape=jax.ShapeDtypeStruct((t, d), F32),
        compiler_params=_params(1, 32 << 20),
        name="moe_combine",
    )(pos_flat, h, rw, y)


def moe_block(h, g, w_router, w_up, w_down):
    t = h.shape[0]
    rows = 2 * t + N_EXPERTS * MOE_TM
    n_row_tiles = rows // MOE_TM
    ri, rw, cnt = moe_router(h, g, w_router)
    counts = cnt[0, :N_EXPERTS].astype(jnp.int32)
    padded = ((counts + MOE_TM - 1) // MOE_TM) * MOE_TM
    ends = jnp.cumsum(padded)
    starts = ends - padded
    pos = starts[ri[:, 0:2]] + ri[:, 2:4]
    pos_flat = pos.reshape(-1).astype(jnp.int32)
    n_tiles = (ends[-1] // MOE_TM).astype(jnp.int32)
    tile_first_row = jnp.arange(n_row_tiles, dtype=jnp.int32) * MOE_TM
    tile_e = jnp.sum((tile_first_row[:, None] >= ends[None, :]).astype(jnp.int32), axis=1)
    tile_e = jnp.minimum(tile_e, N_EXPERTS - 1)
    tile_idx = jnp.arange(n_row_tiles, dtype=jnp.int32)
    is_active = tile_idx < n_tiles
    tile_e = jnp.where(is_active, tile_e, tile_e[n_tiles - 1]).astype(jnp.int32)
    first = (is_active & ((tile_idx == 0) | (tile_e != jnp.roll(tile_e, 1)))).astype(jnp.int32)
    eidx = jnp.arange(N_EXPERTS, dtype=jnp.int32)
    later = (eidx[None, :] > eidx[:, None]) & (padded > 0)[None, :]
    nxt = jnp.min(jnp.where(later, eidx[None, :], N_EXPERTS), axis=1)
    nxt = jnp.where(nxt == N_EXPERTS, -1, nxt).astype(jnp.int32)
    sched = (tile_e, first, nxt[tile_e], jnp.stack([n_tiles, tile_e[0]]).astype(jnp.int32))

    xs = moe_scatter(pos_flat, h, g, rows)
    act = moe_up(xs, w_up, sched)
    y = moe_down(act, w_down, sched, 0)
    y = moe_down(act, w_down, sched, 1, partial=y)
    return moe_combine(pos_flat, h, rw, y)


IN_PROJ_MOVES = ((0, 2560, 0), (2560, 64, COL_KPE), (2624, 512, COL_SWQ), (3136, 128, COL_SWK),
                 (3264, 128, COL_SWV), (3392, 512, COL_MEQ))


def _in_proj_layout_kernel(w_ref, o_ref):
    for src, width, dst in IN_PROJ_MOVES:
        o_ref[:, dst:dst + width] = w_ref[:, src:src + width].astype(o_ref.dtype)
    tail = COL_KPE + MLA_ROPE
    o_ref[:, tail:] = jnp.zeros((o_ref.shape[0], IN_COLS_PAD - tail), o_ref.dtype)


def _pad_in_proj(w_in, tk=256):
    nl, d, n = w_in.shape
    return pl.pallas_call(
        _in_proj_layout_kernel,
        grid=(nl, d // tk),
        in_specs=[pl.BlockSpec((None, tk, n), lambda l, i: (l, i, 0))],
        out_specs=pl.BlockSpec((None, tk, IN_COLS_PAD), lambda l, i: (l, i, 0)),
        out_shape=jax.ShapeDtypeStruct((nl, d, IN_COLS_PAD), BF16),
        compiler_params=_params(2, 32 << 20),
        name="in_proj_layout",
    )(w_in)


def _pad_heads(x, n_heads, width):
    lead = x.shape[:-1]
    x = x.reshape(lead + (n_heads, width))
    x = jnp.pad(x, [(0, 0)] * len(lead) + [(0, 0), (0, MLA_PAD - width)])
    return x.reshape(lead + (n_heads * MLA_PAD,))


def kernel(x, mem, norm_mix, w_in, na_q_norm, na_k_norm, na_rpb, mla_cq_norm, mla_w_uq, mla_ckv_norm, mla_w_ukv, mla_q_norm, mla_k_norm, swa_q_norm, swa_k_norm, swa_sink, mem_norm, mem_w_kv, mem_q_norm, mem_k_norm, w_branch, w_gate, b_gate, w_o, norm_ffn, ffn_w_up, ffn_w_down, moe_router, moe_w_up, moe_w_down):
    batch, seq, d = x.shape
    assert (seq, d) == (SEQ, D_MODEL) and mem.shape[1] == MEM_LEN
    t = batch * seq
    depth = w_in.shape[0]
    h = x.reshape(t, d)
    mem2 = mem.reshape(batch * MEM_LEN, d)
    mla_tabs = _rope_tables(MLA_PAD, MLA_ROPE, MLA_NOPE)
    swa_tabs = _rope_tables(LANES, SWA_HD, 0)
    w_in_pad = _pad_in_proj(w_in)
    wq_pad = _pad_heads(mla_w_uq, MLA_HEADS, MLA_QK).astype(BF16)
    wkv_bf = mla_w_ukv.astype(BF16)

    for l in range(depth):
        u = rmsnorm_bf16(h, norm_mix[l])
        z = ws_matmul(u, w_in_pad, l, tm=1024, tn=512, out_dtype=F32, name="in_proj")

        o_na = na_attention(z, _na_bias_table(na_rpb[l]), na_q_norm[l], na_k_norm[l], batch)

        q_mla, k_mla, v_mla = mla_prep(
            z, wq_pad[l], wkv_bf[l], mla_cq_norm[l], mla_ckv_norm[l],
            jnp.pad(mla_q_norm[l], (0, MLA_PAD - MLA_QK)).reshape(1, MLA_PAD),
            jnp.pad(mla_k_norm[l], (0, MLA_PAD - MLA_QK)).reshape(1, MLA_PAD), mla_tabs)
        o_mla = mla_attention(q_mla, k_mla, v_mla, batch)

        o_swa = swa_attention(z, swa_sink[l], swa_q_norm[l], swa_k_norm[l], swa_tabs, batch)

        memn = rmsnorm_bf16(mem2, mem_norm[l])
        memkv = ws_matmul(memn, mem_w_kv, l, tm=1024, tn=512, out_dtype=F32, name="mem_kv")
        o_mem = mem_attention(z, memkv, mem_q_norm[l], mem_k_norm[l], batch)

        merged = gated_merge(u, (o_na, o_mla, o_swa, o_mem), w_gate, w_branch, b_gate, l)
        h = ws_matmul(merged, w_o, l, tm=512, tn=1024, out_dtype=F32, residual=h, name="out_proj")

        if l % 2 == 0:
            hn = rmsnorm_bf16(h, norm_ffn[l])
            act = ws_swiglu(hn, ffn_w_up, l // 2, D_FF, tm=1024, tn=512, name="ffn_up")
            h = ws_matmul(act, ffn_w_down, l // 2, tm=512, tn=512, out_dtype=F32, residual=h, name="ffn_down")
        else:
            h = moe_block(h, norm_ffn[l], moe_router[l // 2], moe_w_up[l // 2], moe_w_down[l // 2])
    return h.reshape(batch, seq, d)
```

```python
import functools

import jax
import jax.numpy as jnp
import numpy as np
from jax import lax
from jax.experimental import pallas as pl
from jax.experimental.pallas import tpu as pltpu

F32 = jnp.float32
BF16 = jnp.bfloat16

D_MODEL = 2048
SEQ = 2048
MEM_LEN = 256
GRID_W = 64
ROPE_THETA = 10000.0
EPS = 1e-6
NEG = -1e30

NA_HEADS = 4
NA_HD = 128
NA_WIN_R = 8
NA_WIN_C = 16
NA_QBLK = 256
NA_KWIN = 768

MLA_HEADS = 4
MLA_NOPE = 128
MLA_ROPE = 64
MLA_QK = MLA_NOPE + MLA_ROPE
MLA_PAD = 256
MLA_Q_RANK = 768
MLA_KV_RANK = 256

SWA_HEADS = 8
SWA_KV_HEADS = 2
SWA_HD = 64
SWA_BLOCK = 128
SWA_QBLK = 256
SWA_BAND = SWA_QBLK + 2 * SWA_BLOCK

MEM_HEADS = 4
MEM_HD = 128

D_FF = 5632
N_EXPERTS = 8
D_FF_EXPERT = 7168
MOE_TM = 512

LANES = 128
VMEM_CAP = 60000 * 1024

COL_NA_Q, COL_NA_K, COL_NA_V = 0, 512, 1024
COL_CQ, COL_CKV = 1536, 2304
COL_SWQ, COL_MEQ, COL_SWK, COL_SWV, COL_KPE = 2560, 3072, 3584, 3712, 3840
IN_COLS_PAD = 4096


def _params(n_axes, vmem_bytes):
    return pltpu.CompilerParams(
        dimension_semantics=("arbitrary",) * n_axes,
        vmem_limit_bytes=int(min(VMEM_CAP, vmem_bytes)))


def _rms(x, g):
    ms = jnp.mean(x * x, axis=-1, keepdims=True)
    return x * lax.rsqrt(ms + EPS) * g


def _sigmoid(x):
    return 1.0 / (1.0 + jnp.exp(-x))


def _cast_rows(src_ref, dst_ref, rows, chunk=256):
    def body(c, carry):
        r = pl.multiple_of(c * chunk, chunk)
        dst_ref[pl.ds(r, chunk), :] = src_ref[pl.ds(r, chunk), :].astype(BF16)
        return carry
    lax.fori_loop(0, rows // chunk, body, 0)


def _dot(a, b):
    return jnp.dot(a, b, preferred_element_type=F32)


def _dot_nt(a, b):
    return lax.dot_general(a, b, (((1,), (1,)), ((), ())), preferred_element_type=F32)


def _rmsnorm_kernel(x_ref, g_ref, o_ref):
    o_ref[...] = _rms(x_ref[...], g_ref[...]).astype(o_ref.dtype)


def rmsnorm_bf16(x, g, tm=512):
    m, d = x.shape
    return pl.pallas_call(
        _rmsnorm_kernel,
        grid=(m // tm,),
        in_specs=[pl.BlockSpec((tm, d), lambda i: (i, 0)),
                  pl.BlockSpec((1, d), lambda i: (0, 0))],
        out_specs=pl.BlockSpec((tm, d), lambda i: (i, 0)),
        out_shape=jax.ShapeDtypeStruct((m, d), BF16),
        compiler_params=_params(1, 4 * tm * d * 6 + (8 << 20)),
        name="rmsnorm_bf16",
    )(x, g.reshape(1, d))


def _ws_plain_kernel(x_ref, w_ref, o_ref, wb_ref):
    @pl.when(pl.program_id(1) == 0)
    def _():
        _cast_rows(w_ref, wb_ref, w_ref.shape[0])
    o_ref[...] = _dot(x_ref[...], wb_ref[...]).astype(o_ref.dtype)


def _ws_bf16w_kernel(x_ref, w_ref, o_ref):
    o_ref[...] = _dot(x_ref[...], w_ref[...]).astype(o_ref.dtype)


def _ws_residual_kernel(x_ref, w_ref, r_ref, o_ref, wb_ref):
    @pl.when(pl.program_id(1) == 0)
    def _():
        _cast_rows(w_ref, wb_ref, w_ref.shape[0])
    o_ref[...] = r_ref[...] + _dot(x_ref[...], wb_ref[...])


def _ws_swiglu_kernel(x_ref, wg_ref, wu_ref, o_ref, wgb_ref, wub_ref):
    @pl.when(pl.program_id(1) == 0)
    def _():
        _cast_rows(wg_ref, wgb_ref, wg_ref.shape[0])
        _cast_rows(wu_ref, wub_ref, wu_ref.shape[0])
    x = x_ref[...]
    g = _dot(x, wgb_ref[...])
    u = _dot(x, wub_ref[...])
    o_ref[...] = (g * _sigmoid(g) * u).astype(o_ref.dtype)


def ws_matmul(x, w, layer, *, tm, tn, out_dtype, residual=None, name):
    m, k = x.shape
    n = w.shape[2]
    in_specs = [pl.BlockSpec((tm, k), lambda j, i: (i, 0)),
                pl.BlockSpec((None, k, tn), lambda j, i: (layer, 0, j))]
    args = [x, w]
    kern = _ws_plain_kernel
    scratch = [pltpu.VMEM((k, tn), BF16)]
    if residual is not None:
        in_specs.append(pl.BlockSpec((tm, tn), lambda j, i: (i, j)))
        args.append(residual)
        kern = _ws_residual_kernel
    elif w.dtype == BF16:
        kern, scratch = _ws_bf16w_kernel, []
    vmem = 2 * (tm * k * 2 + k * tn * 4 + 2 * tm * tn * 4) + k * tn * 2 + tm * tn * 4 + (6 << 20)
    return pl.pallas_call(
        kern,
        grid=(n // tn, m // tm),
        in_specs=in_specs,
        out_specs=pl.BlockSpec((tm, tn), lambda j, i: (i, j)),
        out_shape=jax.ShapeDtypeStruct((m, n), out_dtype),
        scratch_shapes=scratch,
        compiler_params=_params(2, vmem),
        name=name,
    )(*args)


def ws_swiglu(x, w_up, layer, d_ff, *, tm, tn, name):
    m, k = x.shape
    nb = d_ff // tn
    vmem = 2 * (tm * k * 2 + 2 * k * tn * 4 + tm * tn * 2) + 2 * k * tn * 2 + 3 * tm * tn * 4 + (6 << 20)
    return pl.pallas_call(
        _ws_swiglu_kernel,
        grid=(nb, m // tm),
        in_specs=[pl.BlockSpec((tm, k), lambda j, i: (i, 0)),
                  pl.BlockSpec((None, k, tn), lambda j, i: (layer, 0, j)),
                  pl.BlockSpec((None, k, tn), lambda j, i: (layer, 0, j + nb))],
        out_specs=pl.BlockSpec((tm, tn), lambda j, i: (i, j)),
        out_shape=jax.ShapeDtypeStruct((m, d_ff), BF16),
        scratch_shapes=[pltpu.VMEM((k, tn), BF16), pltpu.VMEM((k, tn), BF16)],
        compiler_params=_params(2, vmem),
        name=name,
    )(x, w_up, w_up)


NA_ROWS = SEQ // GRID_W
NA_NBLK = SEQ // NA_QBLK
NA_QROWS = NA_QBLK // GRID_W
NA_KROWS = NA_KWIN // GRID_W
assert NA_KROWS >= NA_QROWS + NA_WIN_R - 1 and (NA_ROWS - NA_KROWS) % 2 == 0 and NA_QROWS % 2 == 0


def _na_key_start_row(blk):
    lo, hi = 0, NA_ROWS - NA_KROWS
    start = NA_QROWS * blk - NA_WIN_R // 2
    return jnp.clip(start, lo, hi) if isinstance(blk, jax.Array) else min(max(start, lo), hi)


def _na_block_geometry(blk):
    geo = []
    for a in range(NA_QROWS):
        qr = NA_QROWS * blk + a
        rs = min(max(qr - NA_WIN_R // 2, 0), NA_ROWS - NA_WIN_R)
        for c in range(NA_KROWS):
            kr = _na_key_start_row(blk) + c
            geo.append(kr - qr + NA_WIN_R - 1 if rs <= kr < rs + NA_WIN_R else None)
    return tuple(geo)


NA_GEOMETRIES = tuple(dict.fromkeys(_na_block_geometry(b) for b in range(NA_NBLK)))
NA_BLOCK_PATTERN = tuple(NA_GEOMETRIES.index(_na_block_geometry(b)) for b in range(NA_NBLK))


def _na_pattern(i):
    res = NA_BLOCK_PATTERN[-1]
    for b in reversed(range(NA_NBLK - 1)):
        res = jnp.where(i == b, NA_BLOCK_PATTERN[b], res)
    return res


def _na_bias_table(rpb):
    hh = rpb.shape[0]
    span = GRID_W - 1
    left = span - (NA_WIN_C - 1)
    right = 2 * span + 1 - left - (2 * NA_WIN_C - 1)
    ext = jnp.pad(rpb, ((0, 0), (0, 0), (left, right)), constant_values=NEG)
    toep = jnp.stack([ext[:, :, span - qc:span - qc + GRID_W] for qc in range(GRID_W)], axis=2)
    qc = np.arange(GRID_W)
    cs = np.clip(qc - NA_WIN_C // 2, 0, GRID_W - NA_WIN_C)
    kc = np.arange(GRID_W)
    col_ok = (kc[None, :] >= cs[:, None]) & (kc[None, :] < cs[:, None] + NA_WIN_C)
    toep = jnp.where(col_ok[None, None], toep, NEG)
    neg = jnp.full((hh, GRID_W, GRID_W), NEG, rpb.dtype)
    patterns = []
    for geo in NA_GEOMETRIES:
        q_rows = []
        for a in range(NA_QROWS):
            offs = geo[a * NA_KROWS:(a + 1) * NA_KROWS]
            k_blocks = [neg if d is None else toep[:, d] for d in offs]
            q_rows.append(jnp.concatenate(k_blocks, axis=-1))
        patterns.append(jnp.concatenate(q_rows, axis=-2))
    return jnp.stack(patterns, axis=0)


def _na_kernel(q_ref, k_ref, v_ref, bias_ref, gq_ref, gk_ref, o_ref, kn_ref, vb_ref):
    i = pl.program_id(1)

    @pl.when(i == 0)
    def _():
        def body(c, carry):
            r = pl.multiple_of(c * 256, 256)
            for h in range(NA_HEADS):
                hs = slice(h * NA_HD, (h + 1) * NA_HD)
                kn_ref[pl.ds(r, 256), hs] = _rms(k_ref[pl.ds(r, 256), hs], gk_ref[...]).astype(BF16)
            vb_ref[pl.ds(r, 256), :] = v_ref[pl.ds(r, 256), :].astype(BF16)
            return carry
        lax.fori_loop(0, SEQ // 256, body, 0)

    start = pl.multiple_of(_na_key_start_row(i) * GRID_W, 2 * GRID_W)
    scale = NA_HD ** -0.5
    for h in range(NA_HEADS):
        hs = slice(h * NA_HD, (h + 1) * NA_HD)
        q = (_rms(q_ref[:, hs], gq_ref[...]) * scale).astype(BF16)
        s = _dot_nt(q, kn_ref[pl.ds(start, NA_KWIN), hs]) + bias_ref[0, h]
        m = jnp.max(s, axis=-1, keepdims=True)
        p = jnp.exp(s - m)
        l = jnp.sum(p, axis=-1, keepdims=True)
        o = _dot(p.astype(BF16), vb_ref[pl.ds(start, NA_KWIN), hs]) / l
        o_ref[:, hs] = o.astype(o_ref.dtype)


def na_attention(z, bias, gq, gk, batch):
    nblk = SEQ // NA_QBLK
    w = NA_HEADS * NA_HD
    vmem = (2 * (NA_QBLK * w * 4 + 2 * SEQ * w * 4 + NA_HEADS * NA_QBLK * NA_KWIN * 4 + NA_QBLK * w * 2)
            + 2 * SEQ * w * 2 + (8 << 20))
    return pl.pallas_call(
        _na_kernel,
        grid=(batch, nblk),
        in_specs=[pl.BlockSpec((NA_QBLK, w), lambda b, i: (b * nblk + i, COL_NA_Q // w)),
                  pl.BlockSpec((SEQ, w), lambda b, i: (b, COL_NA_K // w)),
                  pl.BlockSpec((SEQ, w), lambda b, i: (b, COL_NA_V // w)),
                  pl.BlockSpec((1, NA_HEADS, NA_QBLK, NA_KWIN), lambda b, i: (_na_pattern(i), 0, 0, 0)),
                  pl.BlockSpec((1, NA_HD), lambda b, i: (0, 0)),
                  pl.BlockSpec((1, NA_HD), lambda b, i: (0, 0))],
        out_specs=pl.BlockSpec((NA_QBLK, w), lambda b, i: (b * nblk + i, 0)),
        out_shape=jax.ShapeDtypeStruct((batch * SEQ, w), BF16),
        scratch_shapes=[pltpu.VMEM((SEQ, w), BF16), pltpu.VMEM((SEQ, w), BF16)],
        compiler_params=_params(2, vmem),
        name="na_attention",
    )(z, z, z, bias, gq.reshape(1, NA_HD), gk.reshape(1, NA_HD))


def _rope_tables(width, head_dim, first_lane):
    half = head_dim // 2
    freqs = ROPE_THETA ** (-2.0 * np.arange(half, dtype=np.float32) / head_dim)
    ang = jnp.arange(SEQ, dtype=F32)[:, None] * jnp.asarray(freqs, F32)[None, :]
    cos, sin = jnp.cos(ang), jnp.sin(ang)
    lane = np.arange(width)
    rel = lane - first_lane
    in_rope = (rel >= 0) & (rel < (width - first_lane if first_lane == 0 else head_dim))
    p = np.where(in_rope, rel % head_dim, 0)
    j = p % half
    first_half = in_rope & (p < half)
    second_half = in_rope & (p >= half)
    cos_t = jnp.where(in_rope[None, :], cos[:, j], 1.0)
    sin_a = jnp.where(first_half[None, :], -sin[:, j], 0.0)
    sin_b = jnp.where(second_half[None, :], sin[:, j], 0.0)
    return cos_t, sin_a, sin_b


def _apply_rope(x, cos_t, sin_a, sin_b, half):
    n = x.shape[-1]
    return (x * cos_t + pltpu.roll(x, n - half, axis=1) * sin_a
            + pltpu.roll(x, half, axis=1) * sin_b)


def _mla_prep_kernel(cq_ref, ckv_ref, kpe_ref, wq_ref, wkv_ref, gcq_ref, gckv_ref,
                     gqn_ref, gkn_ref, cos_ref, sa_ref, sb_ref, q_ref, k_ref, v_ref):
    half = MLA_ROPE // 2
    cos_t, sin_a, sin_b = cos_ref[...], sa_ref[...], sb_ref[...]
    scale = MLA_QK ** -0.5
    cq = _rms(cq_ref[...], gcq_ref[...]).astype(BF16)
    q_raw = _dot(cq, wq_ref[...])
    ckv = _rms(ckv_ref[...], gckv_ref[...]).astype(BF16)
    kv_raw = _dot(ckv, wkv_ref[...])
    kpe = kpe_ref[...]
    kpe_ss = jnp.sum(kpe * kpe, axis=-1, keepdims=True)
    for h in range(MLA_HEADS):
        cs = slice(h * MLA_PAD, (h + 1) * MLA_PAD)
        qc = q_raw[:, cs]
        ms = jnp.sum(qc * qc, axis=-1, keepdims=True) * (1.0 / MLA_QK)
        qn = qc * lax.rsqrt(ms + EPS) * gqn_ref[...]
        q_ref[:, cs] = (_apply_rope(qn, cos_t, sin_a, sin_b, half) * scale).astype(BF16)
        kn = kv_raw[:, h * MLA_PAD:h * MLA_PAD + MLA_NOPE]
        ms = (jnp.sum(kn * kn, axis=-1, keepdims=True) + kpe_ss) * (1.0 / MLA_QK)
        kc = jnp.concatenate([kn, kpe], axis=-1) * lax.rsqrt(ms + EPS) * gkn_ref[...]
        k_ref[:, cs] = _apply_rope(kc, cos_t, sin_a, sin_b, half).astype(BF16)
        v_ref[:, h * MLA_NOPE:(h + 1) * MLA_NOPE] = kv_raw[:, h * MLA_PAD + MLA_NOPE:(h + 1) * MLA_PAD].astype(BF16)


def mla_prep(z, wq_pad, wkv, gcq, gckv, gqn_pad, gkn_pad, tabs, tm=512):
    t = z.shape[0]
    sb = SEQ // tm
    qw = MLA_HEADS * MLA_PAD
    row = lambda i: (i, 0)
    const = lambda i: (0, 0)
    pos = lambda i: (i % sb, 0)
    return pl.pallas_call(
        _mla_prep_kernel,
        grid=(t // tm,),
        in_specs=[pl.BlockSpec((tm, MLA_Q_RANK), lambda i: (i, COL_CQ // MLA_Q_RANK)),
                  pl.BlockSpec((tm, MLA_KV_RANK), lambda i: (i, COL_CKV // MLA_KV_RANK)),
                  pl.BlockSpec((tm, LANES), lambda i: (i, COL_KPE // LANES)),
                  pl.BlockSpec((MLA_Q_RANK, qw), const),
                  pl.BlockSpec((MLA_KV_RANK, qw), const),
                  pl.BlockSpec((1, MLA_Q_RANK), const),
                  pl.BlockSpec((1, MLA_KV_RANK), const),
                  pl.BlockSpec((1, MLA_PAD), const),
                  pl.BlockSpec((1, MLA_PAD), const),
                  pl.BlockSpec((tm, MLA_PAD), pos),
                  pl.BlockSpec((tm, MLA_PAD), pos),
                  pl.BlockSpec((tm, MLA_PAD), pos)],
        out_specs=[pl.BlockSpec((tm, qw), row), pl.BlockSpec((tm, qw), row),
                   pl.BlockSpec((tm, MLA_HEADS * MLA_NOPE), row)],
        out_shape=[jax.ShapeDtypeStruct((t, qw), BF16), jax.ShapeDtypeStruct((t, qw), BF16),
                   jax.ShapeDtypeStruct((t, MLA_HEADS * MLA_NOPE), BF16)],
        compiler_params=_params(1, 40 << 20),
        name="mla_prep",
    )(z, z, z, wq_pad, wkv, gcq.reshape(1, -1), gckv.reshape(1, -1), gqn_pad, gkn_pad, *tabs)


def _mla_attn_kernel(q_ref, k_ref, v_ref, o_ref):
    for h in range(MLA_HEADS):
        cs = slice(h * MLA_PAD, (h + 1) * MLA_PAD)
        vs = slice(h * MLA_NOPE, (h + 1) * MLA_NOPE)
        s = _dot_nt(q_ref[:, cs], k_ref[:, cs])
        m = jnp.max(s, axis=-1, keepdims=True)
        p = jnp.exp(s - m)
        l = jnp.sum(p, axis=-1, keepdims=True)
        o = _dot(p.astype(BF16), v_ref[:, vs]) / l
        o_ref[:, vs] = o.astype(o_ref.dtype)


def mla_attention(q, k, v, batch, tq=256):
    nq = SEQ // tq
    qw = MLA_HEADS * MLA_PAD
    vw = MLA_HEADS * MLA_NOPE
    vmem = 2 * (tq * qw * 2 + SEQ * qw * 2 + SEQ * vw * 2 + tq * vw * 2) + 4 * tq * SEQ * 4 + (8 << 20)
    return pl.pallas_call(
        _mla_attn_kernel,
        grid=(batch, nq),
        in_specs=[pl.BlockSpec((tq, qw), lambda b, i: (b * nq + i, 0)),
                  pl.BlockSpec((SEQ, qw), lambda b, i: (b, 0)),
                  pl.BlockSpec((SEQ, vw), lambda b, i: (b, 0))],
        out_specs=pl.BlockSpec((tq, vw), lambda b, i: (b * nq + i, 0)),
        out_shape=jax.ShapeDtypeStruct((batch * SEQ, vw), BF16),
        compiler_params=_params(2, vmem),
        name="mla_attention",
    )(q, k, v)


def _rms_halves(x, g):
    lo = lax.broadcasted_iota(jnp.int32, x.shape, 1) < SWA_HD
    x2 = x * x
    s_lo = jnp.sum(jnp.where(lo, x2, 0.0), axis=-1, keepdims=True)
    s_hi = jnp.sum(jnp.where(lo, 0.0, x2), axis=-1, keepdims=True)
    ms = jnp.where(lo, s_lo, s_hi) * (1.0 / SWA_HD)
    return x * lax.rsqrt(ms + EPS) * g


def _swa_kernel(sink_ref, q_ref, k_ref, v_ref, gq_ref, gk_ref, cosq_ref, saq_ref, sbq_ref,
                cosk_ref, sak_ref, sbk_ref, o_ref, kk_ref, vv_ref):
    n = pl.program_id(1)
    nblk = SEQ // SWA_QBLK
    half = SWA_HD // 2

    @pl.when(n == 0)
    def _():
        zeros = jnp.zeros((SWA_BLOCK, LANES), BF16)
        for c in range(4):
            kk_ref[c, pl.ds(0, SWA_BLOCK), :] = zeros
            kk_ref[c, pl.ds(SEQ + SWA_BLOCK, SWA_BLOCK), :] = zeros
            vv_ref[c, pl.ds(0, SWA_BLOCK), :] = zeros
            vv_ref[c, pl.ds(SEQ + SWA_BLOCK, SWA_BLOCK), :] = zeros

        def body(c, carry):
            r = pl.multiple_of(c * 256, 256)
            dst = pl.ds(r + SWA_BLOCK, 256)
            lo = lax.broadcasted_iota(jnp.int32, (256, LANES), 1) < SWA_HD
            kr = _apply_rope(_rms_halves(k_ref[pl.ds(r, 256), :], gk_ref[...]),
                             cosk_ref[pl.ds(r, 256), :], sak_ref[pl.ds(r, 256), :],
                             sbk_ref[pl.ds(r, 256), :], half)
            ks = pltpu.roll(kr, SWA_HD, axis=1)
            kk_ref[0, dst, :] = jnp.where(lo, kr, 0.0).astype(BF16)
            kk_ref[1, dst, :] = jnp.where(lo, 0.0, ks).astype(BF16)
            kk_ref[2, dst, :] = jnp.where(lo, ks, 0.0).astype(BF16)
            kk_ref[3, dst, :] = jnp.where(lo, 0.0, kr).astype(BF16)
            vr = v_ref[pl.ds(r, 256), :]
            vs = pltpu.roll(vr, SWA_HD, axis=1)
            vv_ref[0, dst, :] = jnp.where(lo, vr, 0.0).astype(BF16)
            vv_ref[1, dst, :] = jnp.where(lo, 0.0, vs).astype(BF16)
            vv_ref[2, dst, :] = jnp.where(lo, vs, 0.0).astype(BF16)
            vv_ref[3, dst, :] = jnp.where(lo, 0.0, vr).astype(BF16)
            return carry
        lax.fori_loop(0, SEQ // 256, body, 0)

    band = pl.ds(pl.multiple_of(n * SWA_QBLK, SWA_QBLK), SWA_BAND)
    a = lax.broadcasted_iota(jnp.int32, (SWA_QBLK, SWA_BAND), 0)
    c = lax.broadcasted_iota(jnp.int32, (SWA_QBLK, SWA_BAND), 1)
    c_min = jnp.where(n == 0, SWA_BLOCK, 0)
    c_max = jnp.where(n == nblk - 1, SWA_BAND - SWA_BLOCK, SWA_BAND)
    valid = (c >= a) & (c <= a + 2 * SWA_BLOCK) & (c >= c_min) & (c < c_max)
    scale = SWA_HD ** -0.5
    for pair in range(SWA_HEADS // 2):
        ps = slice(pair * LANES, (pair + 1) * LANES)
        grp = pair // 2
        q = _apply_rope(_rms_halves(q_ref[:, ps], gq_ref[...]),
                        cosq_ref[...], saq_ref[...], sbq_ref[...], half)
        q = (q * scale).astype(BF16)
        acc = None
        for hf in range(2):
            sink = sink_ref[2 * pair + hf]
            s = _dot_nt(q, kk_ref[2 * grp + hf, band, :])
            s = jnp.where(valid, s, NEG)
            m = jnp.maximum(jnp.max(s, axis=-1, keepdims=True), sink)
            p = jnp.exp(s - m)
            den = jnp.sum(p, axis=-1, keepdims=True) + jnp.exp(sink - m)
            o = _dot(p.astype(BF16), vv_ref[2 * grp + hf, band, :]) / den
            acc = o if acc is None else acc + o
        o_ref[:, ps] = acc.astype(o_ref.dtype)


def swa_attention(z, sink, gq, gk, tabs, batch):
    nblk = SEQ // SWA_QBLK
    qw = SWA_HEADS * SWA_HD
    cos_t, sin_a, sin_b = tabs
    g2 = lambda g: jnp.concatenate([g, g]).reshape(1, LANES)
    blk = lambda b, n: (n, 0)
    full = lambda b, n: (0, 0)
    tab_blk = pl.BlockSpec((SWA_QBLK, LANES), blk)
    tab_full = pl.BlockSpec((SEQ, LANES), full)
    vmem = (2 * (SWA_QBLK * qw * 6 + 2 * SEQ * LANES * 4 + 3 * SEQ * LANES * 4 + 3 * SWA_QBLK * LANES * 4)
            + 8 * (SEQ + 2 * SWA_BLOCK) * LANES * 2 + 8 * SWA_QBLK * SWA_BAND * 4 + (8 << 20))
    return pl.pallas_call(
        _swa_kernel,
        grid=(batch, nblk),
        in_specs=[pl.BlockSpec(memory_space=pltpu.SMEM),
                  pl.BlockSpec((SWA_QBLK, qw), lambda b, n: (b * nblk + n, COL_SWQ // qw)),
                  pl.BlockSpec((SEQ, LANES), lambda b, n: (b, COL_SWK // LANES)),
                  pl.BlockSpec((SEQ, LANES), lambda b, n: (b, COL_SWV // LANES)),
                  pl.BlockSpec((1, LANES), full), pl.BlockSpec((1, LANES), full),
                  tab_blk, tab_blk, tab_blk, tab_full, tab_full, tab_full],
        out_specs=pl.BlockSpec((SWA_QBLK, qw), lambda b, n: (b * nblk + n, 0)),
        out_shape=jax.ShapeDtypeStruct((batch * SEQ, qw), BF16),
        scratch_shapes=[pltpu.VMEM((4, SEQ + 2 * SWA_BLOCK, LANES), BF16),
                        pltpu.VMEM((4, SEQ + 2 * SWA_BLOCK, LANES), BF16)],
        compiler_params=_params(2, vmem),
        name="swa_attention",
    )(sink, z, z, z, g2(gq), g2(gk), cos_t, sin_a, sin_b, cos_t, sin_a, sin_b)


def _mem_attn_kernel(q_ref, k_ref, v_ref, gq_ref, gk_ref, o_ref):
    scale = MEM_HD ** -0.5
    for h in range(MEM_HEADS):
        hs = slice(h * MEM_HD, (h + 1) * MEM_HD)
        q = (_rms(q_ref[:, hs], gq_ref[...]) * scale).astype(BF16)
        k = _rms(k_ref[:, hs], gk_ref[...]).astype(BF16)
        s = _dot_nt(q, k)
        m = jnp.max(s, axis=-1, keepdims=True)
        p = jnp.exp(s - m)
        l = jnp.sum(p, axis=-1, keepdims=True)
        o = _dot(p.astype(BF16), v_ref[:, hs].astype(BF16)) / l
        o_ref[:, hs] = o.astype(o_ref.dtype)


def mem_attention(z, memkv, gq, gk, batch, tq=512):
    nq = SEQ // tq
    w = MEM_HEADS * MEM_HD
    return pl.pallas_call(
        _mem_attn_kernel,
        grid=(batch, nq),
        in_specs=[pl.BlockSpec((tq, w), lambda b, i: (b * nq + i, COL_MEQ // w)),
                  pl.BlockSpec((MEM_LEN, w), lambda b, i: (b, 0)),
                  pl.BlockSpec((MEM_LEN, w), lambda b, i: (b, 1)),
                  pl.BlockSpec((1, MEM_HD), lambda b, i: (0, 0)),
                  pl.BlockSpec((1, MEM_HD), lambda b, i: (0, 0))],
        out_specs=pl.BlockSpec((tq, w), lambda b, i: (b * nq + i, 0)),
        out_shape=jax.ShapeDtypeStruct((batch * SEQ, w), BF16),
        compiler_params=_params(2, 24 << 20),
        name="mem_attention",
    )(z, memkv, memkv, gq.reshape(1, MEM_HD), gk.reshape(1, MEM_HD))


def _merge_kernel(layer, u_ref, b0_ref, b1_ref, b2_ref, b3_ref, wg_hbm, wb_ref, bg_ref, o_ref,
                  stage_ref, wgs_ref, wbs_ref, sem):
    branches = (b0_ref, b1_ref, b2_ref, b3_ref)
    j, i = pl.program_id(0), pl.program_id(1)
    tn = wgs_ref.shape[2]
    slot = j % 2

    def copies(jj, s):
        return [pltpu.make_async_copy(wg_hbm.at[layer, :, n, pl.ds(pl.multiple_of(jj * tn, tn), tn)],
                                      stage_ref.at[s, n], sem.at[s]) for n in range(4)]

    @pl.when(jnp.logical_and(j == 0, i == 0))
    def _():
        for c in copies(0, 0):
            c.start()

    @pl.when(i == 0)
    def _():
        for c in copies(j, slot):
            c.wait()

        @pl.when(j + 1 < pl.num_programs(0))
        def _():
            for c in copies(j + 1, 1 - slot):
                c.start()
        for n in range(4):
            _cast_rows(stage_ref.at[slot, n], wgs_ref.at[n], D_MODEL)
            wbs_ref[n] = wb_ref[n].astype(BF16)

    u = u_ref[...]
    acc = None
    for n in range(4):
        gate = _sigmoid(_dot(u, wgs_ref[n]) + bg_ref[n:n + 1, :])
        term = gate * _dot(branches[n][...], wbs_ref[n])
        acc = term if acc is None else acc + term
    o_ref[...] = acc.astype(o_ref.dtype)


def gated_merge(u, branches, w_gate, w_branch, b_gate, layer, tm=1024, tn=256):
    t = u.shape[0]
    bw = branches[0].shape[1]
    vmem = (2 * (tm * D_MODEL * 2 + 4 * tm * bw * 2 + 4 * D_MODEL * tn * 4 + 4 * bw * tn * 4 + tm * tn * 2)
            + 4 * D_MODEL * tn * 2 + 4 * bw * tn * 2 + 4 * tm * tn * 4 + (6 << 20))
    return pl.pallas_call(
        functools.partial(_merge_kernel, layer),
        grid=(D_MODEL // tn, t // tm),
        in_specs=[pl.BlockSpec((tm, D_MODEL), lambda j, i: (i, 0))]
        + [pl.BlockSpec((tm, bw), lambda j, i: (i, 0)) for _ in range(4)]
        + [pl.BlockSpec(memory_space=pl.ANY),
           pl.BlockSpec((None, 4, bw, tn), lambda j, i: (layer, 0, 0, j)),
           pl.BlockSpec((None, 4, tn), lambda j, i: (layer, 0, j))],
        out_specs=pl.BlockSpec((tm, tn), lambda j, i: (i, j)),
        out_shape=jax.ShapeDtypeStruct((t, D_MODEL), BF16),
        scratch_shapes=[pltpu.VMEM((2, 4, D_MODEL, tn), F32), pltpu.VMEM((4, D_MODEL, tn), BF16),
                        pltpu.VMEM((4, bw, tn), BF16), pltpu.SemaphoreType.DMA((2,))],
        compiler_params=_params(2, vmem),
        name="gated_merge",
    )(u, *branches, w_gate, w_branch, b_gate)


def _router_kernel(h_ref, g_ref, wr_ref, ri_ref, rw_ref, cnt_ref, carry_ref):
    @pl.when(pl.program_id(0) == 0)
    def _():
        carry_ref[...] = jnp.zeros_like(carry_ref)

    tm = h_ref.shape[0]
    hn = _rms(h_ref[...], g_ref[...])
    logits = jnp.dot(hn, wr_ref[...], precision=lax.Precision.HIGHEST, preferred_element_type=F32)
    lane = lax.broadcasted_iota(jnp.int32, (tm, LANES), 1)
    lane_f = lane.astype(F32)
    logits = jnp.where(lane < N_EXPERTS, logits, -jnp.inf)
    m1 = jnp.max(logits, axis=-1, keepdims=True)
    i1 = jnp.min(jnp.where(logits == m1, lane_f, float(LANES)), axis=-1, keepdims=True)
    oh1 = lane_f == i1
    rest = jnp.where(oh1, -jnp.inf, logits)
    m2 = jnp.max(rest, axis=-1, keepdims=True)
    i2 = jnp.min(jnp.where(rest == m2, lane_f, float(LANES)), axis=-1, keepdims=True)
    oh2 = lane_f == i2
    e2 = jnp.exp(m2 - m1)
    w1 = 1.0 / (1.0 + e2)
    w2 = e2 / (1.0 + e2)
    chosen = jnp.where(oh1 | oh2, 1.0, 0.0)
    before = (lax.broadcasted_iota(jnp.int32, (tm, tm), 1)
              < lax.broadcasted_iota(jnp.int32, (tm, tm), 0))
    prefix = _dot(jnp.where(before, 1.0, 0.0).astype(BF16), chosen.astype(BF16)) + carry_ref[...]
    r1 = jnp.sum(jnp.where(oh1, prefix, 0.0), axis=-1, keepdims=True).astype(jnp.int32)
    r2 = jnp.sum(jnp.where(oh2, prefix, 0.0), axis=-1, keepdims=True).astype(jnp.int32)
    carry_ref[...] += jnp.sum(chosen, axis=0, keepdims=True)
    ri_ref[...] = jnp.where(lane == 0, i1.astype(jnp.int32),
                            jnp.where(lane == 1, i2.astype(jnp.int32),
                                      jnp.where(lane == 2, r1, jnp.where(lane == 3, r2, 0))))
    rw_ref[...] = jnp.where(lane == 0, w1, jnp.where(lane == 1, w2, 0.0))
    cnt_ref[...] = carry_ref[...]


def moe_router(h, g, w_router, tm=512):
    t, d = h.shape
    wr = jnp.pad(w_router, ((0, 0), (0, LANES - N_EXPERTS)))
    return pl.pallas_call(
        _router_kernel,
        grid=(t // tm,),
        in_specs=[pl.BlockSpec((tm, d), lambda i: (i, 0)),
                  pl.BlockSpec((1, d), lambda i: (0, 0)),
                  pl.BlockSpec((d, LANES), lambda i: (0, 0))],
        out_specs=[pl.BlockSpec((tm, LANES), lambda i: (i, 0)),
                   pl.BlockSpec((tm, LANES), lambda i: (i, 0)),
                   pl.BlockSpec((1, LANES), lambda i: (0, 0))],
        out_shape=[jax.ShapeDtypeStruct((t, LANES), jnp.int32),
                   jax.ShapeDtypeStruct((t, LANES), F32),
                   jax.ShapeDtypeStruct((1, LANES), F32)],
        scratch_shapes=[pltpu.VMEM((1, LANES), F32)],
        compiler_params=_params(1, 32 << 20),
        name="moe_router",
    )(h, g.reshape(1, d), wr)


ROW_DMA_UNROLL = 8


def _row_copy(src_hbm, row, dst_vmem, r, sem):
    return pltpu.make_async_copy(src_hbm.at[pl.ds(row, 1), :], dst_vmem.at[pl.ds(r, 1), :], sem)


def _moe_scatter_kernel(pos_ref, h_ref, g_ref, xs_in_hbm, xs_hbm, pk_ref, sem):
    del xs_in_hbm
    tg = h_ref.shape[0]
    i = pl.program_id(0)
    last = pl.num_programs(0) - 1
    slot = i % 2

    def row_copy(s, r, dst_row):
        return pltpu.make_async_copy(pk_ref.at[s, pl.ds(r, 1), :], xs_hbm.at[pl.ds(dst_row, 1), :], sem.at[s])

    def wait_slot(s):
        def drain(r, carry):
            row_copy(s, r, 0).wait()
            row_copy(s, r, 0).wait()
            return carry
        lax.fori_loop(0, tg, drain, 0, unroll=8)

    @pl.when(i >= 2)
    def _():
        wait_slot(slot)

    pk_ref[slot] = _rms(h_ref[...], g_ref[...])

    def issue(grp, carry):
        for k in range(ROW_DMA_UNROLL):
            r = grp * ROW_DMA_UNROLL + k
            t = i * tg + r
            row_copy(slot, r, pos_ref[2 * t]).start(priority=0)
            row_copy(slot, r, pos_ref[2 * t + 1]).start(priority=1)
        return carry
    lax.fori_loop(0, tg // ROW_DMA_UNROLL, issue, 0)

    @pl.when(jnp.logical_and(i == last, i >= 1))
    def _():
        wait_slot(1 - slot)

    @pl.when(i == last)
    def _():
        wait_slot(slot)


def _cast_kernel(x_ref, o_ref):
    o_ref[...] = x_ref[...].astype(o_ref.dtype)


def cast_bf16(x, tm=512):
    m, d = x.shape
    return pl.pallas_call(
        _cast_kernel,
        grid=(m // tm,),
        in_specs=[pl.BlockSpec((tm, d), lambda i: (i, 0))],
        out_specs=pl.BlockSpec((tm, d), lambda i: (i, 0)),
        out_shape=jax.ShapeDtypeStruct((m, d), BF16),
        compiler_params=_params(1, 32 << 20),
        name="cast_bf16",
    )(x)


def moe_scatter(pos_flat, h, g, rows, tg=256):
    t, d = h.shape
    xs0 = jnp.zeros((rows, d), F32)
    return pl.pallas_call(
        _moe_scatter_kernel,
        grid_spec=pltpu.PrefetchScalarGridSpec(
            num_scalar_prefetch=1,
            grid=(t // tg,),
            in_specs=[pl.BlockSpec((tg, d), lambda i, p: (i, 0)),
                      pl.BlockSpec((1, d), lambda i, p: (0, 0)),
                      pl.BlockSpec(memory_space=pl.ANY)],
            out_specs=pl.BlockSpec(memory_space=pl.ANY),
            scratch_shapes=[pltpu.VMEM((2, tg, d), F32), pltpu.SemaphoreType.DMA((2,))]),
        out_shape=jax.ShapeDtypeStruct((rows, d), F32),
        input_output_aliases={3: 0},
        compiler_params=_params(1, 24 << 20),
        name="moe_scatter",
    )(pos_flat, h, g.reshape(1, d), xs0)


def _stream_expert_weights(te_ref, first_ref, nxt_ref, meta_ref, cnt_ref, make_copies, cast_slot):
    j, i = pl.program_id(0), pl.program_id(1)

    @pl.when(jnp.logical_and(j == 0, i == 0))
    def _():
        cnt_ref[0] = 0
        for c in make_copies(0, te_ref[0], 0):
            c.start()

    @pl.when(first_ref[i] == 1)
    def _():
        slot = cnt_ref[0] % 2
        for c in make_copies(j, te_ref[i], slot):
            c.wait()
        wraps = nxt_ref[i] < 0
        j_next = jnp.where(wraps, j + 1, j)
        e_next = jnp.where(wraps, meta_ref[1], nxt_ref[i])

        @pl.when(j_next < pl.num_programs(0))
        def _():
            for c in make_copies(j_next, e_next, 1 - slot):
                c.start()
        cast_slot(slot)
        cnt_ref[0] = cnt_ref[0] + 1


def _moe_up_kernel(te_ref, first_ref, nxt_ref, meta_ref, x_ref, w_hbm, o_ref,
                   stage_ref, wgb_ref, wub_ref, sem, cnt_ref):
    i = pl.program_id(1)
    active = i < meta_ref[0]
    tf = wgb_ref.shape[1]
    nf = pl.num_programs(0)

    def make_copies(jj, e, slot):
        return [pltpu.make_async_copy(w_hbm.at[e, :, pl.ds(pl.multiple_of((jj + half * nf) * tf, tf), tf)],
                                      stage_ref.at[slot, half], sem.at[slot]) for half in range(2)]

    def cast_slot(slot):
        _cast_rows(stage_ref.at[slot, 0], wgb_ref, D_MODEL)
        _cast_rows(stage_ref.at[slot, 1], wub_ref, D_MODEL)

    _stream_expert_weights(te_ref, first_ref, nxt_ref, meta_ref, cnt_ref, make_copies, cast_slot)

    @pl.when(active)
    def _():
        x = x_ref[...]
        g = _dot(x, wgb_ref[...])
        u = _dot(x, wub_ref[...])
        o_ref[...] = (g * _sigmoid(g) * u).astype(o_ref.dtype)

    @pl.when(jnp.logical_not(active))
    def _():
        o_ref[...] = jnp.zeros_like(o_ref)


def _active_row(j, i, te, first, nxt, meta):
    return jnp.maximum(jnp.minimum(i, meta[0] - 1), 0)


def moe_up(xs, w_up, sched, tf=512):
    rows, d = xs.shape
    nf = D_FF_EXPERT // tf
    vmem = (2 * (MOE_TM * d * 2 + MOE_TM * tf * 2) + 4 * d * tf * 4 + 2 * d * tf * 2
            + 3 * MOE_TM * tf * 4 + 2 * MOE_TM * d * 2 + (6 << 20))
    return pl.pallas_call(
        _moe_up_kernel,
        grid_spec=pltpu.PrefetchScalarGridSpec(
            num_scalar_prefetch=4,
            grid=(nf, rows // MOE_TM),
            in_specs=[pl.BlockSpec((MOE_TM, d), lambda j, i, *s: (_active_row(j, i, *s), 0)),
                      pl.BlockSpec(memory_space=pl.ANY)],
            out_specs=pl.BlockSpec((MOE_TM, tf), lambda j, i, *s: (i, j)),
            scratch_shapes=[pltpu.VMEM((2, 2, d, tf), F32), pltpu.VMEM((d, tf), BF16),
                            pltpu.VMEM((d, tf), BF16), pltpu.SemaphoreType.DMA((2,)),
                            pltpu.SMEM((1,), jnp.int32)]),
        out_shape=jax.ShapeDtypeStruct((rows, D_FF_EXPERT), BF16),
        compiler_params=_params(2, vmem),
        name="moe_up",
    )(*sched, xs, w_up)


def _moe_down_kernel(k_half, te_ref, first_ref, nxt_ref, meta_ref, a_ref, w_hbm, *rest):
    if len(rest) == 6:
        part_ref, o_ref, stage_ref, wb_ref, sem, cnt_ref = rest
    else:
        part_ref, (o_ref, stage_ref, wb_ref, sem, cnt_ref) = None, rest
    i = pl.program_id(1)
    active = i < meta_ref[0]
    fk, tn = wb_ref.shape

    def make_copies(jj, e, slot):
        return [pltpu.make_async_copy(
            w_hbm.at[e, pl.ds(k_half * fk, fk), pl.ds(pl.multiple_of(jj * tn, tn), tn)],
            stage_ref.at[slot], sem.at[slot])]

    def cast_slot(slot):
        _cast_rows(stage_ref.at[slot], wb_ref, fk)

    _stream_expert_weights(te_ref, first_ref, nxt_ref, meta_ref, cnt_ref, make_copies, cast_slot)

    @pl.when(active)
    def _():
        y = _dot(a_ref[...], wb_ref[...])
        o_ref[...] = y if part_ref is None else part_ref[...] + y

    @pl.when(jnp.logical_not(active))
    def _():
        o_ref[...] = jnp.zeros_like(o_ref)


def moe_down(act, w_down, sched, k_half, partial=None, tn=512):
    rows, f = act.shape
    fk = f // 2
    in_specs = [pl.BlockSpec((MOE_TM, fk), lambda j, i, *s: (_active_row(j, i, *s), k_half)),
                pl.BlockSpec(memory_space=pl.ANY)]
    args = [*sched, act, w_down]
    if partial is not None:
        in_specs.append(pl.BlockSpec((MOE_TM, tn), lambda j, i, *s: (i, j)))
        args.append(partial)
    vmem = 2 * (MOE_TM * fk * 2 + fk * tn * 4 + 2 * MOE_TM * tn * 4) + fk * tn * 2 + MOE_TM * tn * 4 + (6 << 20)
    return pl.pallas_call(
        functools.partial(_moe_down_kernel, k_half),
        grid_spec=pltpu.PrefetchScalarGridSpec(
            num_scalar_prefetch=4,
            grid=(D_MODEL // tn, rows // MOE_TM),
            in_specs=in_specs,
            out_specs=pl.BlockSpec((MOE_TM, tn), lambda j, i, *s: (i, j)),
            scratch_shapes=[pltpu.VMEM((2, fk, tn), F32), pltpu.VMEM((fk, tn), BF16),
                            pltpu.SemaphoreType.DMA((2,)), pltpu.SMEM((1,), jnp.int32)]),
        out_shape=jax.ShapeDtypeStruct((rows, D_MODEL), F32),
        compiler_params=_params(2, vmem),
        name="moe_down_%d" % k_half,
    )(*args)


def _moe_combine_kernel(pos_ref, h_ref, rw_ref, y_hbm, o_ref, b1_ref, b2_ref, sem):
    tc = b1_ref.shape[1]
    i = pl.program_id(0)
    slot = i % 2

    def start_rows(step, dst_slot):
        def issue(grp, carry):
            for k in range(ROW_DMA_UNROLL):
                r = grp * ROW_DMA_UNROLL + k
                t = step * tc + r
                _row_copy(y_hbm, pos_ref[2 * t], b1_ref.at[dst_slot], r, sem.at[dst_slot]).start(priority=0)
                _row_copy(y_hbm, pos_ref[2 * t + 1], b2_ref.at[dst_slot], r, sem.at[dst_slot]).start(priority=1)
            return carry
        lax.fori_loop(0, tc // ROW_DMA_UNROLL, issue, 0)

    @pl.when(i == 0)
    def _():
        start_rows(0, 0)

    @pl.when(i + 1 < pl.num_programs(0))
    def _():
        start_rows(i + 1, 1 - slot)

    def drain(r, carry):
        _row_copy(y_hbm, 0, b1_ref.at[slot], r, sem.at[slot]).wait()
        _row_copy(y_hbm, 0, b2_ref.at[slot], r, sem.at[slot]).wait()
        return carry
    lax.fori_loop(0, tc, drain, 0, unroll=8)
    o_ref[...] = h_ref[...] + rw_ref[:, 0:1] * b1_ref[slot] + rw_ref[:, 1:2] * b2_ref[slot]


def moe_combine(pos_flat, h, rw, y, tc=256):
    t, d = h.shape
    return pl.pallas_call(
        _moe_combine_kernel,
        grid_spec=pltpu.PrefetchScalarGridSpec(
            num_scalar_prefetch=1,
            grid=(t // tc,),
            in_specs=[pl.BlockSpec((tc, d), lambda i, p: (i, 0)),
                      pl.BlockSpec((tc, LANES), lambda i, p: (i, 0)),
                      pl.BlockSpec(memory_space=pl.ANY)],
            out_specs=pl.BlockSpec((tc, d), lambda i, p: (i, 0)),
            scratch_shapes=[pltpu.VMEM((2, tc, d), F32), pltpu.VMEM((2, tc, d), F32),
                            pltpu.SemaphoreType.DMA((2,))]),
        out_shape=jax.ShapeDtypeStruct((t, d), F32),
        compiler_params=_params(1, 32 << 20),
        name="moe_combine",
    )(pos_flat, h, rw, y)


def moe_block(h, g, w_router, w_up, w_down):
    t = h.shape[0]
    rows = 2 * t + N_EXPERTS * MOE_TM
    n_row_tiles = rows // MOE_TM
    ri, rw, cnt = moe_router(h, g, w_router)
    counts = cnt[0, :N_EXPERTS].astype(jnp.int32)
    padded = ((counts + MOE_TM - 1) // MOE_TM) * MOE_TM
    ends = jnp.cumsum(padded)
    starts = ends - padded
    pos = starts[ri[:, 0:2]] + ri[:, 2:4]
    pos_flat = pos.reshape(-1).astype(jnp.int32)
    n_tiles = (ends[-1] // MOE_TM).astype(jnp.int32)
    tile_first_row = jnp.arange(n_row_tiles, dtype=jnp.int32) * MOE_TM
    tile_e = jnp.sum((tile_first_row[:, None] >= ends[None, :]).astype(jnp.int32), axis=1)
    tile_e = jnp.minimum(tile_e, N_EXPERTS - 1)
    tile_idx = jnp.arange(n_row_tiles, dtype=jnp.int32)
    is_active = tile_idx < n_tiles
    tile_e = jnp.where(is_active, tile_e, tile_e[n_tiles - 1]).astype(jnp.int32)
    first = (is_active & ((tile_idx == 0) | (tile_e != jnp.roll(tile_e, 1)))).astype(jnp.int32)
    eidx = jnp.arange(N_EXPERTS, dtype=jnp.int32)
    later = (eidx[None, :] > eidx[:, None]) & (padded > 0)[None, :]
    nxt = jnp.min(jnp.where(later, eidx[None, :], N_EXPERTS), axis=1)
    nxt = jnp.where(nxt == N_EXPERTS, -1, nxt).astype(jnp.int32)
    sched = (tile_e, first, nxt[tile_e], jnp.stack([n_tiles, tile_e[0]]).astype(jnp.int32))

    xs = cast_bf16(moe_scatter(pos_flat, h, g, rows))
    act = moe_up(xs, w_up, sched)
    y = moe_down(act, w_down, sched, 0)
    y = moe_down(act, w_down, sched, 1, partial=y)
    return moe_combine(pos_flat, h, rw, y)


IN_PROJ_MOVES = ((0, 2560, 0), (2560, 64, COL_KPE), (2624, 512, COL_SWQ), (3136, 128, COL_SWK),
                 (3264, 128, COL_SWV), (3392, 512, COL_MEQ))


def _in_proj_layout_kernel(w_ref, o_ref):
    for src, width, dst in IN_PROJ_MOVES:
        o_ref[:, dst:dst + width] = w_ref[:, src:src + width].astype(o_ref.dtype)
    tail = COL_KPE + MLA_ROPE
    o_ref[:, tail:] = jnp.zeros((o_ref.shape[0], IN_COLS_PAD - tail), o_ref.dtype)


def _pad_in_proj(w_in, tk=256):
    nl, d, n = w_in.shape
    return pl.pallas_call(
        _in_proj_layout_kernel,
        grid=(nl, d // tk),
        in_specs=[pl.BlockSpec((None, tk, n), lambda l, i: (l, i, 0))],
        out_specs=pl.BlockSpec((None, tk, IN_COLS_PAD), lambda l, i: (l, i, 0)),
        out_shape=jax.ShapeDtypeStruct((nl, d, IN_COLS_PAD), BF16),
        compiler_params=_params(2, 32 << 20),
        name="in_proj_layout",
    )(w_in)


def _pad_heads(x, n_heads, width):
    lead = x.shape[:-1]
    x = x.reshape(lead + (n_heads, width))
    x = jnp.pad(x, [(0, 0)] * len(lead) + [(0, 0), (0, MLA_PAD - width)])
    return x.reshape(lead + (n_heads * MLA_PAD,))


def kernel(x, mem, norm_mix, w_in, na_q_norm, na_k_norm, na_rpb, mla_cq_norm, mla_w_uq, mla_ckv_norm, mla_w_ukv, mla_q_norm, mla_k_norm, swa_q_norm, swa_k_norm, swa_sink, mem_norm, mem_w_kv, mem_q_norm, mem_k_norm, w_branch, w_gate, b_gate, w_o, norm_ffn, ffn_w_up, ffn_w_down, moe_router, moe_w_up, moe_w_down):
    batch, seq, d = x.shape
    assert (seq, d) == (SEQ, D_MODEL) and mem.shape[1] == MEM_LEN
    t = batch * seq
    depth = w_in.shape[0]
    h = x.reshape(t, d)
    mem2 = mem.reshape(batch * MEM_LEN, d)
    mla_tabs = _rope_tables(MLA_PAD, MLA_ROPE, MLA_NOPE)
    swa_tabs = _rope_tables(LANES, SWA_HD, 0)
    w_in_pad = _pad_in_proj(w_in)
    wq_pad = _pad_heads(mla_w_uq, MLA_HEADS, MLA_QK).astype(BF16)
    wkv_bf = mla_w_ukv.astype(BF16)

    for l in range(depth):
        u = rmsnorm_bf16(h, norm_mix[l])
        z = ws_matmul(u, w_in_pad, l, tm=1024, tn=512, out_dtype=F32, name="in_proj")

        o_na = na_attention(z, _na_bias_table(na_rpb[l]), na_q_norm[l], na_k_norm[l], batch)

        q_mla, k_mla, v_mla = mla_prep(
            z, wq_pad[l], wkv_bf[l], mla_cq_norm[l], mla_ckv_norm[l],
            jnp.pad(mla_q_norm[l], (0, MLA_PAD - MLA_QK)).reshape(1, MLA_PAD),
            jnp.pad(mla_k_norm[l], (0, MLA_PAD - MLA_QK)).reshape(1, MLA_PAD), mla_tabs)
        o_mla = mla_attention(q_mla, k_mla, v_mla, batch)

        o_swa = swa_attention(z, swa_sink[l], swa_q_norm[l], swa_k_norm[l], swa_tabs, batch)

        memn = rmsnorm_bf16(mem2, mem_norm[l])
        memkv = ws_matmul(memn, mem_w_kv, l, tm=1024, tn=512, out_dtype=F32, name="mem_kv")
        o_mem = mem_attention(z, memkv, mem_q_norm[l], mem_k_norm[l], batch)

        merged = gated_merge(u, (o_na, o_mla, o_swa, o_mem), w_gate, w_branch, b_gate, l)
        h = ws_matmul(merged, w_o, l, tm=512, tn=1024, out_dtype=F32, residual=h, name="out_proj")

        if l % 2 == 0:
            hn = rmsnorm_bf16(h, norm_ffn[l])
            act = ws_swiglu(hn, ffn_w_up, l // 2, D_FF, tm=1024, tn=512, name="ffn_up")
            h = ws_matmul(act, ffn_w_down, l // 2, tm=512, tn=512, out_dtype=F32, residual=h, name="ffn_down")
        else:
            h = moe_block(h, norm_ffn[l], moe_router[l // 2], moe_w_up[l // 2], moe_w_down[l // 2])
    return h.reshape(batch, seq, d)
```

```python
import functools

import jax
import jax.numpy as jnp
import numpy as np
from jax import lax
from jax.experimental import pallas as pl
from jax.experimental.pallas import tpu as pltpu

F32 = jnp.float32
BF16 = jnp.bfloat16

D_MODEL = 2048
SEQ = 2048
MEM_LEN = 256
GRID_W = 64
ROPE_THETA = 10000.0
EPS = 1e-6
NEG = -1e30

NA_HEADS = 4
NA_HD = 128
NA_WIN_R = 8
NA_WIN_C = 16
NA_QBLK = 256
NA_KWIN = 768

MLA_HEADS = 4
MLA_NOPE = 128
MLA_ROPE = 64
MLA_QK = MLA_NOPE + MLA_ROPE
MLA_PAD = 256
MLA_Q_RANK = 768
MLA_KV_RANK = 256

SWA_HEADS = 8
SWA_KV_HEADS = 2
SWA_HD = 64
SWA_BLOCK = 128
SWA_QBLK = 256
SWA_BAND = SWA_QBLK + 2 * SWA_BLOCK

MEM_HEADS = 4
MEM_HD = 128

D_FF = 5632
N_EXPERTS = 8
D_FF_EXPERT = 7168
MOE_TM = 512

LANES = 128
VMEM_CAP = 60000 * 1024

COL_NA_Q, COL_NA_K, COL_NA_V = 0, 512, 1024
COL_CQ, COL_CKV = 1536, 2304
COL_SWQ, COL_MEQ, COL_SWK, COL_SWV, COL_KPE = 2560, 3072, 3584, 3712, 3840
IN_COLS_PAD = 4096


def _params(n_axes, vmem_bytes):
    return pltpu.CompilerParams(
        dimension_semantics=("arbitrary",) * n_axes,
        vmem_limit_bytes=int(min(VMEM_CAP, vmem_bytes)))


def _rms(x, g):
    ms = jnp.mean(x * x, axis=-1, keepdims=True)
    return x * lax.rsqrt(ms + EPS) * g


def _sigmoid(x):
    return 1.0 / (1.0 + jnp.exp(-x))


def _cast_rows(src_ref, dst_ref, rows, chunk=256):
    def body(c, carry):
        r = pl.multiple_of(c * chunk, chunk)
        dst_ref[pl.ds(r, chunk), :] = src_ref[pl.ds(r, chunk), :].astype(BF16)
        return carry
    lax.fori_loop(0, rows // chunk, body, 0)


def _dot(a, b):
    return jnp.dot(a, b, preferred_element_type=F32)


def _dot_nt(a, b):
    return lax.dot_general(a, b, (((1,), (1,)), ((), ())), preferred_element_type=F32)


def _rmsnorm_kernel(x_ref, g_ref, o_ref):
    o_ref[...] = _rms(x_ref[...], g_ref[...]).astype(o_ref.dtype)


def rmsnorm_bf16(x, g, tm=512):
    m, d = x.shape
    return pl.pallas_call(
        _rmsnorm_kernel,
        grid=(m // tm,),
        in_specs=[pl.BlockSpec((tm, d), lambda i: (i, 0)),
                  pl.BlockSpec((1, d), lambda i: (0, 0))],
        out_specs=pl.BlockSpec((tm, d), lambda i: (i, 0)),
        out_shape=jax.ShapeDtypeStruct((m, d), BF16),
        compiler_params=_params(1, 4 * tm * d * 6 + (8 << 20)),
        name="rmsnorm_bf16",
    )(x, g.reshape(1, d))


def _ws_plain_kernel(x_ref, w_ref, o_ref, wb_ref):
    @pl.when(pl.program_id(1) == 0)
    def _():
        _cast_rows(w_ref, wb_ref, w_ref.shape[0])
    o_ref[...] = _dot(x_ref[...], wb_ref[...]).astype(o_ref.dtype)


def _ws_bf16w_kernel(x_ref, w_ref, o_ref):
    o_ref[...] = _dot(x_ref[...], w_ref[...]).astype(o_ref.dtype)


def _ws_residual_kernel(x_ref, w_ref, r_ref, o_ref, wb_ref):
    @pl.when(pl.program_id(1) == 0)
    def _():
        _cast_rows(w_ref, wb_ref, w_ref.shape[0])
    o_ref[...] = r_ref[...] + _dot(x_ref[...], wb_ref[...])


def _ws_swiglu_kernel(x_ref, wg_ref, wu_ref, o_ref, wgb_ref, wub_ref):
    @pl.when(pl.program_id(1) == 0)
    def _():
        _cast_rows(wg_ref, wgb_ref, wg_ref.shape[0])
        _cast_rows(wu_ref, wub_ref, wu_ref.shape[0])
    x = x_ref[...]
    g = _dot(x, wgb_ref[...])
    u = _dot(x, wub_ref[...])
    o_ref[...] = (g * _sigmoid(g) * u).astype(o_ref.dtype)


def ws_matmul(x, w, layer, *, tm, tn, out_dtype, residual=None, name):
    m, k = x.shape
    n = w.shape[2]
    in_specs = [pl.BlockSpec((tm, k), lambda j, i: (i, 0)),
                pl.BlockSpec((None, k, tn), lambda j, i: (layer, 0, j))]
    args = [x, w]
    kern = _ws_plain_kernel
    scratch = [pltpu.VMEM((k, tn), BF16)]
    if residual is not None:
        in_specs.append(pl.BlockSpec((tm, tn), lambda j, i: (i, j)))
        args.append(residual)
        kern = _ws_residual_kernel
    elif w.dtype == BF16:
        kern, scratch = _ws_bf16w_kernel, []
    vmem = 2 * (tm * k * 2 + k * tn * 4 + 2 * tm * tn * 4) + k * tn * 2 + tm * tn * 4 + (6 << 20)
    return pl.pallas_call(
        kern,
        grid=(n // tn, m // tm),
        in_specs=in_specs,
        out_specs=pl.BlockSpec((tm, tn), lambda j, i: (i, j)),
        out_shape=jax.ShapeDtypeStruct((m, n), out_dtype),
        scratch_shapes=scratch,
        compiler_params=_params(2, vmem),
        name=name,
    )(*args)


def ws_swiglu(x, w_up, layer, d_ff, *, tm, tn, name):
    m, k = x.shape
    nb = d_ff // tn
    vmem = 2 * (tm * k * 2 + 2 * k * tn * 4 + tm * tn * 2) + 2 * k * tn * 2 + 3 * tm * tn * 4 + (6 << 20)
    return pl.pallas_call(
        _ws_swiglu_kernel,
        grid=(nb, m // tm),
        in_specs=[pl.BlockSpec((tm, k), lambda j, i: (i, 0)),
                  pl.BlockSpec((None, k, tn), lambda j, i: (layer, 0, j)),
                  pl.BlockSpec((None, k, tn), lambda j, i: (layer, 0, j + nb))],
        out_specs=pl.BlockSpec((tm, tn), lambda j, i: (i, j)),
        out_shape=jax.ShapeDtypeStruct((m, d_ff), BF16),
        scratch_shapes=[pltpu.VMEM((k, tn), BF16), pltpu.VMEM((k, tn), BF16)],
        compiler_params=_params(2, vmem),
        name=name,
    )(x, w_up, w_up)


NA_ROWS = SEQ // GRID_W
NA_NBLK = SEQ // NA_QBLK
NA_QROWS = NA_QBLK // GRID_W
NA_KROWS = NA_KWIN // GRID_W
assert NA_KROWS >= NA_QROWS + NA_WIN_R - 1 and (NA_ROWS - NA_KROWS) % 2 == 0 and NA_QROWS % 2 == 0


def _na_key_start_row(blk):
    lo, hi = 0, NA_ROWS - NA_KROWS
    start = NA_QROWS * blk - NA_WIN_R // 2
    return jnp.clip(start, lo, hi) if isinstance(blk, jax.Array) else min(max(start, lo), hi)


def _na_block_geometry(blk):
    geo = []
    for a in range(NA_QROWS):
        qr = NA_QROWS * blk + a
        rs = min(max(qr - NA_WIN_R // 2, 0), NA_ROWS - NA_WIN_R)
        for c in range(NA_KROWS):
            kr = _na_key_start_row(blk) + c
            geo.append(kr - qr + NA_WIN_R - 1 if rs <= kr < rs + NA_WIN_R else None)
    return tuple(geo)


NA_GEOMETRIES = tuple(dict.fromkeys(_na_block_geometry(b) for b in range(NA_NBLK)))
NA_BLOCK_PATTERN = tuple(NA_GEOMETRIES.index(_na_block_geometry(b)) for b in range(NA_NBLK))


def _na_pattern(i):
    res = NA_BLOCK_PATTERN[-1]
    for b in reversed(range(NA_NBLK - 1)):
        res = jnp.where(i == b, NA_BLOCK_PATTERN[b], res)
    return res


def _na_bias_table(rpb):
    hh = rpb.shape[0]
    span = GRID_W - 1
    left = span - (NA_WIN_C - 1)
    right = 2 * span + 1 - left - (2 * NA_WIN_C - 1)
    ext = jnp.pad(rpb, ((0, 0), (0, 0), (left, right)), constant_values=NEG)
    toep = jnp.stack([ext[:, :, span - qc:span - qc + GRID_W] for qc in range(GRID_W)], axis=2)
    qc = np.arange(GRID_W)
    cs = np.clip(qc - NA_WIN_C // 2, 0, GRID_W - NA_WIN_C)
    kc = np.arange(GRID_W)
    col_ok = (kc[None, :] >= cs[:, None]) & (kc[None, :] < cs[:, None] + NA_WIN_C)
    toep = jnp.where(col_ok[None, None], toep, NEG)
    neg = jnp.full((hh, GRID_W, GRID_W), NEG, rpb.dtype)
    patterns = []
    for geo in NA_GEOMETRIES:
        q_rows = []
        for a in range(NA_QROWS):
            offs = geo[a * NA_KROWS:(a + 1) * NA_KROWS]
            k_blocks = [neg if d is None else toep[:, d] for d in offs]
            q_rows.append(jnp.concatenate(k_blocks, axis=-1))
        patterns.append(jnp.concatenate(q_rows, axis=-2))
    return jnp.stack(patterns, axis=0)


def _na_kernel(q_ref, k_ref, v_ref, bias_ref, gq_ref, gk_ref, o_ref, kn_ref, vb_ref):
    i = pl.program_id(1)

    @pl.when(i == 0)
    def _():
        def body(c, carry):
            r = pl.multiple_of(c * 256, 256)
            for h in range(NA_HEADS):
                hs = slice(h * NA_HD, (h + 1) * NA_HD)
                kn_ref[pl.ds(r, 256), hs] = _rms(k_ref[pl.ds(r, 256), hs], gk_ref[...]).astype(BF16)
            vb_ref[pl.ds(r, 256), :] = v_ref[pl.ds(r, 256), :].astype(BF16)
            return carry
        lax.fori_loop(0, SEQ // 256, body, 0)

    start = pl.multiple_of(_na_key_start_row(i) * GRID_W, 2 * GRID_W)
    scale = NA_HD ** -0.5
    for h in range(NA_HEADS):
        hs = slice(h * NA_HD, (h + 1) * NA_HD)
        q = (_rms(q_ref[:, hs], gq_ref[...]) * scale).astype(BF16)
        s = _dot_nt(q, kn_ref[pl.ds(start, NA_KWIN), hs]) + bias_ref[0, h]
        m = jnp.max(s, axis=-1, keepdims=True)
        p = jnp.exp(s - m)
        l = jnp.sum(p, axis=-1, keepdims=True)
        o = _dot(p.astype(BF16), vb_ref[pl.ds(start, NA_KWIN), hs]) / l
        o_ref[:, hs] = o.astype(o_ref.dtype)


def na_attention(z, bias, gq, gk, batch):
    nblk = SEQ // NA_QBLK
    w = NA_HEADS * NA_HD
    vmem = (2 * (NA_QBLK * w * 4 + 2 * SEQ * w * 4 + NA_HEADS * NA_QBLK * NA_KWIN * 4 + NA_QBLK * w * 2)
            + 2 * SEQ * w * 2 + (8 << 20))
    return pl.pallas_call(
        _na_kernel,
        grid=(batch, nblk),
        in_specs=[pl.BlockSpec((NA_QBLK, w), lambda b, i: (b * nblk + i, COL_NA_Q // w)),
                  pl.BlockSpec((SEQ, w), lambda b, i: (b, COL_NA_K // w)),
                  pl.BlockSpec((SEQ, w), lambda b, i: (b, COL_NA_V // w)),
                  pl.BlockSpec((1, NA_HEADS, NA_QBLK, NA_KWIN), lambda b, i: (_na_pattern(i), 0, 0, 0)),
                  pl.BlockSpec((1, NA_HD), lambda b, i: (0, 0)),
                  pl.BlockSpec((1, NA_HD), lambda b, i: (0, 0))],
        out_specs=pl.BlockSpec((NA_QBLK, w), lambda b, i: (b * nblk + i, 0)),
        out_shape=jax.ShapeDtypeStruct((batch * SEQ, w), BF16),
        scratch_shapes=[pltpu.VMEM((SEQ, w), BF16), pltpu.VMEM((SEQ, w), BF16)],
        compiler_params=_params(2, vmem),
        name="na_attention",
    )(z, z, z, bias, gq.reshape(1, NA_HD), gk.reshape(1, NA_HD))


def _rope_tables(width, head_dim, first_lane):
    half = head_dim // 2
    freqs = ROPE_THETA ** (-2.0 * np.arange(half, dtype=np.float32) / head_dim)
    ang = jnp.arange(SEQ, dtype=F32)[:, None] * jnp.asarray(freqs, F32)[None, :]
    cos, sin = jnp.cos(ang), jnp.sin(ang)
    lane = np.arange(width)
    rel = lane - first_lane
    in_rope = (rel >= 0) & (rel < (width - first_lane if first_lane == 0 else head_dim))
    p = np.where(in_rope, rel % head_dim, 0)
    j = p % half
    first_half = in_rope & (p < half)
    second_half = in_rope & (p >= half)
    cos_t = jnp.where(in_rope[None, :], cos[:, j], 1.0)
    sin_a = jnp.where(first_half[None, :], -sin[:, j], 0.0)
    sin_b = jnp.where(second_half[None, :], sin[:, j], 0.0)
    return cos_t, sin_a, sin_b


def _apply_rope(x, cos_t, sin_a, sin_b, half):
    n = x.shape[-1]
    return (x * cos_t + pltpu.roll(x, n - half, axis=1) * sin_a
            + pltpu.roll(x, half, axis=1) * sin_b)


def _mla_prep_kernel(cq_ref, ckv_ref, kpe_ref, wq_ref, wkv_ref, gcq_ref, gckv_ref,
                     gqn_ref, gkn_ref, cos_ref, sa_ref, sb_ref, q_ref, k_ref, v_ref):
    half = MLA_ROPE // 2
    cos_t, sin_a, sin_b = cos_ref[...], sa_ref[...], sb_ref[...]
    scale = MLA_QK ** -0.5
    cq = _rms(cq_ref[...], gcq_ref[...]).astype(BF16)
    q_raw = _dot(cq, wq_ref[...])
    ckv = _rms(ckv_ref[...], gckv_ref[...]).astype(BF16)
    kv_raw = _dot(ckv, wkv_ref[...])
    kpe = kpe_ref[...]
    kpe_ss = jnp.sum(kpe * kpe, axis=-1, keepdims=True)
    for h in range(MLA_HEADS):
        cs = slice(h * MLA_PAD, (h + 1) * MLA_PAD)
        qc = q_raw[:, cs]
        ms = jnp.sum(qc * qc, axis=-1, keepdims=True) * (1.0 / MLA_QK)
        qn = qc * lax.rsqrt(ms + EPS) * gqn_ref[...]
        q_ref[:, cs] = (_apply_rope(qn, cos_t, sin_a, sin_b, half) * scale).astype(BF16)
        kn = kv_raw[:, h * MLA_PAD:h * MLA_PAD + MLA_NOPE]
        ms = (jnp.sum(kn * kn, axis=-1, keepdims=True) + kpe_ss) * (1.0 / MLA_QK)
        kc = jnp.concatenate([kn, kpe], axis=-1) * lax.rsqrt(ms + EPS) * gkn_ref[...]
        k_ref[:, cs] = _apply_rope(kc, cos_t, sin_a, sin_b, half).astype(BF16)
        v_ref[:, h * MLA_NOPE:(h + 1) * MLA_NOPE] = kv_raw[:, h * MLA_PAD + MLA_NOPE:(h + 1) * MLA_PAD].astype(BF16)


def mla_prep(z, wq_pad, wkv, gcq, gckv, gqn_pad, gkn_pad, tabs, tm=512):
    t = z.shape[0]
    sb = SEQ // tm
    qw = MLA_HEADS * MLA_PAD
    row = lambda i: (i, 0)
    const = lambda i: (0, 0)
    pos = lambda i: (i % sb, 0)
    return pl.pallas_call(
        _mla_prep_kernel,
        grid=(t // tm,),
        in_specs=[pl.BlockSpec((tm, MLA_Q_RANK), lambda i: (i, COL_CQ // MLA_Q_RANK)),
                  pl.BlockSpec((tm, MLA_KV_RANK), lambda i: (i, COL_CKV // MLA_KV_RANK)),
                  pl.BlockSpec((tm, LANES), lambda i: (i, COL_KPE // LANES)),
                  pl.BlockSpec((MLA_Q_RANK, qw), const),
                  pl.BlockSpec((MLA_KV_RANK, qw), const),
                  pl.BlockSpec((1, MLA_Q_RANK), const),
                  pl.BlockSpec((1, MLA_KV_RANK), const),
                  pl.BlockSpec((1, MLA_PAD), const),
                  pl.BlockSpec((1, MLA_PAD), const),
                  pl.BlockSpec((tm, MLA_PAD), pos),
                  pl.BlockSpec((tm, MLA_PAD), pos),
                  pl.BlockSpec((tm, MLA_PAD), pos)],
        out_specs=[pl.BlockSpec((tm, qw), row), pl.BlockSpec((tm, qw), row),
                   pl.BlockSpec((tm, MLA_HEADS * MLA_NOPE), row)],
        out_shape=[jax.ShapeDtypeStruct((t, qw), BF16), jax.ShapeDtypeStruct((t, qw), BF16),
                   jax.ShapeDtypeStruct((t, MLA_HEADS * MLA_NOPE), BF16)],
        compiler_params=_params(1, 40 << 20),
        name="mla_prep",
    )(z, z, z, wq_pad, wkv, gcq.reshape(1, -1), gckv.reshape(1, -1), gqn_pad, gkn_pad, *tabs)


def _mla_attn_kernel(q_ref, k_ref, v_ref, o_ref):
    for h in range(MLA_HEADS):
        cs = slice(h * MLA_PAD, (h + 1) * MLA_PAD)
        vs = slice(h * MLA_NOPE, (h + 1) * MLA_NOPE)
        s = _dot_nt(q_ref[:, cs], k_ref[:, cs])
        m = jnp.max(s, axis=-1, keepdims=True)
        p = jnp.exp(s - m)
        l = jnp.sum(p, axis=-1, keepdims=True)
        o = _dot(p.astype(BF16), v_ref[:, vs]) / l
        o_ref[:, vs] = o.astype(o_ref.dtype)


def mla_attention(q, k, v, batch, tq=512):
    nq = SEQ // tq
    qw = MLA_HEADS * MLA_PAD
    vw = MLA_HEADS * MLA_NOPE
    vmem = 2 * (tq * qw * 2 + SEQ * qw * 2 + SEQ * vw * 2 + tq * vw * 2) + 4 * tq * SEQ * 4 + (8 << 20)
    return pl.pallas_call(
        _mla_attn_kernel,
        grid=(batch, nq),
        in_specs=[pl.BlockSpec((tq, qw), lambda b, i: (b * nq + i, 0)),
                  pl.BlockSpec((SEQ, qw), lambda b, i: (b, 0)),
                  pl.BlockSpec((SEQ, vw), lambda b, i: (b, 0))],
        out_specs=pl.BlockSpec((tq, vw), lambda b, i: (b * nq + i, 0)),
        out_shape=jax.ShapeDtypeStruct((batch * SEQ, vw), BF16),
        compiler_params=_params(2, vmem),
        name="mla_attention",
    )(q, k, v)


def _rms_halves(x, g):
    lo = lax.broadcasted_iota(jnp.int32, x.shape, 1) < SWA_HD
    x2 = x * x
    s_lo = jnp.sum(jnp.where(lo, x2, 0.0), axis=-1, keepdims=True)
    s_hi = jnp.sum(jnp.where(lo, 0.0, x2), axis=-1, keepdims=True)
    ms = jnp.where(lo, s_lo, s_hi) * (1.0 / SWA_HD)
    return x * lax.rsqrt(ms + EPS) * g


def _swa_kernel(sink_ref, q_ref, k_ref, v_ref, gq_ref, gk_ref, cosq_ref, saq_ref, sbq_ref,
                cosk_ref, sak_ref, sbk_ref, o_ref, kk_ref, vv_ref):
    n = pl.program_id(1)
    nblk = SEQ // SWA_QBLK
    half = SWA_HD // 2

    @pl.when(n == 0)
    def _():
        zeros = jnp.zeros((SWA_BLOCK, LANES), BF16)
        for c in range(4):
            kk_ref[c, pl.ds(0, SWA_BLOCK), :] = zeros
            kk_ref[c, pl.ds(SEQ + SWA_BLOCK, SWA_BLOCK), :] = zeros
            vv_ref[c, pl.ds(0, SWA_BLOCK), :] = zeros
            vv_ref[c, pl.ds(SEQ + SWA_BLOCK, SWA_BLOCK), :] = zeros

        def body(c, carry):
            r = pl.multiple_of(c * 256, 256)
            dst = pl.ds(r + SWA_BLOCK, 256)
            lo = lax.broadcasted_iota(jnp.int32, (256, LANES), 1) < SWA_HD
            kr = _apply_rope(_rms_halves(k_ref[pl.ds(r, 256), :], gk_ref[...]),
                             cosk_ref[pl.ds(r, 256), :], sak_ref[pl.ds(r, 256), :],
                             sbk_ref[pl.ds(r, 256), :], half)
            ks = pltpu.roll(kr, SWA_HD, axis=1)
            kk_ref[0, dst, :] = jnp.where(lo, kr, 0.0).astype(BF16)
            kk_ref[1, dst, :] = jnp.where(lo, 0.0, ks).astype(BF16)
            kk_ref[2, dst, :] = jnp.where(lo, ks, 0.0).astype(BF16)
            kk_ref[3, dst, :] = jnp.where(lo, 0.0, kr).astype(BF16)
            vr = v_ref[pl.ds(r, 256), :]
            vs = pltpu.roll(vr, SWA_HD, axis=1)
            vv_ref[0, dst, :] = jnp.where(lo, vr, 0.0).astype(BF16)
            vv_ref[1, dst, :] = jnp.where(lo, 0.0, vs).astype(BF16)
            vv_ref[2, dst, :] = jnp.where(lo, vs, 0.0).astype(BF16)
            vv_ref[3, dst, :] = jnp.where(lo, 0.0, vr).astype(BF16)
            return carry
        lax.fori_loop(0, SEQ // 256, body, 0)

    band = pl.ds(pl.multiple_of(n * SWA_QBLK, SWA_QBLK), SWA_BAND)
    a = lax.broadcasted_iota(jnp.int32, (SWA_QBLK, SWA_BAND), 0)
    c = lax.broadcasted_iota(jnp.int32, (SWA_QBLK, SWA_BAND), 1)
    c_min = jnp.where(n == 0, SWA_BLOCK, 0)
    c_max = jnp.where(n == nblk - 1, SWA_BAND - SWA_BLOCK, SWA_BAND)
    valid = (c >= a) & (c <= a + 2 * SWA_BLOCK) & (c >= c_min) & (c < c_max)
    scale = SWA_HD ** -0.5
    for pair in range(SWA_HEADS // 2):
        ps = slice(pair * LANES, (pair + 1) * LANES)
        grp = pair // 2
        q = _apply_rope(_rms_halves(q_ref[:, ps], gq_ref[...]),
                        cosq_ref[...], saq_ref[...], sbq_ref[...], half)
        q = (q * scale).astype(BF16)
        acc = None
        for hf in range(2):
            sink = sink_ref[2 * pair + hf]
            s = _dot_nt(q, kk_ref[2 * grp + hf, band, :])
            s = jnp.where(valid, s, NEG)
            m = jnp.maximum(jnp.max(s, axis=-1, keepdims=True), sink)
            p = jnp.exp(s - m)
            den = jnp.sum(p, axis=-1, keepdims=True) + jnp.exp(sink - m)
            o = _dot(p.astype(BF16), vv_ref[2 * grp + hf, band, :]) / den
            acc = o if acc is None else acc + o
        o_ref[:, ps] = acc.astype(o_ref.dtype)


def swa_attention(z, sink, gq, gk, tabs, batch):
    nblk = SEQ // SWA_QBLK
    qw = SWA_HEADS * SWA_HD
    cos_t, sin_a, sin_b = tabs
    g2 = lambda g: jnp.concatenate([g, g]).reshape(1, LANES)
    blk = lambda b, n: (n, 0)
    full = lambda b, n: (0, 0)
    tab_blk = pl.BlockSpec((SWA_QBLK, LANES), blk)
    tab_full = pl.BlockSpec((SEQ, LANES), full)
    vmem = (2 * (SWA_QBLK * qw * 6 + 2 * SEQ * LANES * 4 + 3 * SEQ * LANES * 4 + 3 * SWA_QBLK * LANES * 4)
            + 8 * (SEQ + 2 * SWA_BLOCK) * LANES * 2 + 8 * SWA_QBLK * SWA_BAND * 4 + (8 << 20))
    return pl.pallas_call(
        _swa_kernel,
        grid=(batch, nblk),
        in_specs=[pl.BlockSpec(memory_space=pltpu.SMEM),
                  pl.BlockSpec((SWA_QBLK, qw), lambda b, n: (b * nblk + n, COL_SWQ // qw)),
                  pl.BlockSpec((SEQ, LANES), lambda b, n: (b, COL_SWK // LANES)),
                  pl.BlockSpec((SEQ, LANES), lambda b, n: (b, COL_SWV // LANES)),
                  pl.BlockSpec((1, LANES), full), pl.BlockSpec((1, LANES), full),
                  tab_blk, tab_blk, tab_blk, tab_full, tab_full, tab_full],
        out_specs=pl.BlockSpec((SWA_QBLK, qw), lambda b, n: (b * nblk + n, 0)),
        out_shape=jax.ShapeDtypeStruct((batch * SEQ, qw), BF16),
        scratch_shapes=[pltpu.VMEM((4, SEQ + 2 * SWA_BLOCK, LANES), BF16),
                        pltpu.VMEM((4, SEQ + 2 * SWA_BLOCK, LANES), BF16)],
        compiler_params=_params(2, vmem),
        name="swa_attention",
    )(sink, z, z, z, g2(gq), g2(gk), cos_t, sin_a, sin_b, cos_t, sin_a, sin_b)


def _mem_attn_kernel(q_ref, k_ref, v_ref, gq_ref, gk_ref, o_ref):
    scale = MEM_HD ** -0.5
    for h in range(MEM_HEADS):
        hs = slice(h * MEM_HD, (h + 1) * MEM_HD)
        q = (_rms(q_ref[:, hs], gq_ref[...]) * scale).astype(BF16)
        k = _rms(k_ref[:, hs], gk_ref[...]).astype(BF16)
        s = _dot_nt(q, k)
        m = jnp.max(s, axis=-1, keepdims=True)
        p = jnp.exp(s - m)
        l = jnp.sum(p, axis=-1, keepdims=True)
        o = _dot(p.astype(BF16), v_ref[:, hs].astype(BF16)) / l
        o_ref[:, hs] = o.astype(o_ref.dtype)


def mem_attention(z, memkv, gq, gk, batch, tq=512):
    nq = SEQ // tq
    w = MEM_HEADS * MEM_HD
    return pl.pallas_call(
        _mem_attn_kernel,
        grid=(batch, nq),
        in_specs=[pl.BlockSpec((tq, w), lambda b, i: (b * nq + i, COL_MEQ // w)),
                  pl.BlockSpec((MEM_LEN, w), lambda b, i: (b, 0)),
                  pl.BlockSpec((MEM_LEN, w), lambda b, i: (b, 1)),
                  pl.BlockSpec((1, MEM_HD), lambda b, i: (0, 0)),
                  pl.BlockSpec((1, MEM_HD), lambda b, i: (0, 0))],
        out_specs=pl.BlockSpec((tq, w), lambda b, i: (b * nq + i, 0)),
        out_shape=jax.ShapeDtypeStruct((batch * SEQ, w), BF16),
        compiler_params=_params(2, 24 << 20),
        name="mem_attention",
    )(z, memkv, memkv, gq.reshape(1, MEM_HD), gk.reshape(1, MEM_HD))


def _merge_kernel(layer, u_ref, b0_ref, b1_ref, b2_ref, b3_ref, wg_hbm, wb_ref, bg_ref, o_ref,
                  stage_ref, wgs_ref, wbs_ref, sem):
    branches = (b0_ref, b1_ref, b2_ref, b3_ref)
    j, i = pl.program_id(0), pl.program_id(1)
    tn = wgs_ref.shape[2]
    slot = j % 2

    def copies(jj, s):
        return [pltpu.make_async_copy(wg_hbm.at[layer, :, n, pl.ds(pl.multiple_of(jj * tn, tn), tn)],
                                      stage_ref.at[s, n], sem.at[s]) for n in range(4)]

    @pl.when(jnp.logical_and(j == 0, i == 0))
    def _():
        for c in copies(0, 0):
            c.start()

    @pl.when(i == 0)
    def _():
        for c in copies(j, slot):
            c.wait()

        @pl.when(j + 1 < pl.num_programs(0))
        def _():
            for c in copies(j + 1, 1 - slot):
                c.start()
        for n in range(4):
            _cast_rows(stage_ref.at[slot, n], wgs_ref.at[n], D_MODEL)
            wbs_ref[n] = wb_ref[n].astype(BF16)

    u = u_ref[...]
    acc = None
    for n in range(4):
        gate = _sigmoid(_dot(u, wgs_ref[n]) + bg_ref[n:n + 1, :])
        term = gate * _dot(branches[n][...], wbs_ref[n])
        acc = term if acc is None else acc + term
    o_ref[...] = acc.astype(o_ref.dtype)


def gated_merge(u, branches, w_gate, w_branch, b_gate, layer, tm=1024, tn=256):
    t = u.shape[0]
    bw = branches[0].shape[1]
    vmem = (2 * (tm * D_MODEL * 2 + 4 * tm * bw * 2 + 4 * D_MODEL * tn * 4 + 4 * bw * tn * 4 + tm * tn * 2)
            + 4 * D_MODEL * tn * 2 + 4 * bw * tn * 2 + 4 * tm * tn * 4 + (6 << 20))
    return pl.pallas_call(
        functools.partial(_merge_kernel, layer),
        grid=(D_MODEL // tn, t // tm),
        in_specs=[pl.BlockSpec((tm, D_MODEL), lambda j, i: (i, 0))]
        + [pl.BlockSpec((tm, bw), lambda j, i: (i, 0)) for _ in range(4)]
        + [pl.BlockSpec(memory_space=pl.ANY),
           pl.BlockSpec((None, 4, bw, tn), lambda j, i: (layer, 0, 0, j)),
           pl.BlockSpec((None, 4, tn), lambda j, i: (layer, 0, j))],
        out_specs=pl.BlockSpec((tm, tn), lambda j, i: (i, j)),
        out_shape=jax.ShapeDtypeStruct((t, D_MODEL), BF16),
        scratch_shapes=[pltpu.VMEM((2, 4, D_MODEL, tn), F32), pltpu.VMEM((4, D_MODEL, tn), BF16),
                        pltpu.VMEM((4, bw, tn), BF16), pltpu.SemaphoreType.DMA((2,))],
        compiler_params=_params(2, vmem),
        name="gated_merge",
    )(u, *branches, w_gate, w_branch, b_gate)


def _router_kernel(h_ref, g_ref, wr_ref, ri_ref, rw_ref, cnt_ref, carry_ref):
    @pl.when(pl.program_id(0) == 0)
    def _():
        carry_ref[...] = jnp.zeros_like(carry_ref)

    tm = h_ref.shape[0]
    hn = _rms(h_ref[...], g_ref[...])
    logits = jnp.dot(hn, wr_ref[...], precision=lax.Precision.HIGHEST, preferred_element_type=F32)
    lane = lax.broadcasted_iota(jnp.int32, (tm, LANES), 1)
    lane_f = lane.astype(F32)
    logits = jnp.where(lane < N_EXPERTS, logits, -jnp.inf)
    m1 = jnp.max(logits, axis=-1, keepdims=True)
    i1 = jnp.min(jnp.where(logits == m1, lane_f, float(LANES)), axis=-1, keepdims=True)
    oh1 = lane_f == i1
    rest = jnp.where(oh1, -jnp.inf, logits)
    m2 = jnp.max(rest, axis=-1, keepdims=True)
    i2 = jnp.min(jnp.where(rest == m2, lane_f, float(LANES)), axis=-1, keepdims=True)
    oh2 = lane_f == i2
    e2 = jnp.exp(m2 - m1)
    w1 = 1.0 / (1.0 + e2)
    w2 = e2 / (1.0 + e2)
    chosen = jnp.where(oh1 | oh2, 1.0, 0.0)
    before = (lax.broadcasted_iota(jnp.int32, (tm, tm), 1)
              < lax.broadcasted_iota(jnp.int32, (tm, tm), 0))
    prefix = _dot(jnp.where(before, 1.0, 0.0).astype(BF16), chosen.astype(BF16)) + carry_ref[...]
    r1 = jnp.sum(jnp.where(oh1, prefix, 0.0), axis=-1, keepdims=True).astype(jnp.int32)
    r2 = jnp.sum(jnp.where(oh2, prefix, 0.0), axis=-1, keepdims=True).astype(jnp.int32)
    carry_ref[...] += jnp.sum(chosen, axis=0, keepdims=True)
    ri_ref[...] = jnp.where(lane == 0, i1.astype(jnp.int32),
                            jnp.where(lane == 1, i2.astype(jnp.int32),
                                      jnp.where(lane == 2, r1, jnp.where(lane == 3, r2, 0))))
    rw_ref[...] = jnp.where(lane == 0, w1, jnp.where(lane == 1, w2, 0.0))
    cnt_ref[...] = carry_ref[...]


def moe_router(h, g, w_router, tm=512):
    t, d = h.shape
    wr = jnp.pad(w_router, ((0, 0), (0, LANES - N_EXPERTS)))
    return pl.pallas_call(
        _router_kernel,
        grid=(t // tm,),
        in_specs=[pl.BlockSpec((tm, d), lambda i: (i, 0)),
                  pl.BlockSpec((1, d), lambda i: (0, 0)),
                  pl.BlockSpec((d, LANES), lambda i: (0, 0))],
        out_specs=[pl.BlockSpec((tm, LANES), lambda i: (i, 0)),
                   pl.BlockSpec((tm, LANES), lambda i: (i, 0)),
                   pl.BlockSpec((1, LANES), lambda i: (0, 0))],
        out_shape=[jax.ShapeDtypeStruct((t, LANES), jnp.int32),
                   jax.ShapeDtypeStruct((t, LANES), F32),
                   jax.ShapeDtypeStruct((1, LANES), F32)],
        scratch_shapes=[pltpu.VMEM((1, LANES), F32)],
        compiler_params=_params(1, 32 << 20),
        name="moe_router",
    )(h, g.reshape(1, d), wr)


ROW_DMA_UNROLL = 8


def _row_copy(src_hbm, row, dst_vmem, r, sem):
    return pltpu.make_async_copy(src_hbm.at[pl.ds(row, 1), :], dst_vmem.at[pl.ds(r, 1), :], sem)


def _moe_scatter_kernel(pos_ref, h_ref, g_ref, xs_in_hbm, xs_hbm, pk_ref, sem):
    del xs_in_hbm
    tg = h_ref.shape[0]
    i = pl.program_id(0)
    last = pl.num_programs(0) - 1
    slot = i % 2

    def row_copy(s, r, dst_row):
        return pltpu.make_async_copy(pk_ref.at[s, pl.ds(r, 1), :], xs_hbm.at[pl.ds(dst_row, 1), :], sem.at[s])

    def wait_slot(s):
        def drain(r, carry):
            row_copy(s, r, 0).wait()
            row_copy(s, r, 0).wait()
            return carry
        lax.fori_loop(0, tg, drain, 0, unroll=8)

    @pl.when(i >= 2)
    def _():
        wait_slot(slot)

    pk_ref[slot] = _rms(h_ref[...], g_ref[...])

    def issue(grp, carry):
        for k in range(ROW_DMA_UNROLL):
            r = grp * ROW_DMA_UNROLL + k
            t = i * tg + r
            row_copy(slot, r, pos_ref[2 * t]).start(priority=0)
            row_copy(slot, r, pos_ref[2 * t + 1]).start(priority=1)
        return carry
    lax.fori_loop(0, tg // ROW_DMA_UNROLL, issue, 0)

    @pl.when(jnp.logical_and(i == last, i >= 1))
    def _():
        wait_slot(1 - slot)

    @pl.when(i == last)
    def _():
        wait_slot(slot)


def _cast_kernel(x_ref, o_ref):
    o_ref[...] = x_ref[...].astype(o_ref.dtype)


def cast_bf16(x, tm=512):
    m, d = x.shape
    return pl.pallas_call(
        _cast_kernel,
        grid=(m // tm,),
        in_specs=[pl.BlockSpec((tm, d), lambda i: (i, 0))],
        out_specs=pl.BlockSpec((tm, d), lambda i: (i, 0)),
        out_shape=jax.ShapeDtypeStruct((m, d), BF16),
        compiler_params=_params(1, 32 << 20),
        name="cast_bf16",
    )(x)


def moe_scatter(pos_flat, h, g, rows, tg=256):
    t, d = h.shape
    xs0 = jnp.zeros((rows, d), F32)
    return pl.pallas_call(
        _moe_scatter_kernel,
        grid_spec=pltpu.PrefetchScalarGridSpec(
            num_scalar_prefetch=1,
            grid=(t // tg,),
            in_specs=[pl.BlockSpec((tg, d), lambda i, p: (i, 0)),
                      pl.BlockSpec((1, d), lambda i, p: (0, 0)),
                      pl.BlockSpec(memory_space=pl.ANY)],
            out_specs=pl.BlockSpec(memory_space=pl.ANY),
            scratch_shapes=[pltpu.VMEM((2, tg, d), F32), pltpu.SemaphoreType.DMA((2,))]),
        out_shape=jax.ShapeDtypeStruct((rows, d), F32),
        input_output_aliases={3: 0},
        compiler_params=_params(1, 24 << 20),
        name="moe_scatter",
    )(pos_flat, h, g.reshape(1, d), xs0)


def _stream_expert_weights(te_ref, first_ref, nxt_ref, meta_ref, cnt_ref, make_copies, cast_slot):
    j, i = pl.program_id(0), pl.program_id(1)

    @pl.when(jnp.logical_and(j == 0, i == 0))
    def _():
        cnt_ref[0] = 0
        for c in make_copies(0, te_ref[0], 0):
            c.start()

    @pl.when(first_ref[i] == 1)
    def _():
        slot = cnt_ref[0] % 2
        for c in make_copies(j, te_ref[i], slot):
            c.wait()
        wraps = nxt_ref[i] < 0
        j_next = jnp.where(wraps, j + 1, j)
        e_next = jnp.where(wraps, meta_ref[1], nxt_ref[i])

        @pl.when(j_next < pl.num_programs(0))
        def _():
            for c in make_copies(j_next, e_next, 1 - slot):
                c.start()
        cast_slot(slot)
        cnt_ref[0] = cnt_ref[0] + 1


def _moe_up_kernel(te_ref, first_ref, nxt_ref, meta_ref, x_ref, w_hbm, o_ref,
                   stage_ref, wgb_ref, wub_ref, sem, cnt_ref):
    i = pl.program_id(1)
    active = i < meta_ref[0]
    tf = wgb_ref.shape[1]
    nf = pl.num_programs(0)

    def make_copies(jj, e, slot):
        return [pltpu.make_async_copy(w_hbm.at[e, :, pl.ds(pl.multiple_of((jj + half * nf) * tf, tf), tf)],
                                      stage_ref.at[slot, half], sem.at[slot]) for half in range(2)]

    def cast_slot(slot):
        _cast_rows(stage_ref.at[slot, 0], wgb_ref, D_MODEL)
        _cast_rows(stage_ref.at[slot, 1], wub_ref, D_MODEL)

    _stream_expert_weights(te_ref, first_ref, nxt_ref, meta_ref, cnt_ref, make_copies, cast_slot)

    @pl.when(active)
    def _():
        x = x_ref[...]
        g = _dot(x, wgb_ref[...])
        u = _dot(x, wub_ref[...])
        o_ref[...] = (g * _sigmoid(g) * u).astype(o_ref.dtype)

    @pl.when(jnp.logical_not(active))
    def _():
        o_ref[...] = jnp.zeros_like(o_ref)


def _active_row(j, i, te, first, nxt, meta):
    return jnp.maximum(jnp.minimum(i, meta[0] - 1), 0)


def moe_up(xs, w_up, sched, tf=512):
    rows, d = xs.shape
    nf = D_FF_EXPERT // tf
    vmem = (2 * (MOE_TM * d * 2 + MOE_TM * tf * 2) + 4 * d * tf * 4 + 2 * d * tf * 2
            + 3 * MOE_TM * tf * 4 + 2 * MOE_TM * d * 2 + (6 << 20))
    return pl.pallas_call(
        _moe_up_kernel,
        grid_spec=pltpu.PrefetchScalarGridSpec(
            num_scalar_prefetch=4,
            grid=(nf, rows // MOE_TM),
            in_specs=[pl.BlockSpec((MOE_TM, d), lambda j, i, *s: (_active_row(j, i, *s), 0)),
                      pl.BlockSpec(memory_space=pl.ANY)],
            out_specs=pl.BlockSpec((MOE_TM, tf), lambda j, i, *s: (i, j)),
            scratch_shapes=[pltpu.VMEM((2, 2, d, tf), F32), pltpu.VMEM((d, tf), BF16),
                            pltpu.VMEM((d, tf), BF16), pltpu.SemaphoreType.DMA((2,)),
                            pltpu.SMEM((1,), jnp.int32)]),
        out_shape=jax.ShapeDtypeStruct((rows, D_FF_EXPERT), BF16),
        compiler_params=_params(2, vmem),
        name="moe_up",
    )(*sched, xs, w_up)


def _moe_down_kernel(te_ref, first_ref, nxt_ref, meta_ref, a_ref, w_hbm, o_ref, stage_ref, wb_ref, sem):
    j, i = pl.program_id(0), pl.program_id(1)
    active = i < meta_ref[0]
    fk, tn = stage_ref.shape[1], stage_ref.shape[2]

    def half_copy(jj, e, kh):
        return pltpu.make_async_copy(
            w_hbm.at[e, pl.ds(kh * fk, fk), pl.ds(pl.multiple_of(jj * tn, tn), tn)],
            stage_ref.at[kh], sem.at[kh])

    @pl.when(jnp.logical_and(j == 0, i == 0))
    def _():
        for kh in range(2):
            half_copy(0, te_ref[0], kh).start()

    @pl.when(first_ref[i] == 1)
    def _():
        wraps = nxt_ref[i] < 0
        j_next = jnp.where(wraps, j + 1, j)
        e_next = jnp.where(wraps, meta_ref[1], nxt_ref[i])
        for kh in range(2):
            half_copy(j, te_ref[i], kh).wait()
            _cast_rows(stage_ref.at[kh], wb_ref.at[pl.ds(kh * fk, fk)], fk)

            @pl.when(j_next < pl.num_programs(0))
            def _():
                half_copy(j_next, e_next, kh).start()

    @pl.when(active)
    def _():
        o_ref[...] = _dot(a_ref[...], wb_ref[...])

    @pl.when(jnp.logical_not(active))
    def _():
        o_ref[...] = jnp.zeros_like(o_ref)


def moe_down(act, w_down, sched, tn=512):
    rows, f = act.shape
    fk = f // 2
    vmem = (2 * (MOE_TM * f * 2 + MOE_TM * tn * 4) + 2 * fk * tn * 4 + f * tn * 2
            + MOE_TM * tn * 4 + (6 << 20))
    return pl.pallas_call(
        _moe_down_kernel,
        grid_spec=pltpu.PrefetchScalarGridSpec(
            num_scalar_prefetch=4,
            grid=(D_MODEL // tn, rows // MOE_TM),
            in_specs=[pl.BlockSpec((MOE_TM, f), lambda j, i, *s: (_active_row(j, i, *s), 0)),
                      pl.BlockSpec(memory_space=pl.ANY)],
            out_specs=pl.BlockSpec((MOE_TM, tn), lambda j, i, *s: (i, j)),
            scratch_shapes=[pltpu.VMEM((2, fk, tn), F32), pltpu.VMEM((f, tn), BF16),
                            pltpu.SemaphoreType.DMA((2,))]),
        out_shape=jax.ShapeDtypeStruct((rows, D_MODEL), F32),
        compiler_params=_params(2, vmem),
        name="moe_down",
    )(*sched, act, w_down)


def _moe_combine_kernel(pos_ref, h_ref, rw_ref, y_hbm, o_ref, b1_ref, b2_ref, sem):
    tc = b1_ref.shape[1]
    i = pl.program_id(0)
    slot = i % 2

    def start_rows(step, dst_slot):
        def issue(grp, carry):
            for k in range(ROW_DMA_UNROLL):
                r = grp * ROW_DMA_UNROLL + k
                t = step * tc + r
                _row_copy(y_hbm, pos_ref[2 * t], b1_ref.at[dst_slot], r, sem.at[dst_slot]).start(priority=0)
                _row_copy(y_hbm, pos_ref[2 * t + 1], b2_ref.at[dst_slot], r, sem.at[dst_slot]).start(priority=1)
            return carry
        lax.fori_loop(0, tc // ROW_DMA_UNROLL, issue, 0)

    @pl.when(i == 0)
    def _():
        start_rows(0, 0)

    @pl.when(i + 1 < pl.num_programs(0))
    def _():
        start_rows(i + 1, 1 - slot)

    def drain(r, carry):
        _row_copy(y_hbm, 0, b1_ref.at[slot], r, sem.at[slot]).wait()
        _row_copy(y_hbm, 0, b2_ref.at[slot], r, sem.at[slot]).wait()
        return carry
    lax.fori_loop(0, tc, drain, 0, unroll=8)
    o_ref[...] = h_ref[...] + rw_ref[:, 0:1] * b1_ref[slot] + rw_ref[:, 1:2] * b2_ref[slot]


def moe_combine(pos_flat, h, rw, y, tc=256):
    t, d = h.shape
    return pl.pallas_call(
        _moe_combine_kernel,
        grid_spec=pltpu.PrefetchScalarGridSpec(
            num_scalar_prefetch=1,
            grid=(t // tc,),
            in_specs=[pl.BlockSpec((tc, d), lambda i, p: (i, 0)),
                      pl.BlockSpec((tc, LANES), lambda i, p: (i, 0)),
                      pl.BlockSpec(memory_space=pl.ANY)],
            out_specs=pl.BlockSpec((tc, d), lambda i, p: (i, 0)),
            scratch_shapes=[pltpu.VMEM((2, tc, d), F32), pltpu.VMEM((2, tc, d), F32),
                            pltpu.SemaphoreType.DMA((2,))]),
        out_shape=jax.ShapeDtypeStruct((t, d), F32),
        compiler_params=_params(1, 32 << 20),
        name="moe_combine",
    )(pos_flat, h, rw, y)


def moe_block(h, g, w_router, w_up, w_down):
    t = h.shape[0]
    rows = 2 * t + N_EXPERTS * MOE_TM
    n_row_tiles = rows // MOE_TM
    ri, rw, cnt = moe_router(h, g, w_router)
    counts = cnt[0, :N_EXPERTS].astype(jnp.int32)
    padded = ((counts + MOE_TM - 1) // MOE_TM) * MOE_TM
    ends = jnp.cumsum(padded)
    starts = ends - padded
    pos = starts[ri[:, 0:2]] + ri[:, 2:4]
    pos_flat = pos.reshape(-1).astype(jnp.int32)
    n_tiles = (ends[-1] // MOE_TM).astype(jnp.int32)
    tile_first_row = jnp.arange(n_row_tiles, dtype=jnp.int32) * MOE_TM
    tile_e = jnp.sum((tile_first_row[:, None] >= ends[None, :]).astype(jnp.int32), axis=1)
    tile_e = jnp.minimum(tile_e, N_EXPERTS - 1)
    tile_idx = jnp.arange(n_row_tiles, dtype=jnp.int32)
    is_active = tile_idx < n_tiles
    tile_e = jnp.where(is_active, tile_e, tile_e[n_tiles - 1]).astype(jnp.int32)
    first = (is_active & ((tile_idx == 0) | (tile_e != jnp.roll(tile_e, 1)))).astype(jnp.int32)
    eidx = jnp.arange(N_EXPERTS, dtype=jnp.int32)
    later = (eidx[None, :] > eidx[:, None]) & (padded > 0)[None, :]
    nxt = jnp.min(jnp.where(later, eidx[None, :], N_EXPERTS), axis=1)
    nxt = jnp.where(nxt == N_EXPERTS, -1, nxt).astype(jnp.int32)
    sched = (tile_e, first, nxt[tile_e], jnp.stack([n_tiles, tile_e[0]]).astype(jnp.int32))

    xs = cast_bf16(moe_scatter(pos_flat, h, g, rows))
    act = moe_up(xs, w_up, sched)
    y = moe_down(act, w_down, sched)
    return moe_combine(pos_flat, h, rw, y)


IN_PROJ_MOVES = ((0, 2560, 0), (2560, 64, COL_KPE), (2624, 512, COL_SWQ), (3136, 128, COL_SWK),
                 (3264, 128, COL_SWV), (3392, 512, COL_MEQ))


def _in_proj_layout_kernel(w_ref, o_ref):
    for src, width, dst in IN_PROJ_MOVES:
        o_ref[:, dst:dst + width] = w_ref[:, src:src + width].astype(o_ref.dtype)
    tail = COL_KPE + MLA_ROPE
    o_ref[:, tail:] = jnp.zeros((o_ref.shape[0], IN_COLS_PAD - tail), o_ref.dtype)


def _pad_in_proj(w_in, tk=256):
    nl, d, n = w_in.shape
    return pl.pallas_call(
        _in_proj_layout_kernel,
        grid=(nl, d // tk),
        in_specs=[pl.BlockSpec((None, tk, n), lambda l, i: (l, i, 0))],
        out_specs=pl.BlockSpec((None, tk, IN_COLS_PAD), lambda l, i: (l, i, 0)),
        out_shape=jax.ShapeDtypeStruct((nl, d, IN_COLS_PAD), BF16),
        compiler_params=_params(2, 32 << 20),
        name="in_proj_layout",
    )(w_in)


def _pad_heads(x, n_heads, width):
    lead = x.shape[:-1]
    x = x.reshape(lead + (n_heads, width))
    x = jnp.pad(x, [(0, 0)] * len(lead) + [(0, 0), (0, MLA_PAD - width)])
    return x.reshape(lead + (n_heads * MLA_PAD,))


def kernel(x, mem, norm_mix, w_in, na_q_norm, na_k_norm, na_rpb, mla_cq_norm, mla_w_uq, mla_ckv_norm, mla_w_ukv, mla_q_norm, mla_k_norm, swa_q_norm, swa_k_norm, swa_sink, mem_norm, mem_w_kv, mem_q_norm, mem_k_norm, w_branch, w_gate, b_gate, w_o, norm_ffn, ffn_w_up, ffn_w_down, moe_router, moe_w_up, moe_w_down):
    batch, seq, d = x.shape
    assert (seq, d) == (SEQ, D_MODEL) and mem.shape[1] == MEM_LEN
    t = batch * seq
    depth = w_in.shape[0]
    h = x.reshape(t, d)
    mem2 = mem.reshape(batch * MEM_LEN, d)
    mla_tabs = _rope_tables(MLA_PAD, MLA_ROPE, MLA_NOPE)
    swa_tabs = _rope_tables(LANES, SWA_HD, 0)
    w_in_pad = _pad_in_proj(w_in)
    wq_pad = _pad_heads(mla_w_uq, MLA_HEADS, MLA_QK).astype(BF16)
    wkv_bf = mla_w_ukv.astype(BF16)

    for l in range(depth):
        u = rmsnorm_bf16(h, norm_mix[l])
        z = ws_matmul(u, w_in_pad, l, tm=1024, tn=1024, out_dtype=F32, name="in_proj")

        o_na = na_attention(z, _na_bias_table(na_rpb[l]), na_q_norm[l], na_k_norm[l], batch)

        q_mla, k_mla, v_mla = mla_prep(
            z, wq_pad[l], wkv_bf[l], mla_cq_norm[l], mla_ckv_norm[l],
            jnp.pad(mla_q_norm[l], (0, MLA_PAD - MLA_QK)).reshape(1, MLA_PAD),
            jnp.pad(mla_k_norm[l], (0, MLA_PAD - MLA_QK)).reshape(1, MLA_PAD), mla_tabs)
        o_mla = mla_attention(q_mla, k_mla, v_mla, batch)

        o_swa = swa_attention(z, swa_sink[l], swa_q_norm[l], swa_k_norm[l], swa_tabs, batch)

        memn = rmsnorm_bf16(mem2, mem_norm[l])
        memkv = ws_matmul(memn, mem_w_kv, l, tm=1024, tn=512, out_dtype=F32, name="mem_kv")
        o_mem = mem_attention(z, memkv, mem_q_norm[l], mem_k_norm[l], batch)

        merged = gated_merge(u, (o_na, o_mla, o_swa, o_mem), w_gate, w_branch, b_gate, l)
        h = ws_matmul(merged, w_o, l, tm=512, tn=1024, out_dtype=F32, residual=h, name="out_proj")

        if l % 2 == 0:
            hn = rmsnorm_bf16(h, norm_ffn[l])
            act = ws_swiglu(hn, ffn_w_up, l // 2, D_FF, tm=1024, tn=512, name="ffn_up")
            h = ws_matmul(act, ffn_w_down, l // 2, tm=512, tn=512, out_dtype=F32, residual=h, name="ffn_down")
        else:
            h = moe_block(h, norm_ffn[l], moe_router[l // 2], moe_w_up[l // 2], moe_w_down[l // 2])
    return h.reshape(batch, seq, d)
```

```python
import functools

import jax
import jax.numpy as jnp
import numpy as np
from jax import lax
from jax.experimental import pallas as pl
from jax.experimental.pallas import tpu as pltpu

F32 = jnp.float32
BF16 = jnp.bfloat16

D_MODEL = 2048
SEQ = 2048
MEM_LEN = 256
GRID_W = 64
ROPE_THETA = 10000.0
EPS = 1e-6
NEG = -1e30

NA_HEADS = 4
NA_HD = 128
NA_WIN_R = 8
NA_WIN_C = 16
NA_QBLK = 256
NA_KWIN = 768

MLA_HEADS = 4
MLA_NOPE = 128
MLA_ROPE = 64
MLA_QK = MLA_NOPE + MLA_ROPE
MLA_PAD = 256
MLA_Q_RANK = 768
MLA_KV_RANK = 256

SWA_HEADS = 8
SWA_KV_HEADS = 2
SWA_HD = 64
SWA_BLOCK = 128
SWA_QBLK = 256
SWA_BAND = SWA_QBLK + 2 * SWA_BLOCK

MEM_HEADS = 4
MEM_HD = 128

D_FF = 5632
N_EXPERTS = 8
D_FF_EXPERT = 7168
MOE_TM = 512

LANES = 128
VMEM_CAP = 60000 * 1024

COL_NA_Q, COL_NA_K, COL_NA_V = 0, 512, 1024
COL_CQ, COL_CKV = 1536, 2304
COL_SWQ, COL_MEQ, COL_SWK, COL_SWV, COL_KPE = 2560, 3072, 3584, 3712, 3840
IN_COLS_PAD = 4096


def _params(n_axes, vmem_bytes):
    return pltpu.CompilerParams(
        dimension_semantics=("arbitrary",) * n_axes,
        vmem_limit_bytes=int(min(VMEM_CAP, vmem_bytes)))


def _rms(x, g):
    ms = jnp.mean(x * x, axis=-1, keepdims=True)
    return x * lax.rsqrt(ms + EPS) * g


def _sigmoid(x):
    return 1.0 / (1.0 + jnp.exp(-x))


def _cast_rows(src_ref, dst_ref, rows, chunk=256):
    def body(c, carry):
        r = pl.multiple_of(c * chunk, chunk)
        dst_ref[pl.ds(r, chunk), :] = src_ref[pl.ds(r, chunk), :].astype(BF16)
        return carry
    lax.fori_loop(0, rows // chunk, body, 0)


def _dot(a, b):
    return jnp.dot(a, b, preferred_element_type=F32)


def _dot_nt(a, b):
    return lax.dot_general(a, b, (((1,), (1,)), ((), ())), preferred_element_type=F32)


def _rmsnorm_kernel(x_ref, g_ref, o_ref):
    o_ref[...] = _rms(x_ref[...], g_ref[...]).astype(o_ref.dtype)


def rmsnorm_bf16(x, g, tm=512):
    m, d = x.shape
    return pl.pallas_call(
        _rmsnorm_kernel,
        grid=(m // tm,),
        in_specs=[pl.BlockSpec((tm, d), lambda i: (i, 0)),
                  pl.BlockSpec((1, d), lambda i: (0, 0))],
        out_specs=pl.BlockSpec((tm, d), lambda i: (i, 0)),
        out_shape=jax.ShapeDtypeStruct((m, d), BF16),
        compiler_params=_params(1, 4 * tm * d * 6 + (8 << 20)),
        name="rmsnorm_bf16",
    )(x, g.reshape(1, d))


def _ws_plain_kernel(x_ref, w_ref, o_ref, wb_ref):
    @pl.when(pl.program_id(1) == 0)
    def _():
        _cast_rows(w_ref, wb_ref, w_ref.shape[0])
    o_ref[...] = _dot(x_ref[...], wb_ref[...]).astype(o_ref.dtype)


def _ws_bf16wt_kernel(x_ref, wt_ref, o_ref):
    o_ref[...] = _dot_nt(x_ref[...], wt_ref[...]).astype(o_ref.dtype)


def _ws_residual_kernel(x_ref, w_ref, r_ref, o_ref, wb_ref):
    @pl.when(pl.program_id(1) == 0)
    def _():
        _cast_rows(w_ref, wb_ref, w_ref.shape[0])
    o_ref[...] = r_ref[...] + _dot(x_ref[...], wb_ref[...])


def _ws_swiglu_kernel(x_ref, wg_ref, wu_ref, o_ref, wgb_ref, wub_ref):
    @pl.when(pl.program_id(1) == 0)
    def _():
        _cast_rows(wg_ref, wgb_ref, wg_ref.shape[0])
        _cast_rows(wu_ref, wub_ref, wu_ref.shape[0])
    x = x_ref[...]
    g = _dot(x, wgb_ref[...])
    u = _dot(x, wub_ref[...])
    o_ref[...] = (g * _sigmoid(g) * u).astype(o_ref.dtype)


def ws_matmul(x, w, layer, *, tm, tn, out_dtype, residual=None, name):
    m, k = x.shape
    n = w.shape[2]
    in_specs = [pl.BlockSpec((tm, k), lambda j, i: (i, 0)),
                pl.BlockSpec((None, k, tn), lambda j, i: (layer, 0, j))]
    args = [x, w]
    kern = _ws_plain_kernel
    scratch = [pltpu.VMEM((k, tn), BF16)]
    if residual is not None:
        in_specs.append(pl.BlockSpec((tm, tn), lambda j, i: (i, j)))
        args.append(residual)
        kern = _ws_residual_kernel
    elif w.dtype == BF16:
        n = w.shape[1]
        in_specs[1] = pl.BlockSpec((None, tn, k), lambda j, i: (layer, j, 0))
        kern, scratch = _ws_bf16wt_kernel, []
    vmem = 2 * (tm * k * 2 + k * tn * 4 + 2 * tm * tn * 4) + k * tn * 2 + tm * tn * 4 + (6 << 20)
    return pl.pallas_call(
        kern,
        grid=(n // tn, m // tm),
        in_specs=in_specs,
        out_specs=pl.BlockSpec((tm, tn), lambda j, i: (i, j)),
        out_shape=jax.ShapeDtypeStruct((m, n), out_dtype),
        scratch_shapes=scratch,
        compiler_params=_params(2, vmem),
        name=name,
    )(*args)


def ws_swiglu(x, w_up, layer, d_ff, *, tm, tn, name):
    m, k = x.shape
    nb = d_ff // tn
    vmem = 2 * (tm * k * 2 + 2 * k * tn * 4 + tm * tn * 2) + 2 * k * tn * 2 + 3 * tm * tn * 4 + (6 << 20)
    return pl.pallas_call(
        _ws_swiglu_kernel,
        grid=(nb, m // tm),
        in_specs=[pl.BlockSpec((tm, k), lambda j, i: (i, 0)),
                  pl.BlockSpec((None, k, tn), lambda j, i: (layer, 0, j)),
                  pl.BlockSpec((None, k, tn), lambda j, i: (layer, 0, j + nb))],
        out_specs=pl.BlockSpec((tm, tn), lambda j, i: (i, j)),
        out_shape=jax.ShapeDtypeStruct((m, d_ff), BF16),
        scratch_shapes=[pltpu.VMEM((k, tn), BF16), pltpu.VMEM((k, tn), BF16)],
        compiler_params=_params(2, vmem),
        name=name,
    )(x, w_up, w_up)


NA_ROWS = SEQ // GRID_W
NA_NBLK = SEQ // NA_QBLK
NA_QROWS = NA_QBLK // GRID_W
NA_KROWS = NA_KWIN // GRID_W
assert NA_KROWS >= NA_QROWS + NA_WIN_R - 1 and (NA_ROWS - NA_KROWS) % 2 == 0 and NA_QROWS % 2 == 0


def _na_key_start_row(blk):
    lo, hi = 0, NA_ROWS - NA_KROWS
    start = NA_QROWS * blk - NA_WIN_R // 2
    return jnp.clip(start, lo, hi) if isinstance(blk, jax.Array) else min(max(start, lo), hi)


def _na_block_geometry(blk):
    geo = []
    for a in range(NA_QROWS):
        qr = NA_QROWS * blk + a
        rs = min(max(qr - NA_WIN_R // 2, 0), NA_ROWS - NA_WIN_R)
        for c in range(NA_KROWS):
            kr = _na_key_start_row(blk) + c
            geo.append(kr - qr + NA_WIN_R - 1 if rs <= kr < rs + NA_WIN_R else None)
    return tuple(geo)


NA_GEOMETRIES = tuple(dict.fromkeys(_na_block_geometry(b) for b in range(NA_NBLK)))
NA_BLOCK_PATTERN = tuple(NA_GEOMETRIES.index(_na_block_geometry(b)) for b in range(NA_NBLK))


def _na_pattern(i):
    res = NA_BLOCK_PATTERN[-1]
    for b in reversed(range(NA_NBLK - 1)):
        res = jnp.where(i == b, NA_BLOCK_PATTERN[b], res)
    return res


def _na_bias_table(rpb):
    hh = rpb.shape[0]
    span = GRID_W - 1
    left = span - (NA_WIN_C - 1)
    right = 2 * span + 1 - left - (2 * NA_WIN_C - 1)
    ext = jnp.pad(rpb, ((0, 0), (0, 0), (left, right)), constant_values=NEG)
    toep = jnp.stack([ext[:, :, span - qc:span - qc + GRID_W] for qc in range(GRID_W)], axis=2)
    qc = np.arange(GRID_W)
    cs = np.clip(qc - NA_WIN_C // 2, 0, GRID_W - NA_WIN_C)
    kc = np.arange(GRID_W)
    col_ok = (kc[None, :] >= cs[:, None]) & (kc[None, :] < cs[:, None] + NA_WIN_C)
    toep = jnp.where(col_ok[None, None], toep, NEG)
    neg = jnp.full((hh, GRID_W, GRID_W), NEG, rpb.dtype)
    patterns = []
    for geo in NA_GEOMETRIES:
        q_rows = []
        for a in range(NA_QROWS):
            offs = geo[a * NA_KROWS:(a + 1) * NA_KROWS]
            k_blocks = [neg if d is None else toep[:, d] for d in offs]
            q_rows.append(jnp.concatenate(k_blocks, axis=-1))
        patterns.append(jnp.concatenate(q_rows, axis=-2))
    return jnp.stack(patterns, axis=0)


def _na_kernel(q_ref, k_ref, v_ref, bias_ref, gq_ref, gk_ref, o_ref, kn_ref, vb_ref):
    i = pl.program_id(1)

    @pl.when(i == 0)
    def _():
        def body(c, carry):
            r = pl.multiple_of(c * 256, 256)
            for h in range(NA_HEADS):
                hs = slice(h * NA_HD, (h + 1) * NA_HD)
                kn_ref[pl.ds(r, 256), hs] = _rms(k_ref[pl.ds(r, 256), hs], gk_ref[...]).astype(BF16)
            vb_ref[pl.ds(r, 256), :] = v_ref[pl.ds(r, 256), :].astype(BF16)
            return carry
        lax.fori_loop(0, SEQ // 256, body, 0)

    start = pl.multiple_of(_na_key_start_row(i) * GRID_W, 2 * GRID_W)
    scale = NA_HD ** -0.5
    for h in range(NA_HEADS):
        hs = slice(h * NA_HD, (h + 1) * NA_HD)
        q = (_rms(q_ref[:, hs], gq_ref[...]) * scale).astype(BF16)
        s = _dot_nt(q, kn_ref[pl.ds(start, NA_KWIN), hs]) + bias_ref[0, h]
        m = jnp.max(s, axis=-1, keepdims=True)
        p = jnp.exp(s - m)
        l = jnp.sum(p, axis=-1, keepdims=True)
        o = _dot(p.astype(BF16), vb_ref[pl.ds(start, NA_KWIN), hs]) / l
        o_ref[:, hs] = o.astype(o_ref.dtype)


def na_attention(z, bias, gq, gk, batch):
    nblk = SEQ // NA_QBLK
    w = NA_HEADS * NA_HD
    vmem = (2 * (NA_QBLK * w * 4 + 2 * SEQ * w * 4 + NA_HEADS * NA_QBLK * NA_KWIN * 4 + NA_QBLK * w * 2)
            + 2 * SEQ * w * 2 + (8 << 20))
    return pl.pallas_call(
        _na_kernel,
        grid=(batch, nblk),
        in_specs=[pl.BlockSpec((NA_QBLK, w), lambda b, i: (b * nblk + i, COL_NA_Q // w)),
                  pl.BlockSpec((SEQ, w), lambda b, i: (b, COL_NA_K // w)),
                  pl.BlockSpec((SEQ, w), lambda b, i: (b, COL_NA_V // w)),
                  pl.BlockSpec((1, NA_HEADS, NA_QBLK, NA_KWIN), lambda b, i: (_na_pattern(i), 0, 0, 0)),
                  pl.BlockSpec((1, NA_HD), lambda b, i: (0, 0)),
                  pl.BlockSpec((1, NA_HD), lambda b, i: (0, 0))],
        out_specs=pl.BlockSpec((NA_QBLK, w), lambda b, i: (b * nblk + i, 0)),
        out_shape=jax.ShapeDtypeStruct((batch * SEQ, w), BF16),
        scratch_shapes=[pltpu.VMEM((SEQ, w), BF16), pltpu.VMEM((SEQ, w), BF16)],
        compiler_params=_params(2, vmem),
        name="na_attention",
    )(z, z, z, bias, gq.reshape(1, NA_HD), gk.reshape(1, NA_HD))


def _rope_tables(width, head_dim, first_lane):
    half = head_dim // 2
    freqs = ROPE_THETA ** (-2.0 * np.arange(half, dtype=np.float32) / head_dim)
    ang = jnp.arange(SEQ, dtype=F32)[:, None] * jnp.asarray(freqs, F32)[None, :]
    cos, sin = jnp.cos(ang), jnp.sin(ang)
    lane = np.arange(width)
    rel = lane - first_lane
    in_rope = (rel >= 0) & (rel < (width - first_lane if first_lane == 0 else head_dim))
    p = np.where(in_rope, rel % head_dim, 0)
    j = p % half
    first_half = in_rope & (p < half)
    second_half = in_rope & (p >= half)
    cos_t = jnp.where(in_rope[None, :], cos[:, j], 1.0)
    sin_a = jnp.where(first_half[None, :], -sin[:, j], 0.0)
    sin_b = jnp.where(second_half[None, :], sin[:, j], 0.0)
    return cos_t, sin_a, sin_b


def _apply_rope(x, cos_t, sin_a, sin_b, half):
    n = x.shape[-1]
    return (x * cos_t + pltpu.roll(x, n - half, axis=1) * sin_a
            + pltpu.roll(x, half, axis=1) * sin_b)


def _mla_prep_kernel(cq_ref, ckv_ref, kpe_ref, wq_ref, wkv_ref, gcq_ref, gckv_ref,
                     gqn_ref, gkn_ref, cos_ref, sa_ref, sb_ref, q_ref, k_ref, v_ref):
    half = MLA_ROPE // 2
    cos_t, sin_a, sin_b = cos_ref[...], sa_ref[...], sb_ref[...]
    scale = MLA_QK ** -0.5
    cq = _rms(cq_ref[...], gcq_ref[...]).astype(BF16)
    q_raw = _dot(cq, wq_ref[...])
    ckv = _rms(ckv_ref[...], gckv_ref[...]).astype(BF16)
    kv_raw = _dot(ckv, wkv_ref[...])
    kpe = kpe_ref[...]
    kpe_ss = jnp.sum(kpe * kpe, axis=-1, keepdims=True)
    for h in range(MLA_HEADS):
        cs = slice(h * MLA_PAD, (h + 1) * MLA_PAD)
        qc = q_raw[:, cs]
        ms = jnp.sum(qc * qc, axis=-1, keepdims=True) * (1.0 / MLA_QK)
        qn = qc * lax.rsqrt(ms + EPS) * gqn_ref[...]
        q_ref[:, cs] = (_apply_rope(qn, cos_t, sin_a, sin_b, half) * scale).astype(BF16)
        kn = kv_raw[:, h * MLA_PAD:h * MLA_PAD + MLA_NOPE]
        ms = (jnp.sum(kn * kn, axis=-1, keepdims=True) + kpe_ss) * (1.0 / MLA_QK)
        kc = jnp.concatenate([kn, kpe], axis=-1) * lax.rsqrt(ms + EPS) * gkn_ref[...]
        k_ref[:, cs] = _apply_rope(kc, cos_t, sin_a, sin_b, half).astype(BF16)
        v_ref[:, h * MLA_NOPE:(h + 1) * MLA_NOPE] = kv_raw[:, h * MLA_PAD + MLA_NOPE:(h + 1) * MLA_PAD].astype(BF16)


def mla_prep(z, wq_pad, wkv, gcq, gckv, gqn_pad, gkn_pad, tabs, tm=512):
    t = z.shape[0]
    sb = SEQ // tm
    qw = MLA_HEADS * MLA_PAD
    row = lambda i: (i, 0)
    const = lambda i: (0, 0)
    pos = lambda i: (i % sb, 0)
    return pl.pallas_call(
        _mla_prep_kernel,
        grid=(t // tm,),
        in_specs=[pl.BlockSpec((tm, MLA_Q_RANK), lambda i: (i, COL_CQ // MLA_Q_RANK)),
                  pl.BlockSpec((tm, MLA_KV_RANK), lambda i: (i, COL_CKV // MLA_KV_RANK)),
                  pl.BlockSpec((tm, LANES), lambda i: (i, COL_KPE // LANES)),
                  pl.BlockSpec((MLA_Q_RANK, qw), const),
                  pl.BlockSpec((MLA_KV_RANK, qw), const),
                  pl.BlockSpec((1, MLA_Q_RANK), const),
                  pl.BlockSpec((1, MLA_KV_RANK), const),
                  pl.BlockSpec((1, MLA_PAD), const),
                  pl.BlockSpec((1, MLA_PAD), const),
                  pl.BlockSpec((tm, MLA_PAD), pos),
                  pl.BlockSpec((tm, MLA_PAD), pos),
                  pl.BlockSpec((tm, MLA_PAD), pos)],
        out_specs=[pl.BlockSpec((tm, qw), row), pl.BlockSpec((tm, qw), row),
                   pl.BlockSpec((tm, MLA_HEADS * MLA_NOPE), row)],
        out_shape=[jax.ShapeDtypeStruct((t, qw), BF16), jax.ShapeDtypeStruct((t, qw), BF16),
                   jax.ShapeDtypeStruct((t, MLA_HEADS * MLA_NOPE), BF16)],
        compiler_params=_params(1, 40 << 20),
        name="mla_prep",
    )(z, z, z, wq_pad, wkv, gcq.reshape(1, -1), gckv.reshape(1, -1), gqn_pad, gkn_pad, *tabs)


def _mla_attn_kernel(q_ref, k_ref, v_ref, o_ref):
    for h in range(MLA_HEADS):
        cs = slice(h * MLA_PAD, (h + 1) * MLA_PAD)
        vs = slice(h * MLA_NOPE, (h + 1) * MLA_NOPE)
        s = _dot_nt(q_ref[:, cs], k_ref[:, cs])
        m = jnp.max(s, axis=-1, keepdims=True)
        p = jnp.exp(s - m)
        l = jnp.sum(p, axis=-1, keepdims=True)
        o = _dot(p.astype(BF16), v_ref[:, vs]) / l
        o_ref[:, vs] = o.astype(o_ref.dtype)


def mla_attention(q, k, v, batch, tq=512):
    nq = SEQ // tq
    qw = MLA_HEADS * MLA_PAD
    vw = MLA_HEADS * MLA_NOPE
    vmem = 2 * (tq * qw * 2 + SEQ * qw * 2 + SEQ * vw * 2 + tq * vw * 2) + 4 * tq * SEQ * 4 + (8 << 20)
    return pl.pallas_call(
        _mla_attn_kernel,
        grid=(batch, nq),
        in_specs=[pl.BlockSpec((tq, qw), lambda b, i: (b * nq + i, 0)),
                  pl.BlockSpec((SEQ, qw), lambda b, i: (b, 0)),
                  pl.BlockSpec((SEQ, vw), lambda b, i: (b, 0))],
        out_specs=pl.BlockSpec((tq, vw), lambda b, i: (b * nq + i, 0)),
        out_shape=jax.ShapeDtypeStruct((batch * SEQ, vw), BF16),
        compiler_params=_params(2, vmem),
        name="mla_attention",
    )(q, k, v)


def _rms_halves(x, g):
    lo = lax.broadcasted_iota(jnp.int32, x.shape, 1) < SWA_HD
    x2 = x * x
    s_lo = jnp.sum(jnp.where(lo, x2, 0.0), axis=-1, keepdims=True)
    s_hi = jnp.sum(jnp.where(lo, 0.0, x2), axis=-1, keepdims=True)
    ms = jnp.where(lo, s_lo, s_hi) * (1.0 / SWA_HD)
    return x * lax.rsqrt(ms + EPS) * g


def _swa_kernel(sink_ref, q_ref, k_ref, v_ref, gq_ref, gk_ref, cosq_ref, saq_ref, sbq_ref,
                cosk_ref, sak_ref, sbk_ref, o_ref, kk_ref, vv_ref):
    n = pl.program_id(1)
    nblk = SEQ // SWA_QBLK
    half = SWA_HD // 2

    @pl.when(n == 0)
    def _():
        zeros = jnp.zeros((SWA_BLOCK, LANES), BF16)
        for c in range(4):
            kk_ref[c, pl.ds(0, SWA_BLOCK), :] = zeros
            kk_ref[c, pl.ds(SEQ + SWA_BLOCK, SWA_BLOCK), :] = zeros
            vv_ref[c, pl.ds(0, SWA_BLOCK), :] = zeros
            vv_ref[c, pl.ds(SEQ + SWA_BLOCK, SWA_BLOCK), :] = zeros

        def body(c, carry):
            r = pl.multiple_of(c * 256, 256)
            dst = pl.ds(r + SWA_BLOCK, 256)
            lo = lax.broadcasted_iota(jnp.int32, (256, LANES), 1) < SWA_HD
            kr = _apply_rope(_rms_halves(k_ref[pl.ds(r, 256), :], gk_ref[...]),
                             cosk_ref[pl.ds(r, 256), :], sak_ref[pl.ds(r, 256), :],
                             sbk_ref[pl.ds(r, 256), :], half)
            ks = pltpu.roll(kr, SWA_HD, axis=1)
            kk_ref[0, dst, :] = jnp.where(lo, kr, 0.0).astype(BF16)
            kk_ref[1, dst, :] = jnp.where(lo, 0.0, ks).astype(BF16)
            kk_ref[2, dst, :] = jnp.where(lo, ks, 0.0).astype(BF16)
            kk_ref[3, dst, :] = jnp.where(lo, 0.0, kr).astype(BF16)
            vr = v_ref[pl.ds(r, 256), :]
            vs = pltpu.roll(vr, SWA_HD, axis=1)
            vv_ref[0, dst, :] = jnp.where(lo, vr, 0.0).astype(BF16)
            vv_ref[1, dst, :] = jnp.where(lo, 0.0, vs).astype(BF16)
            vv_ref[2, dst, :] = jnp.where(lo, vs, 0.0).astype(BF16)
            vv_ref[3, dst, :] = jnp.where(lo, 0.0, vr).astype(BF16)
            return carry
        lax.fori_loop(0, SEQ // 256, body, 0)

    band = pl.ds(pl.multiple_of(n * SWA_QBLK, SWA_QBLK), SWA_BAND)
    a = lax.broadcasted_iota(jnp.int32, (SWA_QBLK, SWA_BAND), 0)
    c = lax.broadcasted_iota(jnp.int32, (SWA_QBLK, SWA_BAND), 1)
    c_min = jnp.where(n == 0, SWA_BLOCK, 0)
    c_max = jnp.where(n == nblk - 1, SWA_BAND - SWA_BLOCK, SWA_BAND)
    valid = (c >= a) & (c <= a + 2 * SWA_BLOCK) & (c >= c_min) & (c < c_max)
    scale = SWA_HD ** -0.5
    for pair in range(SWA_HEADS // 2):
        ps = slice(pair * LANES, (pair + 1) * LANES)
        grp = pair // 2
        q = _apply_rope(_rms_halves(q_ref[:, ps], gq_ref[...]),
                        cosq_ref[...], saq_ref[...], sbq_ref[...], half)
        q = (q * scale).astype(BF16)
        acc = None
        for hf in range(2):
            sink = sink_ref[2 * pair + hf]
            s = _dot_nt(q, kk_ref[2 * grp + hf, band, :])
            s = jnp.where(valid, s, NEG)
            m = jnp.maximum(jnp.max(s, axis=-1, keepdims=True), sink)
            p = jnp.exp(s - m)
            den = jnp.sum(p, axis=-1, keepdims=True) + jnp.exp(sink - m)
            o = _dot(p.astype(BF16), vv_ref[2 * grp + hf, band, :]) / den
            acc = o if acc is None else acc + o
        o_ref[:, ps] = acc.astype(o_ref.dtype)


def swa_attention(z, sink, gq, gk, tabs, batch):
    nblk = SEQ // SWA_QBLK
    qw = SWA_HEADS * SWA_HD
    cos_t, sin_a, sin_b = tabs
    g2 = lambda g: jnp.concatenate([g, g]).reshape(1, LANES)
    blk = lambda b, n: (n, 0)
    full = lambda b, n: (0, 0)
    tab_blk = pl.BlockSpec((SWA_QBLK, LANES), blk)
    tab_full = pl.BlockSpec((SEQ, LANES), full)
    vmem = (2 * (SWA_QBLK * qw * 6 + 2 * SEQ * LANES * 4 + 3 * SEQ * LANES * 4 + 3 * SWA_QBLK * LANES * 4)
            + 8 * (SEQ + 2 * SWA_BLOCK) * LANES * 2 + 8 * SWA_QBLK * SWA_BAND * 4 + (8 << 20))
    return pl.pallas_call(
        _swa_kernel,
        grid=(batch, nblk),
        in_specs=[pl.BlockSpec(memory_space=pltpu.SMEM),
                  pl.BlockSpec((SWA_QBLK, qw), lambda b, n: (b * nblk + n, COL_SWQ // qw)),
                  pl.BlockSpec((SEQ, LANES), lambda b, n: (b, COL_SWK // LANES)),
                  pl.BlockSpec((SEQ, LANES), lambda b, n: (b, COL_SWV // LANES)),
                  pl.BlockSpec((1, LANES), full), pl.BlockSpec((1, LANES), full),
                  tab_blk, tab_blk, tab_blk, tab_full, tab_full, tab_full],
        out_specs=pl.BlockSpec((SWA_QBLK, qw), lambda b, n: (b * nblk + n, 0)),
        out_shape=jax.ShapeDtypeStruct((batch * SEQ, qw), BF16),
        scratch_shapes=[pltpu.VMEM((4, SEQ + 2 * SWA_BLOCK, LANES), BF16),
                        pltpu.VMEM((4, SEQ + 2 * SWA_BLOCK, LANES), BF16)],
        compiler_params=_params(2, vmem),
        name="swa_attention",
    )(sink, z, z, z, g2(gq), g2(gk), cos_t, sin_a, sin_b, cos_t, sin_a, sin_b)


def _mem_attn_kernel(q_ref, k_ref, v_ref, gq_ref, gk_ref, o_ref):
    scale = MEM_HD ** -0.5
    for h in range(MEM_HEADS):
        hs = slice(h * MEM_HD, (h + 1) * MEM_HD)
        q = (_rms(q_ref[:, hs], gq_ref[...]) * scale).astype(BF16)
        k = _rms(k_ref[:, hs], gk_ref[...]).astype(BF16)
        s = _dot_nt(q, k)
        m = jnp.max(s, axis=-1, keepdims=True)
        p = jnp.exp(s - m)
        l = jnp.sum(p, axis=-1, keepdims=True)
        o = _dot(p.astype(BF16), v_ref[:, hs].astype(BF16)) / l
        o_ref[:, hs] = o.astype(o_ref.dtype)


def mem_attention(z, memkv, gq, gk, batch, tq=512):
    nq = SEQ // tq
    w = MEM_HEADS * MEM_HD
    return pl.pallas_call(
        _mem_attn_kernel,
        grid=(batch, nq),
        in_specs=[pl.BlockSpec((tq, w), lambda b, i: (b * nq + i, COL_MEQ // w)),
                  pl.BlockSpec((MEM_LEN, w), lambda b, i: (b, 0)),
                  pl.BlockSpec((MEM_LEN, w), lambda b, i: (b, 1)),
                  pl.BlockSpec((1, MEM_HD), lambda b, i: (0, 0)),
                  pl.BlockSpec((1, MEM_HD), lambda b, i: (0, 0))],
        out_specs=pl.BlockSpec((tq, w), lambda b, i: (b * nq + i, 0)),
        out_shape=jax.ShapeDtypeStruct((batch * SEQ, w), BF16),
        compiler_params=_params(2, 24 << 20),
        name="mem_attention",
    )(z, memkv, memkv, gq.reshape(1, MEM_HD), gk.reshape(1, MEM_HD))


def _merge_kernel(layer, u_ref, b0_ref, b1_ref, b2_ref, b3_ref, wg_hbm, wb_ref, bg_ref, o_ref,
                  stage_ref, wgs_ref, wbs_ref, sem):
    branches = (b0_ref, b1_ref, b2_ref, b3_ref)
    j, i = pl.program_id(0), pl.program_id(1)
    tn = wgs_ref.shape[2]
    slot = j % 2

    def copies(jj, s):
        return [pltpu.make_async_copy(wg_hbm.at[layer, :, n, pl.ds(pl.multiple_of(jj * tn, tn), tn)],
                                      stage_ref.at[s, n], sem.at[s]) for n in range(4)]

    @pl.when(jnp.logical_and(j == 0, i == 0))
    def _():
        for c in copies(0, 0):
            c.start()

    @pl.when(i == 0)
    def _():
        for c in copies(j, slot):
            c.wait()

        @pl.when(j + 1 < pl.num_programs(0))
        def _():
            for c in copies(j + 1, 1 - slot):
                c.start()
        for n in range(4):
            _cast_rows(stage_ref.at[slot, n], wgs_ref.at[n], D_MODEL)
            wbs_ref[n] = wb_ref[n].astype(BF16)

    u = u_ref[...]
    acc = None
    for n in range(4):
        gate = _sigmoid(_dot(u, wgs_ref[n]) + bg_ref[n:n + 1, :])
        term = gate * _dot(branches[n][...], wbs_ref[n])
        acc = term if acc is None else acc + term
    o_ref[...] = acc.astype(o_ref.dtype)


def gated_merge(u, branches, w_gate, w_branch, b_gate, layer, tm=1024, tn=256):
    t = u.shape[0]
    bw = branches[0].shape[1]
    vmem = (2 * (tm * D_MODEL * 2 + 4 * tm * bw * 2 + 4 * D_MODEL * tn * 4 + 4 * bw * tn * 4 + tm * tn * 2)
            + 4 * D_MODEL * tn * 2 + 4 * bw * tn * 2 + 4 * tm * tn * 4 + (6 << 20))
    return pl.pallas_call(
        functools.partial(_merge_kernel, layer),
        grid=(D_MODEL // tn, t // tm),
        in_specs=[pl.BlockSpec((tm, D_MODEL), lambda j, i: (i, 0))]
        + [pl.BlockSpec((tm, bw), lambda j, i: (i, 0)) for _ in range(4)]
        + [pl.BlockSpec(memory_space=pl.ANY),
           pl.BlockSpec((None, 4, bw, tn), lambda j, i: (layer, 0, 0, j)),
           pl.BlockSpec((None, 4, tn), lambda j, i: (layer, 0, j))],
        out_specs=pl.BlockSpec((tm, tn), lambda j, i: (i, j)),
        out_shape=jax.ShapeDtypeStruct((t, D_MODEL), BF16),
        scratch_shapes=[pltpu.VMEM((2, 4, D_MODEL, tn), F32), pltpu.VMEM((4, D_MODEL, tn), BF16),
                        pltpu.VMEM((4, bw, tn), BF16), pltpu.SemaphoreType.DMA((2,))],
        compiler_params=_params(2, vmem),
        name="gated_merge",
    )(u, *branches, w_gate, w_branch, b_gate)


def _router_kernel(h_ref, g_ref, wr_ref, ri_ref, rw_ref, cnt_ref, carry_ref):
    @pl.when(pl.program_id(0) == 0)
    def _():
        carry_ref[...] = jnp.zeros_like(carry_ref)

    tm = h_ref.shape[0]
    hn = _rms(h_ref[...], g_ref[...])
    logits = jnp.dot(hn, wr_ref[...], precision=lax.Precision.HIGHEST, preferred_element_type=F32)
    lane = lax.broadcasted_iota(jnp.int32, (tm, LANES), 1)
    lane_f = lane.astype(F32)
    logits = jnp.where(lane < N_EXPERTS, logits, -jnp.inf)
    m1 = jnp.max(logits, axis=-1, keepdims=True)
    i1 = jnp.min(jnp.where(logits == m1, lane_f, float(LANES)), axis=-1, keepdims=True)
    oh1 = lane_f == i1
    rest = jnp.where(oh1, -jnp.inf, logits)
    m2 = jnp.max(rest, axis=-1, keepdims=True)
    i2 = jnp.min(jnp.where(rest == m2, lane_f, float(LANES)), axis=-1, keepdims=True)
    oh2 = lane_f == i2
    e2 = jnp.exp(m2 - m1)
    w1 = 1.0 / (1.0 + e2)
    w2 = e2 / (1.0 + e2)
    chosen = jnp.where(oh1 | oh2, 1.0, 0.0)
    before = (lax.broadcasted_iota(jnp.int32, (tm, tm), 1)
              < lax.broadcasted_iota(jnp.int32, (tm, tm), 0))
    prefix = _dot(jnp.where(before, 1.0, 0.0).astype(BF16), chosen.astype(BF16)) + carry_ref[...]
    r1 = jnp.sum(jnp.where(oh1, prefix, 0.0), axis=-1, keepdims=True).astype(jnp.int32)
    r2 = jnp.sum(jnp.where(oh2, prefix, 0.0), axis=-1, keepdims=True).astype(jnp.int32)
    carry_ref[...] += jnp.sum(chosen, axis=0, keepdims=True)
    ri_ref[...] = jnp.where(lane == 0, i1.astype(jnp.int32),
                            jnp.where(lane == 1, i2.astype(jnp.int32),
                                      jnp.where(lane == 2, r1, jnp.where(lane == 3, r2, 0))))
    rw_ref[...] = jnp.where(lane == 0, w1, jnp.where(lane == 1, w2, 0.0))
    cnt_ref[...] = carry_ref[...]


def moe_router(h, g, w_router, tm=512):
    t, d = h.shape
    wr = jnp.pad(w_router, ((0, 0), (0, LANES - N_EXPERTS)))
    return pl.pallas_call(
        _router_kernel,
        grid=(t // tm,),
        in_specs=[pl.BlockSpec((tm, d), lambda i: (i, 0)),
                  pl.BlockSpec((1, d), lambda i: (0, 0)),
                  pl.BlockSpec((d, LANES), lambda i: (0, 0))],
        out_specs=[pl.BlockSpec((tm, LANES), lambda i: (i, 0)),
                   pl.BlockSpec((tm, LANES), lambda i: (i, 0)),
                   pl.BlockSpec((1, LANES), lambda i: (0, 0))],
        out_shape=[jax.ShapeDtypeStruct((t, LANES), jnp.int32),
                   jax.ShapeDtypeStruct((t, LANES), F32),
                   jax.ShapeDtypeStruct((1, LANES), F32)],
        scratch_shapes=[pltpu.VMEM((1, LANES), F32)],
        compiler_params=_params(1, 32 << 20),
        name="moe_router",
    )(h, g.reshape(1, d), wr)


ROW_DMA_UNROLL = 8


def _row_copy(src_hbm, row, dst_vmem, r, sem):
    return pltpu.make_async_copy(src_hbm.at[pl.ds(row, 1), :], dst_vmem.at[pl.ds(r, 1), :], sem)


def _moe_scatter_kernel(pos_ref, h_ref, g_ref, xs_in_hbm, xs_hbm, pk_ref, sem):
    del xs_in_hbm
    tg = h_ref.shape[0]
    i = pl.program_id(0)
    last = pl.num_programs(0) - 1
    slot = i % 2

    def row_copy(s, r, dst_row):
        return pltpu.make_async_copy(pk_ref.at[s, pl.ds(r, 1), :], xs_hbm.at[pl.ds(dst_row, 1), :], sem.at[s])

    def wait_slot(s):
        def drain(r, carry):
            row_copy(s, r, 0).wait()
            row_copy(s, r, 0).wait()
            return carry
        lax.fori_loop(0, tg, drain, 0, unroll=8)

    @pl.when(i >= 2)
    def _():
        wait_slot(slot)

    pk_ref[slot] = _rms(h_ref[...], g_ref[...])

    def issue(grp, carry):
        for k in range(ROW_DMA_UNROLL):
            r = grp * ROW_DMA_UNROLL + k
            t = i * tg + r
            row_copy(slot, r, pos_ref[2 * t]).start(priority=0)
            row_copy(slot, r, pos_ref[2 * t + 1]).start(priority=1)
        return carry
    lax.fori_loop(0, tg // ROW_DMA_UNROLL, issue, 0)

    @pl.when(jnp.logical_and(i == last, i >= 1))
    def _():
        wait_slot(1 - slot)

    @pl.when(i == last)
    def _():
        wait_slot(slot)


def _cast_kernel(x_ref, o_ref):
    o_ref[...] = x_ref[...].astype(o_ref.dtype)


def cast_bf16(x, tm=512):
    m, d = x.shape
    return pl.pallas_call(
        _cast_kernel,
        grid=(m // tm,),
        in_specs=[pl.BlockSpec((tm, d), lambda i: (i, 0))],
        out_specs=pl.BlockSpec((tm, d), lambda i: (i, 0)),
        out_shape=jax.ShapeDtypeStruct((m, d), BF16),
        compiler_params=_params(1, 32 << 20),
        name="cast_bf16",
    )(x)


def moe_scatter(pos_flat, h, g, rows, tg=256):
    t, d = h.shape
    xs0 = jnp.zeros((rows, d), F32)
    return pl.pallas_call(
        _moe_scatter_kernel,
        grid_spec=pltpu.PrefetchScalarGridSpec(
            num_scalar_prefetch=1,
            grid=(t // tg,),
            in_specs=[pl.BlockSpec((tg, d), lambda i, p: (i, 0)),
                      pl.BlockSpec((1, d), lambda i, p: (0, 0)),
                      pl.BlockSpec(memory_space=pl.ANY)],
            out_specs=pl.BlockSpec(memory_space=pl.ANY),
            scratch_shapes=[pltpu.VMEM((2, tg, d), F32), pltpu.SemaphoreType.DMA((2,))]),
        out_shape=jax.ShapeDtypeStruct((rows, d), F32),
        input_output_aliases={3: 0},
        compiler_params=_params(1, 24 << 20),
        name="moe_scatter",
    )(pos_flat, h, g.reshape(1, d), xs0)


def _stream_segment_weights(te_ref, first_ref, nxt_ref, meta_ref, part_copy, cast_part, n_parts):
    j, i = pl.program_id(0), pl.program_id(1)

    @pl.when(jnp.logical_and(jnp.logical_and(j == 0, i == 0), meta_ref[0] > 0))
    def _():
        for part in range(n_parts):
            part_copy(0, te_ref[0], part).start()

    @pl.when(first_ref[i] == 1)
    def _():
        wraps = nxt_ref[i] < 0
        j_next = jnp.where(wraps, j + 1, j)
        e_next = jnp.where(wraps, meta_ref[1], nxt_ref[i])
        for part in range(n_parts):
            part_copy(j, te_ref[i], part).wait()
            cast_part(part)

            @pl.when(j_next < pl.num_programs(0))
            def _():
                part_copy(j_next, e_next, part).start()


def _moe_up_kernel(te_ref, first_ref, nxt_ref, meta_ref, x_ref, w_hbm, o_ref,
                   stage_ref, wgb_ref, wub_ref, sem):
    i = pl.program_id(1)
    active = i < meta_ref[0]
    tf = wgb_ref.shape[1]
    nf = pl.num_programs(0)
    operands = (wgb_ref, wub_ref)

    def part_copy(jj, e, part):
        return pltpu.make_async_copy(w_hbm.at[e, :, pl.ds(pl.multiple_of((jj + part * nf) * tf, tf), tf)],
                                     stage_ref.at[part], sem.at[part])

    def cast_part(part):
        _cast_rows(stage_ref.at[part], operands[part], D_MODEL)

    _stream_segment_weights(te_ref, first_ref, nxt_ref, meta_ref, part_copy, cast_part, 2)

    @pl.when(active)
    def _():
        x = x_ref[...]
        g = _dot(x, wgb_ref[...])
        u = _dot(x, wub_ref[...])
        o_ref[...] = (g * _sigmoid(g) * u).astype(o_ref.dtype)

    @pl.when(jnp.logical_not(active))
    def _():
        o_ref[...] = jnp.zeros_like(o_ref)


def _active_row(j, i, te, first, nxt, meta):
    return jnp.maximum(jnp.minimum(i, meta[0] - 1), 0)


def moe_up(xs, w_up, sched, tf=1024):
    rows, d = xs.shape
    nf = D_FF_EXPERT // tf
    vmem = (2 * (MOE_TM * d * 2 + MOE_TM * tf * 2) + 2 * d * tf * 4 + 2 * d * tf * 2
            + 4 * MOE_TM * tf * 4 + (6 << 20))
    return pl.pallas_call(
        _moe_up_kernel,
        grid_spec=pltpu.PrefetchScalarGridSpec(
            num_scalar_prefetch=4,
            grid=(nf, rows // MOE_TM),
            in_specs=[pl.BlockSpec((MOE_TM, d), lambda j, i, *s: (_active_row(j, i, *s), 0)),
                      pl.BlockSpec(memory_space=pl.ANY)],
            out_specs=pl.BlockSpec((MOE_TM, tf), lambda j, i, *s: (i, j)),
            scratch_shapes=[pltpu.VMEM((2, d, tf), F32), pltpu.VMEM((d, tf), BF16),
                            pltpu.VMEM((d, tf), BF16), pltpu.SemaphoreType.DMA((2,))]),
        out_shape=jax.ShapeDtypeStruct((rows, D_FF_EXPERT), BF16),
        compiler_params=_params(2, vmem),
        name="moe_up",
    )(*sched, xs, w_up)


def _moe_down_kernel(te_ref, first_ref, nxt_ref, meta_ref, a_ref, w_hbm, o_ref, stage_ref, wb_ref, sem):
    i = pl.program_id(1)
    active = i < meta_ref[0]
    fk, tn = stage_ref.shape[1], stage_ref.shape[2]

    def part_copy(jj, e, kh):
        return pltpu.make_async_copy(
            w_hbm.at[e, pl.ds(kh * fk, fk), pl.ds(pl.multiple_of(jj * tn, tn), tn)],
            stage_ref.at[kh], sem.at[kh])

    def cast_part(kh):
        _cast_rows(stage_ref.at[kh], wb_ref.at[pl.ds(kh * fk, fk)], fk)

    _stream_segment_weights(te_ref, first_ref, nxt_ref, meta_ref, part_copy, cast_part, 2)

    @pl.when(active)
    def _():
        o_ref[...] = _dot(a_ref[...], wb_ref[...])

    @pl.when(jnp.logical_not(active))
    def _():
        o_ref[...] = jnp.zeros_like(o_ref)


def moe_down(act, w_down, sched, tn=512):
    rows, f = act.shape
    fk = f // 2
    vmem = (2 * (MOE_TM * f * 2 + MOE_TM * tn * 4) + 2 * fk * tn * 4 + f * tn * 2
            + MOE_TM * tn * 4 + (6 << 20))
    return pl.pallas_call(
        _moe_down_kernel,
        grid_spec=pltpu.PrefetchScalarGridSpec(
            num_scalar_prefetch=4,
            grid=(D_MODEL // tn, rows // MOE_TM),
            in_specs=[pl.BlockSpec((MOE_TM, f), lambda j, i, *s: (_active_row(j, i, *s), 0)),
                      pl.BlockSpec(memory_space=pl.ANY)],
            out_specs=pl.BlockSpec((MOE_TM, tn), lambda j, i, *s: (i, j)),
            scratch_shapes=[pltpu.VMEM((2, fk, tn), F32), pltpu.VMEM((f, tn), BF16),
                            pltpu.SemaphoreType.DMA((2,))]),
        out_shape=jax.ShapeDtypeStruct((rows, D_MODEL), F32),
        compiler_params=_params(2, vmem),
        name="moe_down",
    )(*sched, act, w_down)


def _moe_combine_kernel(pos_ref, h_ref, rw_ref, y_hbm, o_ref, b1_ref, b2_ref, sem):
    tc = b1_ref.shape[1]
    i = pl.program_id(0)
    slot = i % 2

    def start_rows(step, dst_slot):
        def issue(grp, carry):
            for k in range(ROW_DMA_UNROLL):
                r = grp * ROW_DMA_UNROLL + k
                t = step * tc + r
                _row_copy(y_hbm, pos_ref[2 * t], b1_ref.at[dst_slot], r, sem.at[dst_slot]).start(priority=0)
                _row_copy(y_hbm, pos_ref[2 * t + 1], b2_ref.at[dst_slot], r, sem.at[dst_slot]).start(priority=1)
            return carry
        lax.fori_loop(0, tc // ROW_DMA_UNROLL, issue, 0)

    @pl.when(i == 0)
    def _():
        start_rows(0, 0)

    @pl.when(i + 1 < pl.num_programs(0))
    def _():
        start_rows(i + 1, 1 - slot)

    def drain(r, carry):
        _row_copy(y_hbm, 0, b1_ref.at[slot], r, sem.at[slot]).wait()
        _row_copy(y_hbm, 0, b2_ref.at[slot], r, sem.at[slot]).wait()
        return carry
    lax.fori_loop(0, tc, drain, 0, unroll=8)
    o_ref[...] = h_ref[...] + rw_ref[:, 0:1] * b1_ref[slot] + rw_ref[:, 1:2] * b2_ref[slot]


def moe_combine(pos_flat, h, rw, y, tc=256):
    t, d = h.shape
    return pl.pallas_call(
        _moe_combine_kernel,
        grid_spec=pltpu.PrefetchScalarGridSpec(
            num_scalar_prefetch=1,
            grid=(t // tc,),
            in_specs=[pl.BlockSpec((tc, d), lambda i, p: (i, 0)),
                      pl.BlockSpec((tc, LANES), lambda i, p: (i, 0)),
                      pl.BlockSpec(memory_space=pl.ANY)],
            out_specs=pl.BlockSpec((tc, d), lambda i, p: (i, 0)),
            scratch_shapes=[pltpu.VMEM((2, tc, d), F32), pltpu.VMEM((2, tc, d), F32),
                            pltpu.SemaphoreType.DMA((2,))]),
        out_shape=jax.ShapeDtypeStruct((t, d), F32),
        compiler_params=_params(1, 32 << 20),
        name="moe_combine",
    )(pos_flat, h, rw, y)


def moe_block(h, g, w_router, w_up, w_down):
    t = h.shape[0]
    rows = 2 * t + N_EXPERTS * MOE_TM
    n_row_tiles = rows // MOE_TM
    ri, rw, cnt = moe_router(h, g, w_router)
    counts = cnt[0, :N_EXPERTS].astype(jnp.int32)
    padded = ((counts + MOE_TM - 1) // MOE_TM) * MOE_TM
    ends = jnp.cumsum(padded)
    starts = ends - padded
    pos = starts[ri[:, 0:2]] + ri[:, 2:4]
    pos_flat = pos.reshape(-1).astype(jnp.int32)
    n_tiles = (ends[-1] // MOE_TM).astype(jnp.int32)
    tile_first_row = jnp.arange(n_row_tiles, dtype=jnp.int32) * MOE_TM
    tile_e = jnp.sum((tile_first_row[:, None] >= ends[None, :]).astype(jnp.int32), axis=1)
    tile_e = jnp.minimum(tile_e, N_EXPERTS - 1)
    tile_idx = jnp.arange(n_row_tiles, dtype=jnp.int32)
    is_active = tile_idx < n_tiles
    tile_e = jnp.where(is_active, tile_e, tile_e[n_tiles - 1]).astype(jnp.int32)
    first = (is_active & ((tile_idx == 0) | (tile_e != jnp.roll(tile_e, 1)))).astype(jnp.int32)
    eidx = jnp.arange(N_EXPERTS, dtype=jnp.int32)
    later = (eidx[None, :] > eidx[:, None]) & (padded > 0)[None, :]
    nxt = jnp.min(jnp.where(later, eidx[None, :], N_EXPERTS), axis=1)
    nxt = jnp.where(nxt == N_EXPERTS, -1, nxt).astype(jnp.int32)
    sched = (tile_e, first, nxt[tile_e], jnp.stack([n_tiles, tile_e[0]]).astype(jnp.int32))

    xs = cast_bf16(moe_scatter(pos_flat, h, g, rows))
    act = moe_up(xs, w_up, sched)
    y = moe_down(act, w_down, sched)
    return moe_combine(pos_flat, h, rw, y)


IN_PROJ_MOVES = ((0, 2560, 0), (2560, 64, COL_KPE), (2624, 512, COL_SWQ), (3136, 128, COL_SWK),
                 (3264, 128, COL_SWV), (3392, 512, COL_MEQ))


def _in_proj_layout_kernel(wt_ref, o_ref):
    for src, width, dst in IN_PROJ_MOVES:
        o_ref[dst:dst + width, :] = wt_ref[src:src + width, :].astype(o_ref.dtype)
    tail = COL_KPE + MLA_ROPE
    o_ref[tail:, :] = jnp.zeros((IN_COLS_PAD - tail, o_ref.shape[1]), o_ref.dtype)


def _pad_in_proj(w_in, tc=256):
    wt = jnp.swapaxes(w_in, 1, 2)
    nl, n, d = wt.shape
    return pl.pallas_call(
        _in_proj_layout_kernel,
        grid=(nl, d // tc),
        in_specs=[pl.BlockSpec((None, n, tc), lambda l, i: (l, 0, i))],
        out_specs=pl.BlockSpec((None, IN_COLS_PAD, tc), lambda l, i: (l, 0, i)),
        out_shape=jax.ShapeDtypeStruct((nl, IN_COLS_PAD, d), BF16),
        compiler_params=_params(2, 32 << 20),
        name="in_proj_layout",
    )(wt)


def _pad_heads(x, n_heads, width):
    lead = x.shape[:-1]
    x = x.reshape(lead + (n_heads, width))
    x = jnp.pad(x, [(0, 0)] * len(lead) + [(0, 0), (0, MLA_PAD - width)])
    return x.reshape(lead + (n_heads * MLA_PAD,))


def kernel(x, mem, norm_mix, w_in, na_q_norm, na_k_norm, na_rpb, mla_cq_norm, mla_w_uq, mla_ckv_norm, mla_w_ukv, mla_q_norm, mla_k_norm, swa_q_norm, swa_k_norm, swa_sink, mem_norm, mem_w_kv, mem_q_norm, mem_k_norm, w_branch, w_gate, b_gate, w_o, norm_ffn, ffn_w_up, ffn_w_down, moe_router, moe_w_up, moe_w_down):
    batch, seq, d = x.shape
    assert (seq, d) == (SEQ, D_MODEL) and mem.shape[1] == MEM_LEN
    t = batch * seq
    depth = w_in.shape[0]
    h = x.reshape(t, d)
    mem2 = mem.reshape(batch * MEM_LEN, d)
    mla_tabs = _rope_tables(MLA_PAD, MLA_ROPE, MLA_NOPE)
    swa_tabs = _rope_tables(LANES, SWA_HD, 0)
    w_in_pad = _pad_in_proj(w_in)
    wq_pad = _pad_heads(mla_w_uq, MLA_HEADS, MLA_QK).astype(BF16)
    wkv_bf = mla_w_ukv.astype(BF16)

    for l in range(depth):
        u = rmsnorm_bf16(h, norm_mix[l])
        z = ws_matmul(u, w_in_pad, l, tm=1024, tn=1024, out_dtype=F32, name="in_proj")

        o_na = na_attention(z, _na_bias_table(na_rpb[l]), na_q_norm[l], na_k_norm[l], batch)

        q_mla, k_mla, v_mla = mla_prep(
            z, wq_pad[l], wkv_bf[l], mla_cq_norm[l], mla_ckv_norm[l],
            jnp.pad(mla_q_norm[l], (0, MLA_PAD - MLA_QK)).reshape(1, MLA_PAD),
            jnp.pad(mla_k_norm[l], (0, MLA_PAD - MLA_QK)).reshape(1, MLA_PAD), mla_tabs)
        o_mla = mla_attention(q_mla, k_mla, v_mla, batch)

        o_swa = swa_attention(z, swa_sink[l], swa_q_norm[l], swa_k_norm[l], swa_tabs, batch)

        memn = rmsnorm_bf16(mem2, mem_norm[l])
        memkv = ws_matmul(memn, mem_w_kv, l, tm=1024, tn=512, out_dtype=F32, name="mem_kv")
        o_mem = mem_attention(z, memkv, mem_q_norm[l], mem_k_norm[l], batch)

        merged = gated_merge(u, (o_na, o_mla, o_swa, o_mem), w_gate, w_branch, b_gate, l)
        h = ws_matmul(merged, w_o, l, tm=512, tn=1024, out_dtype=F32, residual=h, name="out_proj")

        if l % 2 == 0:
            hn = rmsnorm_bf16(h, norm_ffn[l])
            act = ws_swiglu(hn, ffn_w_up, l // 2, D_FF, tm=1024, tn=512, name="ffn_up")
            h = ws_matmul(act, ffn_w_down, l // 2, tm=512, tn=512, out_dtype=F32, residual=h, name="ffn_down")
        else:
            h = moe_block(h, norm_ffn[l], moe_router[l // 2], moe_w_up[l // 2], moe_w_down[l // 2])
    return h.reshape(batch, seq, d)
```
